```python
import math
import jax
import jax.numpy as jnp
from jax import lax
import numpy as np

D_MODEL = 2048
BATCH = 1
SEQ = 8192
DEPTH = 2

N_A_LAYERS = DEPTH // 2
N_B_LAYERS = DEPTH - N_A_LAYERS

GDN_QK_HEADS = 16
GDN_V_HEADS = 32
GDN_HEAD_DIM = 128
GDN_QK_DIM = GDN_QK_HEADS * GDN_HEAD_DIM
GDN_V_DIM = GDN_V_HEADS * GDN_HEAD_DIM
CONV_DIM = 2 * GDN_QK_DIM + GDN_V_DIM
GDN_PROJ = CONV_DIM + GDN_V_DIM + 2 * GDN_V_HEADS
CONV_K = 4
GDN_CHUNK = 64

FOX_HEADS = 16
FOX_KV_HEADS = 2
FOX_GROUP = FOX_HEADS // FOX_KV_HEADS
FOX_HEAD_DIM = 256
FOX_Q_DIM = FOX_HEADS * FOX_HEAD_DIM
FOX_KV_DIM = FOX_KV_HEADS * FOX_HEAD_DIM
KV_PROJ = 2 * FOX_KV_DIM + FOX_HEADS
Q_BLOCK = 128

FFN_HIDDEN = ((8 * D_MODEL // 3 + 255) // 256) * 256

NORM_EPS = 1e-6

kernel_name = "yoco_gdn_fox_adaln_trunk"


def rms_norm(x, w):
    xf = x.astype(jnp.float32)
    y = xf * lax.rsqrt(jnp.mean(xf * xf, axis=-1, keepdims=True) + NORM_EPS)
    return (y * w.astype(jnp.float32)).astype(x.dtype)


def modulate(h, shift, scale):
    return h * (1 + scale) + shift


def l2_normalize(x):
    xf = x.astype(jnp.float32)
    return xf * lax.rsqrt(jnp.sum(xf * xf, axis=-1, keepdims=True) + NORM_EPS)


def swiglu(h, w_in, w_out):
    gate, up = jnp.split(h @ w_in, 2, axis=-1)
    return (jax.nn.silu(gate) * up) @ w_out


def causal_conv(x, w):
    width = w.shape[0]
    length = x.shape[1]
    xp = jnp.pad(x, ((0, 0), (width - 1, 0), (0, 0)))
    return sum(xp[:, i:i + length] * w[i] for i in range(width))


def gated_delta_rule_chunked(q, k, v, g, beta):
    b, h, length, dk = q.shape
    dv = v.shape[-1]
    n = length // GDN_CHUNK
    blk = lambda t: t.reshape(b, h, n, GDN_CHUNK, *t.shape[3:])
    q = blk(q) * dk ** -0.5
    k = blk(k)
    v = blk(v)
    beta = blk(beta)
    g = jnp.cumsum(blk(g), axis=-1)
    causal = jnp.tril(jnp.ones((GDN_CHUNK, GDN_CHUNK), dtype=bool))
    strict = jnp.tril(jnp.ones((GDN_CHUNK, GDN_CHUNK), dtype=bool), k=-1)
    decay = jnp.exp(jnp.where(causal, g[..., :, None] - g[..., None, :], -jnp.inf))
    k_beta = k * beta[..., None]
    a_strict = jnp.where(strict, jnp.einsum('bhnid,bhnjd->bhnij', k_beta, k) * decay, 0.0)
    eye = jnp.eye(GDN_CHUNK, dtype=q.dtype)
    rhs = jnp.concatenate([v * beta[..., None], k_beta * jnp.exp(g)[..., None]], axis=-1)
    sol = lax.linalg.triangular_solve(a_strict + eye, rhs, left_side=True, lower=True,
                                      unit_diagonal=True)
    u, w = sol[..., :dv], sol[..., dv:]
    attn = jnp.where(causal, jnp.einsum('bhnid,bhnjd->bhnij', q, k) * decay, 0.0)
    g_last = g[..., -1]
    q_dec = q * jnp.exp(g)[..., None]
    k_dec = k * jnp.exp(g_last[..., None] - g)[..., None]

    def step(state, xs):
        q_c, k_c, u_c, w_c, attn_c, gl_c = xs
        v_new = u_c - jnp.einsum('bhik,bhkv->bhiv', w_c, state)
        o = jnp.einsum('bhik,bhkv->bhiv', q_c, state) + jnp.einsum('bhij,bhjv->bhiv', attn_c, v_new)
        state = state * jnp.exp(gl_c)[..., None, None] + jnp.einsum('bhik,bhiv->bhkv', k_c, v_new)
        return state, o

    xs = tuple(jnp.moveaxis(t, 2, 0) for t in (q_dec, k_dec, u, w, attn, g_last))
    s0 = jnp.zeros((b, h, dk, dv), jnp.float32)
    _, o = lax.scan(step, s0, xs)
    return jnp.moveaxis(o, 0, 2).reshape(b, h, length, dv)


def gated_deltanet(h, w_in, conv_w, a_log, dt_bias, norm_w, w_out):
    b, length, _ = h.shape
    qkv, z, beta_logit, a = jnp.split(
        h @ w_in, [CONV_DIM, CONV_DIM + GDN_V_DIM, CONV_DIM + GDN_V_DIM + GDN_V_HEADS], axis=-1)
    qkv = jax.nn.silu(causal_conv(qkv, conv_w))
    q, k, v = jnp.split(qkv, [GDN_QK_DIM, 2 * GDN_QK_DIM], axis=-1)
    rep = GDN_V_HEADS // GDN_QK_HEADS
    heads = lambda t, nh: t.reshape(b, length, nh, GDN_HEAD_DIM)
    q = jnp.repeat(l2_normalize(heads(q, GDN_QK_HEADS)), rep, axis=2)
    k = jnp.repeat(l2_normalize(heads(k, GDN_QK_HEADS)), rep, axis=2)
    v = heads(v, GDN_V_HEADS).astype(jnp.float32)
    beta = jax.nn.sigmoid(beta_logit.astype(jnp.float32))
    g = -jnp.exp(a_log.astype(jnp.float32)) * jax.nn.softplus(
        a.astype(jnp.float32) + dt_bias.astype(jnp.float32))
    tr = lambda t: jnp.swapaxes(t, 1, 2)
    o = gated_delta_rule_chunked(tr(q), tr(k), tr(v), tr(g), tr(beta))
    o = rms_norm(tr(o), norm_w) * jax.nn.silu(heads(z, GDN_V_HEADS).astype(jnp.float32))
    return o.reshape(b, length, GDN_V_DIM).astype(h.dtype) @ w_out


def shared_kv(x, cond, ada_w, ada_b, norm_w, w_kv, k_norm_w, forget_b):
    b, length, _ = x.shape
    shift, scale = (m[:, None, :] for m in jnp.split(cond @ ada_w + ada_b, 2, axis=-1))
    h = modulate(rms_norm(x, norm_w), shift, scale)
    k, v, f_logit = jnp.split(h @ w_kv, [FOX_KV_DIM, 2 * FOX_KV_DIM], axis=-1)
    k = rms_norm(k.reshape(b, length, FOX_KV_HEADS, FOX_HEAD_DIM), k_norm_w)
    v = v.reshape(b, length, FOX_KV_HEADS, FOX_HEAD_DIM)
    log_f = jax.nn.log_sigmoid(f_logit.astype(jnp.float32) + forget_b.astype(jnp.float32))
    f_cum = jnp.cumsum(log_f, axis=1).reshape(b, length, FOX_KV_HEADS, FOX_GROUP)
    return k, v, f_cum


def blocked_forgetting_softmax(q, k, v, f_cum):
    b, length, kvh, grp, hd = q.shape
    nb = length // Q_BLOCK
    qb = jnp.swapaxes(q.reshape(b, nb, Q_BLOCK, kvh, grp, hd), 0, 1)
    fb = jnp.swapaxes(f_cum.reshape(b, nb, Q_BLOCK, kvh, grp), 0, 1)
    f_k = jnp.transpose(f_cum, (0, 2, 3, 1))
    k_pos = jnp.arange(length)
    scale = hd ** -0.5

    def one_block(args):
        i, q_i, f_i = args
        s = jnp.einsum('bqhgd,bkhd->bhgqk', q_i, k, preferred_element_type=jnp.float32) * scale
        s = s + jnp.transpose(f_i, (0, 2, 3, 1))[..., None] - f_k[..., None, :]
        q_pos = i * Q_BLOCK + jnp.arange(Q_BLOCK)
        s = jnp.where(k_pos[None, :] <= q_pos[:, None], s, -jnp.inf)
        p = jax.nn.softmax(s, axis=-1).astype(v.dtype)
        return jnp.einsum('bhgqk,bkhd->bqhgd', p, v)

    o = lax.map(one_block, (jnp.arange(nb), qb, fb))
    return jnp.swapaxes(o, 0, 1).reshape(b, length, kvh, grp, hd)


def forgetting_attention(h, k, v, f_cum, w_in, q_norm_w, w_out):
    b, length, _ = h.shape
    q, gate = jnp.split(h @ w_in, 2, axis=-1)
    q = rms_norm(q.reshape(b, length, FOX_HEADS, FOX_HEAD_DIM), q_norm_w)
    q = q.reshape(b, length, FOX_KV_HEADS, FOX_GROUP, FOX_HEAD_DIM)
    o = blocked_forgetting_softmax(q, k, v, f_cum).reshape(b, length, FOX_Q_DIM)
    return (o * jax.nn.sigmoid(gate)) @ w_out


def setup_inputs(seed: int = 0) -> dict:
    key = jax.random.key(seed)
    ks = iter(jax.random.split(key, 40))
    f32 = jnp.float32

    def dense(shape, fan_in, s=1.0):
        return (s * fan_in ** -0.5) * jax.random.normal(next(ks), shape, f32)

    def gain(shape):
        return 1.0 + 0.02 * jax.random.normal(next(ks), shape, f32)

    def small(shape, s):
        return s * jax.random.normal(next(ks), shape, f32)

    d = D_MODEL
    x = jax.random.normal(next(ks), (BATCH, SEQ, d), f32)
    c = jax.random.normal(next(ks), (BATCH, d), f32)
    ada_w = dense((DEPTH, d, 6 * d), d, 0.5)
    ada_b = small((DEPTH, 6 * d), 0.02)
    norm_mix = gain((DEPTH, d))
    norm_ffn = gain((DEPTH, d))
    ffn_w_in = dense((DEPTH, d, 2 * FFN_HIDDEN), d)
    ffn_w_out = dense((DEPTH, FFN_HIDDEN, d), FFN_HIDDEN)
    gdn_w_in = dense((N_A_LAYERS, d, GDN_PROJ), d)
    gdn_conv = dense((N_A_LAYERS, CONV_K, CONV_DIM), CONV_K)
    gdn_a_log = jnp.log(jax.random.uniform(next(ks), (N_A_LAYERS, GDN_V_HEADS), f32, 1.0, 16.0))
    dt = jnp.exp(jax.random.uniform(next(ks), (N_A_LAYERS, GDN_V_HEADS), f32,
                                    math.log(1e-3), math.log(1e-1)))
    gdn_dt_bias = dt + jnp.log(-jnp.expm1(-dt))
    gdn_norm = gain((N_A_LAYERS, GDN_HEAD_DIM))
    gdn_w_out = dense((N_A_LAYERS, GDN_V_DIM, d), GDN_V_DIM)
    kv_ada_w = dense((d, 2 * d), d, 0.5)
    kv_ada_b = small((2 * d,), 0.02)
    kv_norm = gain((d,))
    kv_w = dense((d, KV_PROJ), d)
    k_norm = gain((FOX_HEAD_DIM,))
    forget_b = jax.random.uniform(next(ks), (FOX_HEADS,), f32, 1.0, 6.0)
    fox_w_in = dense((N_B_LAYERS, d, 2 * FOX_Q_DIM), d)
    q_norm = gain((N_B_LAYERS, FOX_HEAD_DIM))
    fox_w_out = dense((N_B_LAYERS, FOX_Q_DIM, d), FOX_Q_DIM)
    out_ada_w = dense((d, 2 * d), d, 0.5)
    out_ada_b = small((2 * d,), 0.02)
    out_norm = gain((d,))
    return {"x": x, "c": c, "ada_w": ada_w, "ada_b": ada_b, "norm_mix": norm_mix,
            "norm_ffn": norm_ffn, "ffn_w_in": ffn_w_in, "ffn_w_out": ffn_w_out,
            "gdn_w_in": gdn_w_in, "gdn_conv": gdn_conv, "gdn_a_log": gdn_a_log,
            "gdn_dt_bias": gdn_dt_bias, "gdn_norm": gdn_norm, "gdn_w_out": gdn_w_out,
            "kv_ada_w": kv_ada_w, "kv_ada_b": kv_ada_b, "kv_norm": kv_norm, "kv_w": kv_w,
            "k_norm": k_norm, "forget_b": forget_b, "fox_w_in": fox_w_in, "q_norm": q_norm,
            "fox_w_out": fox_w_out, "out_ada_w": out_ada_w, "out_ada_b": out_ada_b,
            "out_norm": out_norm}


def reference(x, c, ada_w, ada_b, norm_mix, norm_ffn, ffn_w_in, ffn_w_out, gdn_w_in, gdn_conv,
              gdn_a_log, gdn_dt_bias, gdn_norm, gdn_w_out, kv_ada_w, kv_ada_b, kv_norm, kv_w,
              k_norm, forget_b, fox_w_in, q_norm, fox_w_out, out_ada_w, out_ada_b, out_norm):
    cond = jax.nn.silu(c)
    k_sh = v_sh = f_sh = None
    for layer in range(DEPTH):
        sh_m, sc_m, g_m, sh_f, sc_f, g_f = (
            m[:, None, :] for m in jnp.split(cond @ ada_w[layer] + ada_b[layer], 6, axis=-1))
        h = modulate(rms_norm(x, norm_mix[layer]), sh_m, sc_m)
        if layer < N_A_LAYERS:
            y = gated_deltanet(h, gdn_w_in[layer], gdn_conv[layer], gdn_a_log[layer],
                               gdn_dt_bias[layer], gdn_norm[layer], gdn_w_out[layer])
        else:
            if layer == N_A_LAYERS:
                k_sh, v_sh, f_sh = shared_kv(x, cond, kv_ada_w, kv_ada_b, kv_norm, kv_w,
                                             k_norm, forget_b)
            j = layer - N_A_LAYERS
            y = forgetting_attention(h, k_sh, v_sh, f_sh, fox_w_in[j], q_norm[j], fox_w_out[j])
        x = x + g_m * y
        h = modulate(rms_norm(x, norm_ffn[layer]), sh_f, sc_f)
        x = x + g_f * swiglu(h, ffn_w_in[layer], ffn_w_out[layer])
    sh_o, sc_o = (m[:, None, :] for m in jnp.split(cond @ out_ada_w + out_ada_b, 2, axis=-1))
    return modulate(rms_norm(x, out_norm), sh_o, sc_o)
```

```python
import functools
import math

import jax
import jax.numpy as jnp
from jax import lax
from jax.experimental import pallas as pl
from jax.experimental.pallas import tpu as pltpu

F32 = jnp.float32
BF16 = jnp.bfloat16
NORM_EPS = 1e-6

V7X_VMEM_BYTES = 64 * 1024 * 1024
VMEM_LIMIT_BYTES = 56 * 1024 * 1024
LANES = 128
SUBLANES = 8

GDN_CHUNK = 64
GDN_HEAD_DIM = 128
GDN_V_HEADS = 32
GDN_QK_HEADS = 16
CONV_K = 4
FOX_HEAD_DIM = 256
FOX_KV_HEADS = 2
FOX_GROUP = 8


def _cparams(sem):
    return pltpu.CompilerParams(dimension_semantics=sem, vmem_limit_bytes=VMEM_LIMIT_BYTES)


def _silu(x):
    return x * jax.nn.sigmoid(x)


def _softplus(x):
    return jnp.maximum(x, 0.0) + jnp.log1p(jnp.exp(-jnp.abs(x)))


def _log_sigmoid(x):
    return jnp.minimum(x, 0.0) - jnp.log1p(jnp.exp(-jnp.abs(x)))


def _norm_mod(x, nw, sh, sc):
    ms = jnp.mean(x * x, axis=-1, keepdims=True)
    y = x * lax.rsqrt(ms + NORM_EPS) * nw
    return y * (1.0 + sc) + sh


NORM_ROWS = 256


def _norm_mod_rows(dst_ref, src_fn, nw, sh, sc):
    tm = dst_ref.shape[0]
    step = min(NORM_ROWS, tm)
    for r in range(0, tm, step):
        rows = slice(r, r + step)
        dst_ref[rows, :] = _norm_mod(src_fn(rows), nw, sh, sc).astype(dst_ref.dtype)


def _adaln_kernel(c_ref, w_ref, b_ref, o_ref):
    cond = _silu(c_ref[...])
    o_ref[...] = jnp.sum(w_ref[...] * cond, axis=0, keepdims=True) + b_ref[...]


def adaln(c_col, w3, b3, layer, tn=1024):
    _, d, n = w3.shape
    tn = min(tn, n)
    return pl.pallas_call(
        _adaln_kernel,
        grid=(n // tn,),
        in_specs=[
            pl.BlockSpec((d, 1), lambda j: (0, 0)),
            pl.BlockSpec((None, d, tn), lambda j: (layer, 0, j)),
            pl.BlockSpec((None, 1, tn), lambda j: (layer, 0, j)),
        ],
        out_specs=pl.BlockSpec((1, tn), lambda j: (0, j)),
        out_shape=jax.ShapeDtypeStruct((1, n), F32),
        compiler_params=_cparams(("arbitrary",)),
        name="adaln",
    )(c_col, w3, b3)


def _nm_proj_kernel(x_ref, nw_ref, sh_ref, sc_ref, w_ref, ws_ref, o_ref, os_ref, h_ref):
    @pl.when(pl.program_id(1) == 0)
    def _():
        _norm_mod_rows(h_ref, lambda rows: x_ref[rows, :], nw_ref[...], sh_ref[...], sc_ref[...])
        os_ref[...] = jnp.dot(h_ref[...], ws_ref[...], preferred_element_type=F32)

    o_ref[...] = jnp.dot(h_ref[...], w_ref[...], preferred_element_type=F32)


def nm_proj(x, nw, sh, sc, w, ws, tm=1024, tn=1024):
    l, d = x.shape
    n = w.shape[1]
    tm, tn = min(tm, l), min(tn, n)
    row = lambda i, j: (0, 0)
    return pl.pallas_call(
        _nm_proj_kernel,
        grid=(l // tm, n // tn),
        in_specs=[
            pl.BlockSpec((tm, d), lambda i, j: (i, 0)),
            pl.BlockSpec((1, d), row), pl.BlockSpec((1, d), row), pl.BlockSpec((1, d), row),
            pl.BlockSpec((d, tn), lambda i, j: (0, j)),
            pl.BlockSpec((d, LANES), row),
        ],
        out_specs=[pl.BlockSpec((tm, tn), lambda i, j: (i, j)),
                   pl.BlockSpec((tm, LANES), lambda i, j: (i, 0))],
        out_shape=[jax.ShapeDtypeStruct((l, n), F32), jax.ShapeDtypeStruct((l, LANES), F32)],
        scratch_shapes=[pltpu.VMEM((tm, d), BF16)],
        compiler_params=_cparams(("arbitrary", "arbitrary")),
        name="nm_proj",
    )(x, nw, sh, sc, w, ws)


def _proj_res_kernel(a_ref, w_ref, x_ref, g_ref, o_ref):
    y = jnp.dot(a_ref[...], w_ref[...], preferred_element_type=F32)
    o_ref[...] = x_ref[...] + g_ref[...] * y


def proj_res(a, w, x, g, tm=1024, tn=512):
    l, k = a.shape
    d = w.shape[1]
    tm, tn = min(tm, l), min(tn, d)
    return pl.pallas_call(
        _proj_res_kernel,
        grid=(l // tm, d // tn),
        in_specs=[
            pl.BlockSpec((tm, k), lambda i, j: (i, 0)),
            pl.BlockSpec((k, tn), lambda i, j: (0, j)),
            pl.BlockSpec((tm, tn), lambda i, j: (i, j)),
            pl.BlockSpec((1, tn), lambda i, j: (0, j)),
        ],
        out_specs=pl.BlockSpec((tm, tn), lambda i, j: (i, j)),
        out_shape=jax.ShapeDtypeStruct((l, d), F32),
        compiler_params=_cparams(("arbitrary", "arbitrary")),
        name="proj_res",
    )(a, w, x, g)


def _ffn_kernel(x_ref, nw_ref, sh_ref, sc_ref, g_ref, wg_ref, wu_ref, wo_ref, fnw_ref, fsh_ref,
                fsc_ref, o_ref, h_ref, acc_ref, *, final):
    j = pl.program_id(1)

    @pl.when(j == 0)
    def _():
        _norm_mod_rows(h_ref, lambda rows: x_ref[rows, :], nw_ref[...], sh_ref[...], sc_ref[...])
        acc_ref[...] = jnp.zeros_like(acc_ref)

    h = h_ref[...]
    gate = jnp.dot(h, wg_ref[...], preferred_element_type=F32)
    up = jnp.dot(h, wu_ref[...], preferred_element_type=F32)
    act = (_silu(gate) * up).astype(BF16)
    acc_ref[...] += jnp.dot(act, wo_ref[...], preferred_element_type=F32)

    @pl.when(j == pl.num_programs(1) - 1)
    def _():
        res = lambda rows: x_ref[rows, :] + g_ref[...] * acc_ref[rows, :]
        if final:
            _norm_mod_rows(o_ref, res, fnw_ref[...], fsh_ref[...], fsc_ref[...])
        else:
            o_ref[...] = res(slice(None))


def ffn(x, nw, sh, sc, g, w_in, w_out, fnw, fsh, fsc, final, tm=512, th=512):
    l, d = x.shape
    hdim = w_out.shape[0]
    tm, th = min(tm, l), min(th, hdim)
    nh = hdim // th
    row = lambda i, j: (0, 0)
    vec = pl.BlockSpec((1, d), row)
    return pl.pallas_call(
        functools.partial(_ffn_kernel, final=final),
        grid=(l // tm, nh),
        in_specs=[
            pl.BlockSpec((tm, d), lambda i, j: (i, 0)),
            vec, vec, vec, vec,
            pl.BlockSpec((d, th), lambda i, j: (0, j)),
            pl.BlockSpec((d, th), lambda i, j: (0, j + nh)),
            pl.BlockSpec((th, d), lambda i, j: (j, 0)),
            vec, vec, vec,
        ],
        out_specs=pl.BlockSpec((tm, d), lambda i, j: (i, 0)),
        out_shape=jax.ShapeDtypeStruct((l, d), F32),
        scratch_shapes=[pltpu.VMEM((tm, d), BF16), pltpu.VMEM((tm, d), F32)],
        compiler_params=_cparams(("arbitrary", "arbitrary")),
        name="ffn",
    )(x, nw, sh, sc, g, w_in, w_in, w_out, fnw, fsh, fsc)


def _gdn_gates_kernel(s_ref, alog_ref, dt_ref, gt_ref, gh_ref):
    x = s_ref[...]
    tm = x.shape[0]
    lane = lax.broadcasted_iota(jnp.int32, x.shape, 1)
    beta = jax.nn.sigmoid(x)
    g = -jnp.exp(alog_ref[...]) * _softplus(x + dt_ref[...])
    g = jnp.where((lane >= GDN_V_HEADS) & (lane < 2 * GDN_V_HEADS), g, 0.0)
    r = lax.broadcasted_iota(jnp.int32, (tm, tm), 0)
    c = lax.broadcasted_iota(jnp.int32, (tm, tm), 1)
    same_chunk = (r // GDN_CHUNK) == (c // GDN_CHUNK)
    tril = jnp.where((r >= c) & same_chunk, 1.0, 0.0).astype(F32)
    gcum = jnp.dot(tril, g, preferred_element_type=F32, precision=lax.Precision.HIGHEST)
    gt = jnp.where(lane < GDN_V_HEADS, beta, gcum)
    gt_ref[...] = gt
    gh_ref[...] = gt.T


def gdn_gates(small, alog_row, dt_row, tm=512):
    l = small.shape[0]
    tm = min(tm, l)
    row = lambda i: (0, 0)
    return pl.pallas_call(
        _gdn_gates_kernel,
        grid=(l // tm,),
        in_specs=[pl.BlockSpec((tm, LANES), lambda i: (i, 0)),
                  pl.BlockSpec((1, LANES), row), pl.BlockSpec((1, LANES), row)],
        out_specs=[pl.BlockSpec((tm, LANES), lambda i: (i, 0)),
                   pl.BlockSpec((LANES, tm), lambda i: (0, i))],
        out_shape=[jax.ShapeDtypeStruct((l, LANES), F32), jax.ShapeDtypeStruct((LANES, l), F32)],
        compiler_params=_cparams(("arbitrary",)),
        name="gdn_gates",
    )(small, alog_row, dt_row)


def _unit_lower_inverse_minus_eye(a):
    c = a.shape[0]
    p = -a
    r = p
    for _ in range(int(math.log2(c)) - 1):
        pb = p.astype(BF16)
        p = jnp.dot(pb, pb, preferred_element_type=F32)
        r = r + p + jnp.dot(r.astype(BF16), p.astype(BF16), preferred_element_type=F32)
    return r


def _gdn_kernel(q_ref, k_ref, v_ref, z_ref, wq_ref, wk_ref, wv_ref, gt_ref, gh_ref, nw_ref,
                o_ref, xe_ref, s_ref):
    p = pl.program_id(0)
    b = pl.program_id(1)
    tb = q_ref.shape[0]
    hd = GDN_HEAD_DIM
    ck = GDN_CHUNK
    pad = SUBLANES

    @pl.when(b == 0)
    def _():
        xe_ref[0:pad, :] = jnp.zeros((pad, 4 * hd), F32)
        s_ref[...] = jnp.zeros_like(s_ref)

    xe_ref[pad:pad + tb, 0:hd] = q_ref[...]
    xe_ref[pad:pad + tb, hd:2 * hd] = k_ref[...]
    xe_ref[pad:pad + tb, 2 * hd:4 * hd] = v_ref[...]

    w = jnp.concatenate([wq_ref[...], wk_ref[...], wv_ref[...]], axis=1)
    conv = w[CONV_K - 1:CONV_K, :] * xe_ref[pad:pad + tb, :]
    for i in range(CONV_K - 1):
        off = pad - (CONV_K - 1) + i
        conv = conv + w[i:i + 1, :] * xe_ref[off:off + tb, :]
    xe_ref[0:pad, :] = xe_ref[tb:tb + pad, :]

    y = _silu(conv)
    q = y[:, 0:hd]
    k = y[:, hd:2 * hd]
    q = q * (lax.rsqrt(jnp.sum(q * q, axis=-1, keepdims=True) + NORM_EPS) * hd ** -0.5)
    k = k * lax.rsqrt(jnp.sum(k * k, axis=-1, keepdims=True) + NORM_EPS)

    gt = pltpu.roll(gt_ref[...], lax.rem(LANES - 2 * p, LANES), axis=1)
    r_idx = lax.broadcasted_iota(jnp.int32, (ck, ck), 0)
    c_idx = lax.broadcasted_iota(jnp.int32, (ck, ck), 1)
    causal = r_idx >= c_idx
    strict = r_idx > c_idx
    nw = nw_ref[...]

    for e in range(2):
        beta_col = gt[:, e:e + 1]
        gc_col = gt[:, GDN_V_HEADS + e:GDN_V_HEADS + e + 1]
        beta_row = gh_ref[pl.ds(2 * p + e, 1), :]
        gc_row = gh_ref[pl.ds(GDN_V_HEADS + 2 * p + e, 1), :]
        v = y[:, (2 + e) * hd:(3 + e) * hd]
        state = s_ref[e]
        for ci in range(tb // ck):
            r0 = ci * ck
            qc, kc, vc = q[r0:r0 + ck], k[r0:r0 + ck], v[r0:r0 + ck]
            kcb = kc.astype(BF16)
            qk_kk = lax.dot_general(jnp.concatenate([qc, kc], axis=0).astype(BF16), kcb,
                                    (((1,), (1,)), ((), ())), preferred_element_type=F32)
            bc, gcc = beta_col[r0:r0 + ck], gc_col[r0:r0 + ck]
            br, gcr = beta_row[:, r0:r0 + ck], gc_row[:, r0:r0 + ck]
            decay = jnp.exp(jnp.where(causal, gcc - gcr, -jnp.inf))
            a = jnp.where(strict, qk_kk[ck:] * decay * bc, 0.0)
            attn = qk_kk[:ck] * decay
            rb = _unit_lower_inverse_minus_eye(a) * br
            eg = jnp.exp(gcc)
            u = bc * vc + jnp.dot(rb.astype(BF16), vc.astype(BF16), preferred_element_type=F32)
            wk = (bc * eg) * kc + jnp.dot((rb * jnp.exp(gcr)).astype(BF16), kcb,
                                          preferred_element_type=F32)
            g_last = gcc[ck - 1:ck, :]
            wq_s = jnp.dot(jnp.concatenate([wk, qc * eg], axis=0).astype(BF16),
                           state.astype(BF16), preferred_element_type=F32)
            v_new = u - wq_s[:ck]
            v_new_b = v_new.astype(BF16)
            o = wq_s[ck:] + jnp.dot(attn.astype(BF16), v_new_b, preferred_element_type=F32)
            k_dec = (kc * jnp.exp(g_last - gcc)).astype(BF16)
            state = state * jnp.exp(g_last) + lax.dot_general(
                k_dec, v_new_b, (((0,), (0,)), ((), ())), preferred_element_type=F32)
            zc = z_ref[r0:r0 + ck, e * hd:(e + 1) * hd]
            on = o * lax.rsqrt(jnp.mean(o * o, axis=-1, keepdims=True) + NORM_EPS) * nw
            o_ref[r0:r0 + ck, e * hd:(e + 1) * hd] = (on * _silu(zc)).astype(o_ref.dtype)
        s_ref[e] = state


def gdn_core(proj, conv_w, gt, gh, norm_w, tb=256):
    l = proj.shape[0]
    hd = GDN_HEAD_DIM
    tb = min(tb, l)
    nqk = GDN_QK_HEADS
    return pl.pallas_call(
        _gdn_kernel,
        grid=(nqk, l // tb),
        in_specs=[
            pl.BlockSpec((tb, hd), lambda p, b: (b, p)),
            pl.BlockSpec((tb, hd), lambda p, b: (b, nqk + p)),
            pl.BlockSpec((tb, 2 * hd), lambda p, b: (b, nqk + p)),
            pl.BlockSpec((tb, 2 * hd), lambda p, b: (b, 2 * nqk + p)),
            pl.BlockSpec((CONV_K, hd), lambda p, b: (0, p)),
            pl.BlockSpec((CONV_K, hd), lambda p, b: (0, nqk + p)),
            pl.BlockSpec((CONV_K, 2 * hd), lambda p, b: (0, nqk + p)),
            pl.BlockSpec((tb, LANES), lambda p, b: (b, 0)),
            pl.BlockSpec((LANES, tb), lambda p, b: (0, b)),
            pl.BlockSpec((1, hd), lambda p, b: (0, 0)),
        ],
        out_specs=pl.BlockSpec((tb, 2 * hd), lambda p, b: (b, p)),
        out_shape=jax.ShapeDtypeStruct((l, GDN_V_HEADS * hd), BF16),
        scratch_shapes=[pltpu.VMEM((tb + SUBLANES, 4 * hd), F32),
                        pltpu.VMEM((2, hd, hd), F32)],
        compiler_params=_cparams(("arbitrary", "arbitrary")),
        name="gdn_core",
    )(proj, proj, proj, proj, conv_w, conv_w, conv_w, gt, gh, norm_w)


def _kv_prep_kernel(kv_ref, fl_ref, knw_ref, fb_ref, k_ref, v_ref, fh_ref, carry_ref):
    hd = FOX_HEAD_DIM
    nkv = FOX_KV_HEADS
    tm = kv_ref.shape[0]

    @pl.when(pl.program_id(0) == 0)
    def _():
        carry_ref[...] = jnp.zeros_like(carry_ref)

    for h in range(nkv):
        kh = kv_ref[:, h * hd:(h + 1) * hd]
        ms = jnp.mean(kh * kh, axis=-1, keepdims=True)
        k_ref[:, h * hd:(h + 1) * hd] = (kh * lax.rsqrt(ms + NORM_EPS) * knw_ref[...]).astype(BF16)
    v_ref[...] = kv_ref[:, nkv * hd:2 * nkv * hd].astype(BF16)

    log_f = _log_sigmoid(fl_ref[...] + fb_ref[...])
    r = lax.broadcasted_iota(jnp.int32, (tm, tm), 0)
    c = lax.broadcasted_iota(jnp.int32, (tm, tm), 1)
    tril = jnp.where(r >= c, 1.0, 0.0).astype(F32)
    cs = jnp.dot(tril, log_f, preferred_element_type=F32,
                 precision=lax.Precision.HIGHEST) + carry_ref[...]
    carry_ref[...] = cs[tm - 1:tm, :]
    fh_ref[...] = cs.T[0:fh_ref.shape[0], :]


def kv_prep(kv, fl, knw, fb_row, tm=512):
    l = kv.shape[0]
    tm = min(tm, l)
    kvd = FOX_KV_HEADS * FOX_HEAD_DIM
    nh = FOX_KV_HEADS * FOX_GROUP
    row = lambda i: (0, 0)
    return pl.pallas_call(
        _kv_prep_kernel,
        grid=(l // tm,),
        in_specs=[pl.BlockSpec((tm, 2 * kvd), lambda i: (i, 0)),
                  pl.BlockSpec((tm, LANES), lambda i: (i, 0)),
                  pl.BlockSpec((1, FOX_HEAD_DIM), row), pl.BlockSpec((1, LANES), row)],
        out_specs=[pl.BlockSpec((tm, kvd), lambda i: (i, 0)),
                   pl.BlockSpec((tm, kvd), lambda i: (i, 0)),
                   pl.BlockSpec((nh, tm), lambda i: (0, i))],
        out_shape=[jax.ShapeDtypeStruct((l, kvd), BF16), jax.ShapeDtypeStruct((l, kvd), BF16),
                   jax.ShapeDtypeStruct((nh, l), F32)],
        scratch_shapes=[pltpu.VMEM((1, LANES), F32)],
        compiler_params=_cparams(("arbitrary",)),
        name="kv_prep",
    )(kv, fl, knw, fb_row)


def _fox_kernel(q_ref, gate_ref, k_ref, v_ref, fh_ref, qnw_ref, o_ref):
    i = pl.program_id(1)
    tq = q_ref.shape[0]
    hd = FOX_HEAD_DIM
    scale = hd ** -0.5
    r_idx = lax.broadcasted_iota(jnp.int32, (tq, tq), 0)
    c_idx = lax.broadcasted_iota(jnp.int32, (tq, tq), 1)
    causal = c_idx <= r_idx
    nt = (((1,), (1,)), ((), ()))

    for g in range(FOX_GROUP):
        qh = q_ref[:, g * hd:(g + 1) * hd]
        ms = jnp.mean(qh * qh, axis=-1, keepdims=True)
        qn = (qh * (lax.rsqrt(ms + NORM_EPS) * scale) * qnw_ref[...]).astype(BF16)

        def step(j, carry, masked):
            m, l, acc = carry
            start = pl.multiple_of(j * tq, tq)
            kj = k_ref[pl.ds(start, tq), :]
            vj = v_ref[pl.ds(start, tq), :]
            fk = fh_ref[g:g + 1, pl.ds(start, tq)]
            s = lax.dot_general(qn, kj, nt, preferred_element_type=F32) - fk
            if masked:
                s = jnp.where(causal, s, -jnp.inf)
            m_new = jnp.maximum(m, jnp.max(s, axis=-1, keepdims=True))
            alpha = jnp.exp(m - m_new)
            pr = jnp.exp(s - m_new)
            l = alpha * l + jnp.sum(pr, axis=-1, keepdims=True)
            acc = alpha * acc + jnp.dot(pr.astype(BF16), vj, preferred_element_type=F32)
            return m_new, l, acc

        init = (jnp.full((tq, 1), -jnp.inf, F32), jnp.zeros((tq, 1), F32),
                jnp.zeros((tq, hd), F32))
        carry = lax.fori_loop(0, i, functools.partial(step, masked=False), init)
        _, l, acc = step(i, carry, True)
        gate = gate_ref[:, g * hd:(g + 1) * hd]
        o_ref[:, g * hd:(g + 1) * hd] = (acc / l * jax.nn.sigmoid(gate)).astype(o_ref.dtype)


def fox_attention(qg, k, v, fh, qnw, tq=256):
    l = qg.shape[0]
    hd = FOX_HEAD_DIM
    gw = FOX_GROUP * hd
    tq = min(tq, l)
    return pl.pallas_call(
        _fox_kernel,
        grid=(FOX_KV_HEADS, l // tq),
        in_specs=[
            pl.BlockSpec((tq, gw), lambda h, i: (i, h)),
            pl.BlockSpec((tq, gw), lambda h, i: (i, FOX_KV_HEADS + h)),
            pl.BlockSpec((l, hd), lambda h, i: (0, h)),
            pl.BlockSpec((l, hd), lambda h, i: (0, h)),
            pl.BlockSpec((FOX_GROUP, l), lambda h, i: (h, 0)),
            pl.BlockSpec((1, hd), lambda h, i: (0, 0)),
        ],
        out_specs=pl.BlockSpec((tq, gw), lambda h, i: (i, h)),
        out_shape=jax.ShapeDtypeStruct((l, FOX_KV_HEADS * gw), BF16),
        compiler_params=_cparams(("arbitrary", "arbitrary")),
        name="fox_attention",
    )(qg, qg, k, v, fh, qnw)


def _pad_lanes(w):
    return jnp.pad(w, ((0, 0), (0, LANES - w.shape[1])))


def kernel(x, c, ada_w, ada_b, norm_mix, norm_ffn, ffn_w_in, ffn_w_out, gdn_w_in, gdn_conv,
           gdn_a_log, gdn_dt_bias, gdn_norm, gdn_w_out, kv_ada_w, kv_ada_b, kv_norm, kv_w, k_norm,
           forget_b, fox_w_in, q_norm, fox_w_out, out_ada_w, out_ada_b, out_norm):
    bsz, l, d = x.shape
    assert bsz == 1 and ada_w.shape[0] == 2 and gdn_w_in.shape[0] == 1 and fox_w_in.shape[0] == 1
    xs = x.reshape(l, d)
    c_col = c.reshape(d, 1)
    row = lambda t: t.reshape(1, -1)

    def mods(w3, b3, layer, n):
        m = adaln(c_col, w3, b3, layer)
        return [m[:, i * d:(i + 1) * d] for i in range(n)]

    ada_b3 = ada_b[:, None, :]
    conv_dim = gdn_conv.shape[2]
    gdn_main = conv_dim + GDN_V_HEADS * GDN_HEAD_DIM
    kvd = FOX_KV_HEADS * FOX_HEAD_DIM

    sh_m, sc_m, g_m, sh_f, sc_f, g_f = mods(ada_w, ada_b3, 0, 6)
    w_in = gdn_w_in[0]
    proj, small = nm_proj(xs, row(norm_mix[0]), sh_m, sc_m, w_in[:, :gdn_main].astype(BF16),
                          _pad_lanes(w_in[:, gdn_main:]).astype(BF16))
    pad32 = lambda t: jnp.pad(t, (GDN_V_HEADS, LANES - 2 * GDN_V_HEADS)).reshape(1, LANES)
    gt, gh = gdn_gates(small, pad32(gdn_a_log[0]), pad32(gdn_dt_bias[0]))
    o = gdn_core(proj, gdn_conv[0], gt, gh, row(gdn_norm[0]))
    xs = proj_res(o, gdn_w_out[0].astype(BF16), xs, g_m)
    one = row(out_norm)
    xs = ffn(xs, row(norm_ffn[0]), sh_f, sc_f, g_f, ffn_w_in[0].astype(BF16),
             ffn_w_out[0].astype(BF16), one, one, one, final=False)

    sh_k, sc_k = mods(kv_ada_w[None], kv_ada_b[None, None, :], 0, 2)
    kvp, fl = nm_proj(xs, row(kv_norm), sh_k, sc_k, kv_w[:, :2 * kvd].astype(BF16),
                      _pad_lanes(kv_w[:, 2 * kvd:]).astype(BF16))
    fb_row = jnp.pad(forget_b, (0, LANES - forget_b.shape[0])).reshape(1, LANES)
    k_sh, v_sh, fh = kv_prep(kvp, fl, row(k_norm), fb_row)

    sh_m, sc_m, g_m, sh_f, sc_f, g_f = mods(ada_w, ada_b3, 1, 6)
    qg, _ = nm_proj(xs, row(norm_mix[1]), sh_m, sc_m, fox_w_in[0].astype(BF16),
                    jnp.zeros((d, LANES), BF16))
    a = fox_attention(qg, k_sh, v_sh, fh, row(q_norm[0]))
    xs = proj_res(a, fox_w_out[0].astype(BF16), xs, g_m)
    sh_o, sc_o = mods(out_ada_w[None], out_ada_b[None, None, :], 0, 2)
    xs = ffn(xs, row(norm_ffn[1]), sh_f, sc_f, g_f, ffn_w_in[1].astype(BF16),
             ffn_w_out[1].astype(BF16), row(out_norm), sh_o, sc_o, final=True)
    return xs.reshape(bsz, l, d)
```

```python
import functools
import math

import jax
import jax.numpy as jnp
from jax import lax
from jax.experimental import pallas as pl
from jax.experimental.pallas import tpu as pltpu

F32 = jnp.float32
BF16 = jnp.bfloat16
NORM_EPS = 1e-6
LOG2E = math.log2(math.e)

V7X_VMEM_BYTES = 64 * 1024 * 1024
VMEM_LIMIT_BYTES = 56 * 1024 * 1024
LANES = 128
SUBLANES = 8

GDN_CHUNK = 64
GDN_HEAD_DIM = 128
GDN_V_HEADS = 32
GDN_QK_HEADS = 16
CONV_K = 4
FOX_HEAD_DIM = 256
FOX_KV_HEADS = 2
FOX_GROUP = 8

NT_DIMS = (((1,), (1,)), ((), ()))
TN_DIMS = (((0,), (0,)), ((), ()))


def _cparams(sem):
    return pltpu.CompilerParams(dimension_semantics=sem, vmem_limit_bytes=VMEM_LIMIT_BYTES)


def _silu(x):
    return x * jax.nn.sigmoid(x)


def _softplus(x):
    return jnp.maximum(x, 0.0) + jnp.log1p(jnp.exp(-jnp.abs(x)))


def _log_sigmoid(x):
    return jnp.minimum(x, 0.0) - jnp.log1p(jnp.exp(-jnp.abs(x)))


def _norm_mod(x, nw, sh, sc):
    ms = jnp.mean(x * x, axis=-1, keepdims=True)
    y = x * lax.rsqrt(ms + NORM_EPS) * nw
    return y * (1.0 + sc) + sh


NORM_ROWS = 256


def _norm_mod_rows(dst_ref, src_fn, nw, sh, sc):
    tm = dst_ref.shape[0]
    step = min(NORM_ROWS, tm)
    for r in range(0, tm, step):
        rows = slice(r, r + step)
        dst_ref[rows, :] = _norm_mod(src_fn(rows), nw, sh, sc).astype(dst_ref.dtype)


def _adaln_kernel(c_ref, w_ref, b_ref, o_ref):
    cond = _silu(c_ref[...])
    o_ref[...] = jnp.sum(w_ref[...] * cond, axis=0, keepdims=True) + b_ref[...]


def adaln(c_col, w3, b3, layer, tn=1024):
    _, d, n = w3.shape
    tn = min(tn, n)
    return pl.pallas_call(
        _adaln_kernel,
        grid=(n // tn,),
        in_specs=[
            pl.BlockSpec((d, 1), lambda j: (0, 0)),
            pl.BlockSpec((None, d, tn), lambda j: (layer, 0, j)),
            pl.BlockSpec((None, 1, tn), lambda j: (layer, 0, j)),
        ],
        out_specs=pl.BlockSpec((1, tn), lambda j: (0, j)),
        out_shape=jax.ShapeDtypeStruct((1, n), F32),
        compiler_params=_cparams(("arbitrary",)),
        name="adaln",
    )(c_col, w3, b3)


def _nm_proj_kernel(x_ref, nw_ref, sh_ref, sc_ref, w_ref, *rest, side):
    if side:
        ws_ref, o_ref, os_ref, h_ref = rest
    else:
        o_ref, h_ref = rest

    @pl.when(pl.program_id(1) == 0)
    def _():
        _norm_mod_rows(h_ref, lambda rows: x_ref[rows, :], nw_ref[...], sh_ref[...], sc_ref[...])
        if side:
            os_ref[...] = jnp.dot(h_ref[...], ws_ref[...], preferred_element_type=F32)

    o_ref[...] = jnp.dot(h_ref[...], w_ref[...], preferred_element_type=F32)


def nm_proj(x, nw, sh, sc, w, ws=None, tm=1024, tn=1024):
    l, d = x.shape
    n = w.shape[1]
    tm, tn = min(tm, l), min(tn, n)
    side = ws is not None
    row = lambda i, j: (0, 0)
    in_specs = [
        pl.BlockSpec((tm, d), lambda i, j: (i, 0)),
        pl.BlockSpec((1, d), row), pl.BlockSpec((1, d), row), pl.BlockSpec((1, d), row),
        pl.BlockSpec((d, tn), lambda i, j: (0, j)),
    ]
    out_specs = [pl.BlockSpec((tm, tn), lambda i, j: (i, j))]
    out_shape = [jax.ShapeDtypeStruct((l, n), F32)]
    args = [x, nw, sh, sc, w]
    if side:
        in_specs.append(pl.BlockSpec((d, LANES), row))
        out_specs.append(pl.BlockSpec((tm, LANES), lambda i, j: (i, 0)))
        out_shape.append(jax.ShapeDtypeStruct((l, LANES), F32))
        args.append(ws)
    out = pl.pallas_call(
        functools.partial(_nm_proj_kernel, side=side),
        grid=(l // tm, n // tn),
        in_specs=in_specs,
        out_specs=out_specs,
        out_shape=out_shape,
        scratch_shapes=[pltpu.VMEM((tm, d), BF16)],
        compiler_params=_cparams(("arbitrary", "arbitrary")),
        name="nm_proj",
    )(*args)
    return out if side else out[0]


def _proj_res_kernel(a_ref, w_ref, x_ref, g_ref, o_ref):
    y = jnp.dot(a_ref[...], w_ref[...], preferred_element_type=F32)
    o_ref[...] = x_ref[...] + g_ref[...] * y


def proj_res(a, w, x, g, tm=1024, tn=512):
    l, k = a.shape
    d = w.shape[1]
    tm, tn = min(tm, l), min(tn, d)
    return pl.pallas_call(
        _proj_res_kernel,
        grid=(l // tm, d // tn),
        in_specs=[
            pl.BlockSpec((tm, k), lambda i, j: (i, 0)),
            pl.BlockSpec((k, tn), lambda i, j: (0, j)),
            pl.BlockSpec((tm, tn), lambda i, j: (i, j)),
            pl.BlockSpec((1, tn), lambda i, j: (0, j)),
        ],
        out_specs=pl.BlockSpec((tm, tn), lambda i, j: (i, j)),
        out_shape=jax.ShapeDtypeStruct((l, d), F32),
        compiler_params=_cparams(("arbitrary", "arbitrary")),
        name="proj_res",
    )(a, w, x, g)


def _ffn_kernel(x_ref, nw_ref, sh_ref, sc_ref, g_ref, wg_ref, wu_ref, wo_ref, fnw_ref, fsh_ref,
                fsc_ref, o_ref, h_ref, acc_ref, *, final):
    j = pl.program_id(1)

    @pl.when(j == 0)
    def _():
        _norm_mod_rows(h_ref, lambda rows: x_ref[rows, :], nw_ref[...], sh_ref[...], sc_ref[...])
        acc_ref[...] = jnp.zeros_like(acc_ref)

    h = h_ref[...]
    gate = jnp.dot(h, wg_ref[...], preferred_element_type=F32)
    up = jnp.dot(h, wu_ref[...], preferred_element_type=F32)
    act = (_silu(gate) * up).astype(BF16)
    acc_ref[...] += jnp.dot(act, wo_ref[...], preferred_element_type=F32)

    @pl.when(j == pl.num_programs(1) - 1)
    def _():
        res = lambda rows: x_ref[rows, :] + g_ref[...] * acc_ref[rows, :]
        if final:
            _norm_mod_rows(o_ref, res, fnw_ref[...], fsh_ref[...], fsc_ref[...])
        else:
            o_ref[...] = res(slice(None))


def ffn(x, nw, sh, sc, g, w_in, w_out, fnw, fsh, fsc, final, tm=512, th=512):
    l, d = x.shape
    hdim = w_out.shape[0]
    tm, th = min(tm, l), min(th, hdim)
    nh = hdim // th
    row = lambda i, j: (0, 0)
    vec = pl.BlockSpec((1, d), row)
    return pl.pallas_call(
        functools.partial(_ffn_kernel, final=final),
        grid=(l // tm, nh),
        in_specs=[
            pl.BlockSpec((tm, d), lambda i, j: (i, 0)),
            vec, vec, vec, vec,
            pl.BlockSpec((d, th), lambda i, j: (0, j)),
            pl.BlockSpec((d, th), lambda i, j: (0, j + nh)),
            pl.BlockSpec((th, d), lambda i, j: (j, 0)),
            vec, vec, vec,
        ],
        out_specs=pl.BlockSpec((tm, d), lambda i, j: (i, 0)),
        out_shape=jax.ShapeDtypeStruct((l, d), F32),
        scratch_shapes=[pltpu.VMEM((tm, d), BF16), pltpu.VMEM((tm, d), F32)],
        compiler_params=_cparams(("arbitrary", "arbitrary")),
        name="ffn",
    )(x, nw, sh, sc, g, w_in, w_in, w_out, fnw, fsh, fsc)


def _gdn_gates_kernel(s_ref, alog_ref, dt_ref, gt_ref, gh_ref):
    x = s_ref[...]
    tm = x.shape[0]
    lane = lax.broadcasted_iota(jnp.int32, x.shape, 1)
    beta = jax.nn.sigmoid(x)
    g = -jnp.exp(alog_ref[...]) * _softplus(x + dt_ref[...])
    g = jnp.where((lane >= GDN_V_HEADS) & (lane < 2 * GDN_V_HEADS), g, 0.0)
    r = lax.broadcasted_iota(jnp.int32, (tm, tm), 0)
    c = lax.broadcasted_iota(jnp.int32, (tm, tm), 1)
    same_chunk = (r // GDN_CHUNK) == (c // GDN_CHUNK)
    tril = jnp.where((r >= c) & same_chunk, 1.0, 0.0).astype(F32)
    gcum = jnp.dot(tril, g, preferred_element_type=F32, precision=lax.Precision.HIGHEST)
    gt = jnp.where(lane < GDN_V_HEADS, beta, gcum)
    gt_ref[...] = gt
    gh_ref[...] = gt.T


def gdn_gates(small, alog_row, dt_row, tm=512):
    l = small.shape[0]
    tm = min(tm, l)
    row = lambda i: (0, 0)
    return pl.pallas_call(
        _gdn_gates_kernel,
        grid=(l // tm,),
        in_specs=[pl.BlockSpec((tm, LANES), lambda i: (i, 0)),
                  pl.BlockSpec((1, LANES), row), pl.BlockSpec((1, LANES), row)],
        out_specs=[pl.BlockSpec((tm, LANES), lambda i: (i, 0)),
                   pl.BlockSpec((LANES, tm), lambda i: (0, i))],
        out_shape=[jax.ShapeDtypeStruct((l, LANES), F32), jax.ShapeDtypeStruct((LANES, l), F32)],
        compiler_params=_cparams(("arbitrary",)),
        name="gdn_gates",
    )(small, alog_row, dt_row)


GDN_PAIRS = 4


def _conv_silu(x_ref, xe_ref, w_ref):
    tb = x_ref.shape[0]
    pad = SUBLANES
    xe_ref[pad:pad + tb, :] = x_ref[...]
    w = w_ref[...]
    acc = w[CONV_K - 1:CONV_K, :] * xe_ref[pad:pad + tb, :]
    for i in range(CONV_K - 1):
        off = pad - (CONV_K - 1) + i
        acc = acc + w[i:i + 1, :] * xe_ref[off:off + tb, :]
    xe_ref[0:pad, :] = xe_ref[tb:tb + pad, :]
    return _silu(acc)


def _gdn_kernel(q_ref, k_ref, v_ref, z_ref, wq_ref, wk_ref, wv_ref, gt_ref, gh_ref, nw_ref,
                o_ref, xq_ref, xk_ref, xv_ref, s_ref, *, pairs):
    gi = pl.program_id(0)
    tb = q_ref.shape[0]
    hd = GDN_HEAD_DIM
    ck = GDN_CHUNK
    nck = tb // ck
    n_sq = int(math.log2(ck)) - 1

    @pl.when(pl.program_id(1) == 0)
    def _():
        for xe_ref in (xq_ref, xk_ref, xv_ref):
            xe_ref[0:SUBLANES, :] = jnp.zeros((SUBLANES, xe_ref.shape[1]), F32)
        s_ref[...] = jnp.zeros_like(s_ref)

    q_all = _conv_silu(q_ref, xq_ref, wq_ref)
    k_all = _conv_silu(k_ref, xk_ref, wk_ref)
    v_all = _conv_silu(v_ref, xv_ref, wv_ref)

    head0 = 2 * pairs * gi
    gt = pltpu.roll(gt_ref[...], lax.rem(LANES - head0, LANES), axis=1)
    r_idx = lax.broadcasted_iota(jnp.int32, (tb, tb), 0)
    c_idx = lax.broadcasted_iota(jnp.int32, (tb, tb), 1)
    same = (r_idx // ck) == (c_idx // ck)
    causal = same & (r_idx >= c_idx)
    strict = same & (r_idx > c_idx)
    same_b = jnp.where(same, 1.0, 0.0).astype(BF16)
    nw = nw_ref[...]

    def block_diag(packed):
        return jnp.concatenate([packed.astype(BF16)] * nck, axis=0) * same_b

    heads = range(2 * pairs)
    qs, ks, kks, qks = [], [], [], []
    for pi in range(pairs):
        q = q_all[:, pi * hd:(pi + 1) * hd]
        k = k_all[:, pi * hd:(pi + 1) * hd]
        q = q * (lax.rsqrt(jnp.sum(q * q, axis=-1, keepdims=True) + NORM_EPS) * hd ** -0.5)
        k = k * lax.rsqrt(jnp.sum(k * k, axis=-1, keepdims=True) + NORM_EPS)
        kb = k.astype(BF16)
        qs.append(q)
        ks.append(k)
        kks.append(lax.dot_general(kb, kb, NT_DIMS, preferred_element_type=F32))
        qks.append(lax.dot_general(q.astype(BF16), kb, NT_DIMS, preferred_element_type=F32))

    bcs, gccs, attns, ps = [], [], [], []
    for h in heads:
        bc = gt[:, h:h + 1]
        gcc = gt[:, GDN_V_HEADS + h:GDN_V_HEADS + h + 1]
        gcr = gh_ref[pl.ds(GDN_V_HEADS + head0 + h, 1), :]
        decay = jnp.exp(jnp.where(causal, gcc - gcr, -jnp.inf))
        a_full = jnp.where(strict, kks[h // 2] * decay * bc, 0.0)
        attn = qks[h // 2] * decay
        attns.append([attn[c * ck:(c + 1) * ck, c * ck:(c + 1) * ck].astype(BF16)
                      for c in range(nck)])
        p = -a_full[0:ck]
        for c in range(1, nck):
            p = p - a_full[c * ck:(c + 1) * ck]
        bcs.append(bc)
        gccs.append(gcc)
        ps.append(p)

    rs = list(ps)
    p_bds = [block_diag(p) for p in ps]
    for _ in range(n_sq):
        ps = [jnp.dot(ps[h].astype(BF16), p_bds[h], preferred_element_type=F32) for h in heads]
        p_bds = [block_diag(p) for p in ps]
        rs = [rs[h] + ps[h] + jnp.dot(rs[h].astype(BF16), p_bds[h], preferred_element_type=F32)
              for h in heads]

    egs = [jnp.exp(gcc) for gcc in gccs]
    us, wqs = [], []
    for h in heads:
        k = ks[h // 2]
        rhs = bcs[h] * jnp.concatenate([v_all[:, h * hd:(h + 1) * hd], k * egs[h]], axis=1)
        uw = rhs + jnp.dot(block_diag(rs[h]), rhs.astype(BF16), preferred_element_type=F32)
        us.append(uw[:, :hd])
        q_dec = qs[h // 2] * egs[h]
        wqs.append([jnp.concatenate([uw[c * ck:(c + 1) * ck, hd:], q_dec[c * ck:(c + 1) * ck]],
                                    axis=0).astype(BF16) for c in range(nck)])

    states = [s_ref[h] for h in heads]
    outs = [[] for _ in heads]
    for ci in range(nck):
        rows = slice(ci * ck, (ci + 1) * ck)
        wq_s = [jnp.dot(wqs[h][ci], states[h].astype(BF16), preferred_element_type=F32)
                for h in heads]
        v_news = [(us[h][rows] - wq_s[h][:ck]).astype(BF16) for h in heads]
        for h in heads:
            outs[h].append(wq_s[h][ck:] + jnp.dot(attns[h][ci], v_news[h],
                                                  preferred_element_type=F32))
        for h in heads:
            g_last = gccs[h][(ci + 1) * ck - 1:(ci + 1) * ck, :]
            k_dec = (ks[h // 2][rows] * jnp.exp(g_last - gccs[h][rows])).astype(BF16)
            states[h] = states[h] * jnp.exp(g_last) + lax.dot_general(
                k_dec, v_news[h], TN_DIMS, preferred_element_type=F32)

    for h in heads:
        s_ref[h] = states[h]
        o = jnp.concatenate(outs[h], axis=0)
        on = o * lax.rsqrt(jnp.mean(o * o, axis=-1, keepdims=True) + NORM_EPS) * nw
        cols = slice(h * hd, (h + 1) * hd)
        o_ref[:, cols] = (on * _silu(z_ref[:, cols])).astype(o_ref.dtype)


def gdn_core(proj, conv_w, gt, gh, norm_w, tb=256, pairs=GDN_PAIRS):
    l = proj.shape[0]
    hd = GDN_HEAD_DIM
    tb = min(tb, l)
    ng = GDN_QK_HEADS // pairs
    qw, vw = pairs * hd, 2 * pairs * hd
    return pl.pallas_call(
        functools.partial(_gdn_kernel, pairs=pairs),
        grid=(ng, l // tb),
        in_specs=[
            pl.BlockSpec((tb, qw), lambda g, b: (b, g)),
            pl.BlockSpec((tb, qw), lambda g, b: (b, ng + g)),
            pl.BlockSpec((tb, vw), lambda g, b: (b, ng + g)),
            pl.BlockSpec((tb, vw), lambda g, b: (b, 2 * ng + g)),
            pl.BlockSpec((CONV_K, qw), lambda g, b: (0, g)),
            pl.BlockSpec((CONV_K, qw), lambda g, b: (0, ng + g)),
            pl.BlockSpec((CONV_K, vw), lambda g, b: (0, ng + g)),
            pl.BlockSpec((tb, LANES), lambda g, b: (b, 0)),
            pl.BlockSpec((LANES, tb), lambda g, b: (0, b)),
            pl.BlockSpec((1, hd), lambda g, b: (0, 0)),
        ],
        out_specs=pl.BlockSpec((tb, vw), lambda g, b: (b, g)),
        out_shape=jax.ShapeDtypeStruct((l, GDN_V_HEADS * hd), BF16),
        scratch_shapes=[pltpu.VMEM((tb + SUBLANES, qw), F32),
                        pltpu.VMEM((tb + SUBLANES, qw), F32),
                        pltpu.VMEM((tb + SUBLANES, vw), F32),
                        pltpu.VMEM((2 * pairs, hd, hd), F32)],
        compiler_params=_cparams(("arbitrary", "arbitrary")),
        name="gdn_core",
    )(proj, proj, proj, proj, conv_w, conv_w, conv_w, gt, gh, norm_w)


def _kv_prep_kernel(kv_ref, fl_ref, knw_ref, fb_ref, k_ref, v_ref, fh_ref, carry_ref):
    hd = FOX_HEAD_DIM
    nkv = FOX_KV_HEADS
    tm = kv_ref.shape[0]

    @pl.when(pl.program_id(0) == 0)
    def _():
        carry_ref[...] = jnp.zeros_like(carry_ref)

    for h in range(nkv):
        kh = kv_ref[:, h * hd:(h + 1) * hd]
        ms = jnp.mean(kh * kh, axis=-1, keepdims=True)
        k_ref[:, h * hd:(h + 1) * hd] = (kh * lax.rsqrt(ms + NORM_EPS) * knw_ref[...]).astype(BF16)
    v_ref[...] = kv_ref[:, nkv * hd:2 * nkv * hd].astype(BF16)

    log_f = _log_sigmoid(fl_ref[...] + fb_ref[...])
    r = lax.broadcasted_iota(jnp.int32, (tm, tm), 0)
    c = lax.broadcasted_iota(jnp.int32, (tm, tm), 1)
    tril = jnp.where(r >= c, 1.0, 0.0).astype(F32)
    cs = jnp.dot(tril, log_f, preferred_element_type=F32,
                 precision=lax.Precision.HIGHEST) + carry_ref[...]
    carry_ref[...] = cs[tm - 1:tm, :]
    fh_ref[...] = (cs * LOG2E).T[0:fh_ref.shape[0], :]


def kv_prep(kv, fl, knw, fb_row, tm=512):
    l = kv.shape[0]
    tm = min(tm, l)
    kvd = FOX_KV_HEADS * FOX_HEAD_DIM
    nh = FOX_KV_HEADS * FOX_GROUP
    row = lambda i: (0, 0)
    return pl.pallas_call(
        _kv_prep_kernel,
        grid=(l // tm,),
        in_specs=[pl.BlockSpec((tm, 2 * kvd), lambda i: (i, 0)),
                  pl.BlockSpec((tm, LANES), lambda i: (i, 0)),
                  pl.BlockSpec((1, FOX_HEAD_DIM), row), pl.BlockSpec((1, LANES), row)],
        out_specs=[pl.BlockSpec((tm, kvd), lambda i: (i, 0)),
                   pl.BlockSpec((tm, kvd), lambda i: (i, 0)),
                   pl.BlockSpec((nh, tm), lambda i: (0, i))],
        out_shape=[jax.ShapeDtypeStruct((l, kvd), BF16), jax.ShapeDtypeStruct((l, kvd), BF16),
                   jax.ShapeDtypeStruct((nh, l), F32)],
        scratch_shapes=[pltpu.VMEM((1, LANES), F32)],
        compiler_params=_cparams(("arbitrary",)),
        name="kv_prep",
    )(kv, fl, knw, fb_row)


def _fox_kernel(q_ref, gate_ref, k_ref, v_ref, fh_ref, qnw_ref, o_ref, qn_ref, m_ref, l_ref,
                acc_ref):
    i = pl.program_id(1)
    tq = q_ref.shape[0]
    hd = FOX_HEAD_DIM
    scale = hd ** -0.5 * LOG2E
    for g in range(FOX_GROUP):
        qh = q_ref[:, g * hd:(g + 1) * hd]
        ms = jnp.mean(qh * qh, axis=-1, keepdims=True)
        qn_ref[g] = (qh * (lax.rsqrt(ms + NORM_EPS) * scale) * qnw_ref[...]).astype(BF16)
    m_ref[...] = jnp.full(m_ref.shape, -jnp.inf, F32)
    l_ref[...] = jnp.zeros_like(l_ref)
    acc_ref[...] = jnp.zeros_like(acc_ref)
    causal = (lax.broadcasted_iota(jnp.int32, (tq, tq), 1)
              <= lax.broadcasted_iota(jnp.int32, (tq, tq), 0))

    def tile(j, masked):
        start = pl.multiple_of(j * tq, tq)
        kj = k_ref[pl.ds(start, tq), :]
        vj = v_ref[pl.ds(start, tq), :]
        qk = lambda g: lax.dot_general(qn_ref[g], kj, NT_DIMS, preferred_element_type=F32)
        s_next = qk(0)
        for g in range(FOX_GROUP):
            s, s_next = s_next, (qk(g + 1) if g + 1 < FOX_GROUP else None)
            s = s - fh_ref[g:g + 1, pl.ds(start, tq)]
            if masked:
                s = jnp.where(causal, s, -jnp.inf)
            m_old = m_ref[g]
            m_new = jnp.maximum(m_old, jnp.max(s, axis=-1, keepdims=True))
            alpha = jnp.exp2(m_old - m_new)
            pr = jnp.exp2(s - m_new)
            l_ref[g] = alpha * l_ref[g] + jnp.sum(pr, axis=-1, keepdims=True)
            acc_ref[g] = alpha * acc_ref[g] + jnp.dot(pr.astype(BF16), vj,
                                                      preferred_element_type=F32)
            m_ref[g] = m_new

    def body(j, carry):
        tile(j, False)
        return carry

    lax.fori_loop(0, i, body, 0)
    tile(i, True)
    for g in range(FOX_GROUP):
        cols = slice(g * hd, (g + 1) * hd)
        o_ref[:, cols] = (acc_ref[g] / l_ref[g] * jax.nn.sigmoid(gate_ref[:, cols])).astype(o_ref.dtype)


def fox_attention(qg, k, v, fh, qnw, tq=512):
    l = qg.shape[0]
    hd = FOX_HEAD_DIM
    gw = FOX_GROUP * hd
    tq = min(tq, l)
    return pl.pallas_call(
        _fox_kernel,
        grid=(FOX_KV_HEADS, l // tq),
        in_specs=[
            pl.BlockSpec((tq, gw), lambda h, i: (i, h)),
            pl.BlockSpec((tq, gw), lambda h, i: (i, FOX_KV_HEADS + h)),
            pl.BlockSpec((l, hd), lambda h, i: (0, h)),
            pl.BlockSpec((l, hd), lambda h, i: (0, h)),
            pl.BlockSpec((FOX_GROUP, l), lambda h, i: (h, 0)),
            pl.BlockSpec((1, hd), lambda h, i: (0, 0)),
        ],
        out_specs=pl.BlockSpec((tq, gw), lambda h, i: (i, h)),
        out_shape=jax.ShapeDtypeStruct((l, FOX_KV_HEADS * gw), BF16),
        scratch_shapes=[pltpu.VMEM((FOX_GROUP, tq, hd), BF16),
                        pltpu.VMEM((FOX_GROUP, tq, 1), F32),
                        pltpu.VMEM((FOX_GROUP, tq, 1), F32),
                        pltpu.VMEM((FOX_GROUP, tq, hd), F32)],
        compiler_params=_cparams(("arbitrary", "arbitrary")),
        name="fox_attention",
    )(qg, qg, k, v, fh, qnw)


def _pad_lanes(w):
    return jnp.pad(w, ((0, 0), (0, LANES - w.shape[1])))


def kernel(x, c, ada_w, ada_b, norm_mix, norm_ffn, ffn_w_in, ffn_w_out, gdn_w_in, gdn_conv,
           gdn_a_log, gdn_dt_bias, gdn_norm, gdn_w_out, kv_ada_w, kv_ada_b, kv_norm, kv_w, k_norm,
           forget_b, fox_w_in, q_norm, fox_w_out, out_ada_w, out_ada_b, out_norm):
    bsz, l, d = x.shape
    assert bsz == 1 and ada_w.shape[0] == 2 and gdn_w_in.shape[0] == 1 and fox_w_in.shape[0] == 1
    xs = x.reshape(l, d)
    c_col = c.reshape(d, 1)
    row = lambda t: t.reshape(1, -1)

    def mods(w3, b3, layer, n):
        m = adaln(c_col, w3, b3, layer)
        return [m[:, i * d:(i + 1) * d] for i in range(n)]

    ada_b3 = ada_b[:, None, :]
    conv_dim = gdn_conv.shape[2]
    gdn_main = conv_dim + GDN_V_HEADS * GDN_HEAD_DIM
    kvd = FOX_KV_HEADS * FOX_HEAD_DIM

    sh_m, sc_m, g_m, sh_f, sc_f, g_f = mods(ada_w, ada_b3, 0, 6)
    w_in = gdn_w_in[0]
    proj, small = nm_proj(xs, row(norm_mix[0]), sh_m, sc_m, w_in[:, :gdn_main].astype(BF16),
                          _pad_lanes(w_in[:, gdn_main:]).astype(BF16))
    pad32 = lambda t: jnp.pad(t, (GDN_V_HEADS, LANES - 2 * GDN_V_HEADS)).reshape(1, LANES)
    gt, gh = gdn_gates(small, pad32(gdn_a_log[0]), pad32(gdn_dt_bias[0]))
    o = gdn_core(proj, gdn_conv[0], gt, gh, row(gdn_norm[0]))
    xs = proj_res(o, gdn_w_out[0].astype(BF16), xs, g_m)
    one = row(out_norm)
    xs = ffn(xs, row(norm_ffn[0]), sh_f, sc_f, g_f, ffn_w_in[0].astype(BF16),
             ffn_w_out[0].astype(BF16), one, one, one, final=False)

    sh_k, sc_k = mods(kv_ada_w[None], kv_ada_b[None, None, :], 0, 2)
    kvp, fl = nm_proj(xs, row(kv_norm), sh_k, sc_k, kv_w[:, :2 * kvd].astype(BF16),
                      _pad_lanes(kv_w[:, 2 * kvd:]).astype(BF16))
    fb_row = jnp.pad(forget_b, (0, LANES - forget_b.shape[0])).reshape(1, LANES)
    k_sh, v_sh, fh = kv_prep(kvp, fl, row(k_norm), fb_row)

    sh_m, sc_m, g_m, sh_f, sc_f, g_f = mods(ada_w, ada_b3, 1, 6)
    qg = nm_proj(xs, row(norm_mix[1]), sh_m, sc_m, fox_w_in[0].astype(BF16))
    a = fox_attention(qg, k_sh, v_sh, fh, row(q_norm[0]))
    xs = proj_res(a, fox_w_out[0].astype(BF16), xs, g_m)
    sh_o, sc_o = mods(out_ada_w[None], out_ada_b[None, None, :], 0, 2)
    xs = ffn(xs, row(norm_ffn[1]), sh_f, sc_f, g_f, ffn_w_in[1].astype(BF16),
             ffn_w_out[1].astype(BF16), row(out_norm), sh_o, sc_o, final=True)
    return xs.reshape(bsz, l, d)
```

```python
import functools
import math

import jax
import jax.numpy as jnp
from jax import lax
from jax.experimental import pallas as pl
from jax.experimental.pallas import tpu as pltpu

F32 = jnp.float32
BF16 = jnp.bfloat16
NORM_EPS = 1e-6
LOG2E = math.log2(math.e)

V7X_VMEM_BYTES = 64 * 1024 * 1024
VMEM_LIMIT_BYTES = 56 * 1024 * 1024
LANES = 128
SUBLANES = 8

GDN_CHUNK = 64
GDN_HEAD_DIM = 128
GDN_V_HEADS = 32
GDN_QK_HEADS = 16
CONV_K = 4
FOX_HEAD_DIM = 256
FOX_KV_HEADS = 2
FOX_GROUP = 8

NT_DIMS = (((1,), (1,)), ((), ()))
TN_DIMS = (((0,), (0,)), ((), ()))


def _cparams(sem):
    return pltpu.CompilerParams(dimension_semantics=sem, vmem_limit_bytes=VMEM_LIMIT_BYTES)


def _silu(x):
    return x * jax.nn.sigmoid(x)


def _softplus(x):
    return jnp.maximum(x, 0.0) + jnp.log1p(jnp.exp(-jnp.abs(x)))


def _log_sigmoid(x):
    return jnp.minimum(x, 0.0) - jnp.log1p(jnp.exp(-jnp.abs(x)))


def _norm_mod(x, nw, sh, sc):
    ms = jnp.mean(x * x, axis=-1, keepdims=True)
    y = x * lax.rsqrt(ms + NORM_EPS) * nw
    return y * (1.0 + sc) + sh


NORM_ROWS = 256


def _norm_mod_rows(dst_ref, src_fn, nw, sh, sc):
    tm = dst_ref.shape[0]
    step = min(NORM_ROWS, tm)
    for r in range(0, tm, step):
        rows = slice(r, r + step)
        dst_ref[rows, :] = _norm_mod(src_fn(rows), nw, sh, sc).astype(dst_ref.dtype)


def _adaln_kernel(c_ref, w_ref, b_ref, o_ref):
    cond = _silu(c_ref[...])
    o_ref[...] = jnp.sum(w_ref[...] * cond, axis=0, keepdims=True) + b_ref[...]


def adaln(c_col, w3, b3, layer, tn=1024):
    _, d, n = w3.shape
    tn = min(tn, n)
    return pl.pallas_call(
        _adaln_kernel,
        grid=(n // tn,),
        in_specs=[
            pl.BlockSpec((d, 1), lambda j: (0, 0)),
            pl.BlockSpec((None, d, tn), lambda j: (layer, 0, j)),
            pl.BlockSpec((None, 1, tn), lambda j: (layer, 0, j)),
        ],
        out_specs=pl.BlockSpec((1, tn), lambda j: (0, j)),
        out_shape=jax.ShapeDtypeStruct((1, n), F32),
        compiler_params=_cparams(("arbitrary",)),
        name="adaln",
    )(c_col, w3, b3)


def _nm_proj_kernel(x_ref, nw_ref, sh_ref, sc_ref, w_ref, *rest, side):
    if side:
        ws_ref, o_ref, os_ref, h_ref = rest
    else:
        o_ref, h_ref = rest

    @pl.when(pl.program_id(1) == 0)
    def _():
        _norm_mod_rows(h_ref, lambda rows: x_ref[rows, :], nw_ref[...], sh_ref[...], sc_ref[...])
        if side:
            os_ref[...] = jnp.dot(h_ref[...], ws_ref[...], preferred_element_type=F32)

    o_ref[...] = jnp.dot(h_ref[...], w_ref[...].astype(BF16), preferred_element_type=F32)


def nm_proj(x, nw, sh, sc, w3, layer, n, ws=None, tm=1024, tn=1024):
    l, d = x.shape
    tm, tn = min(tm, l), min(tn, n)
    side = ws is not None
    row = lambda i, j: (0, 0)
    in_specs = [
        pl.BlockSpec((tm, d), lambda i, j: (i, 0)),
        pl.BlockSpec((1, d), row), pl.BlockSpec((1, d), row), pl.BlockSpec((1, d), row),
        pl.BlockSpec((None, d, tn), lambda i, j: (layer, 0, j)),
    ]
    out_specs = [pl.BlockSpec((tm, tn), lambda i, j: (i, j))]
    out_shape = [jax.ShapeDtypeStruct((l, n), F32)]
    args = [x, nw, sh, sc, w3]
    if side:
        in_specs.append(pl.BlockSpec((d, LANES), row))
        out_specs.append(pl.BlockSpec((tm, LANES), lambda i, j: (i, 0)))
        out_shape.append(jax.ShapeDtypeStruct((l, LANES), F32))
        args.append(ws)
    out = pl.pallas_call(
        functools.partial(_nm_proj_kernel, side=side),
        grid=(l // tm, n // tn),
        in_specs=in_specs,
        out_specs=out_specs,
        out_shape=out_shape,
        scratch_shapes=[pltpu.VMEM((tm, d), BF16)],
        compiler_params=_cparams(("arbitrary", "arbitrary")),
        name="nm_proj",
    )(*args)
    return out if side else out[0]


def _proj_res_kernel(a_ref, w_ref, x_ref, g_ref, o_ref):
    y = jnp.dot(a_ref[...], w_ref[...].astype(BF16), preferred_element_type=F32)
    o_ref[...] = x_ref[...] + g_ref[...] * y


def proj_res(a, w3, layer, x, g, tm=1024, tn=512):
    l, k = a.shape
    d = w3.shape[2]
    tm, tn = min(tm, l), min(tn, d)
    return pl.pallas_call(
        _proj_res_kernel,
        grid=(l // tm, d // tn),
        in_specs=[
            pl.BlockSpec((tm, k), lambda i, j: (i, 0)),
            pl.BlockSpec((None, k, tn), lambda i, j: (layer, 0, j)),
            pl.BlockSpec((tm, tn), lambda i, j: (i, j)),
            pl.BlockSpec((1, tn), lambda i, j: (0, j)),
        ],
        out_specs=pl.BlockSpec((tm, tn), lambda i, j: (i, j)),
        out_shape=jax.ShapeDtypeStruct((l, d), F32),
        compiler_params=_cparams(("arbitrary", "arbitrary")),
        name="proj_res",
    )(a, w3, x, g)


def _ffn_kernel(x_ref, nw_ref, sh_ref, sc_ref, g_ref, wg_ref, wu_ref, wo_ref, fnw_ref, fsh_ref,
                fsc_ref, o_ref, h_ref, acc_ref, *, final):
    j = pl.program_id(1)

    @pl.when(j == 0)
    def _():
        _norm_mod_rows(h_ref, lambda rows: x_ref[rows, :], nw_ref[...], sh_ref[...], sc_ref[...])
        acc_ref[...] = jnp.zeros_like(acc_ref)

    h = h_ref[...]
    gate = jnp.dot(h, wg_ref[...], preferred_element_type=F32)
    up = jnp.dot(h, wu_ref[...], preferred_element_type=F32)
    act = (_silu(gate) * up).astype(BF16)
    acc_ref[...] += jnp.dot(act, wo_ref[...], preferred_element_type=F32)

    @pl.when(j == pl.num_programs(1) - 1)
    def _():
        res = lambda rows: x_ref[rows, :] + g_ref[...] * acc_ref[rows, :]
        if final:
            _norm_mod_rows(o_ref, res, fnw_ref[...], fsh_ref[...], fsc_ref[...])
        else:
            o_ref[...] = res(slice(None))


def ffn(x, nw, sh, sc, g, w_in, w_out, layer, fnw, fsh, fsc, final, tm=512, th=512):
    l, d = x.shape
    hdim = w_out.shape[1]
    tm, th = min(tm, l), min(th, hdim)
    nh = hdim // th
    row = lambda i, j: (0, 0)
    vec = pl.BlockSpec((1, d), row)
    return pl.pallas_call(
        functools.partial(_ffn_kernel, final=final),
        grid=(l // tm, nh),
        in_specs=[
            pl.BlockSpec((tm, d), lambda i, j: (i, 0)),
            vec, vec, vec, vec,
            pl.BlockSpec((None, d, th), lambda i, j: (layer, 0, j)),
            pl.BlockSpec((None, d, th), lambda i, j: (layer, 0, j + nh)),
            pl.BlockSpec((None, th, d), lambda i, j: (layer, j, 0)),
            vec, vec, vec,
        ],
        out_specs=pl.BlockSpec((tm, d), lambda i, j: (i, 0)),
        out_shape=jax.ShapeDtypeStruct((l, d), F32),
        scratch_shapes=[pltpu.VMEM((tm, d), BF16), pltpu.VMEM((tm, d), F32)],
        compiler_params=_cparams(("arbitrary", "arbitrary")),
        name="ffn",
    )(x, nw, sh, sc, g, w_in, w_in, w_out, fnw, fsh, fsc)


def _gdn_gates_kernel(s_ref, alog_ref, dt_ref, gt_ref, gh_ref):
    x = s_ref[...]
    tm = x.shape[0]
    lane = lax.broadcasted_iota(jnp.int32, x.shape, 1)
    beta = jax.nn.sigmoid(x)
    g = -jnp.exp(alog_ref[...]) * _softplus(x + dt_ref[...])
    g = jnp.where((lane >= GDN_V_HEADS) & (lane < 2 * GDN_V_HEADS), g, 0.0)
    r = lax.broadcasted_iota(jnp.int32, (tm, tm), 0)
    c = lax.broadcasted_iota(jnp.int32, (tm, tm), 1)
    same_chunk = (r // GDN_CHUNK) == (c // GDN_CHUNK)
    tril = jnp.where((r >= c) & same_chunk, 1.0, 0.0).astype(F32)
    gcum = jnp.dot(tril, g, preferred_element_type=F32, precision=lax.Precision.HIGHEST)
    gt = jnp.where(lane < GDN_V_HEADS, beta, gcum)
    gt_ref[...] = gt
    gh_ref[...] = gt.T


def gdn_gates(small, alog_row, dt_row, tm=512):
    l = small.shape[0]
    tm = min(tm, l)
    row = lambda i: (0, 0)
    return pl.pallas_call(
        _gdn_gates_kernel,
        grid=(l // tm,),
        in_specs=[pl.BlockSpec((tm, LANES), lambda i: (i, 0)),
                  pl.BlockSpec((1, LANES), row), pl.BlockSpec((1, LANES), row)],
        out_specs=[pl.BlockSpec((tm, LANES), lambda i: (i, 0)),
                   pl.BlockSpec((LANES, tm), lambda i: (0, i))],
        out_shape=[jax.ShapeDtypeStruct((l, LANES), F32), jax.ShapeDtypeStruct((LANES, l), F32)],
        compiler_params=_cparams(("arbitrary",)),
        name="gdn_gates",
    )(small, alog_row, dt_row)


GDN_PAIRS = 4


def _conv_silu(x_ref, xe_ref, w_ref):
    tb = x_ref.shape[0]
    pad = SUBLANES
    xe_ref[pad:pad + tb, :] = x_ref[...]
    w = w_ref[...]
    acc = w[CONV_K - 1:CONV_K, :] * xe_ref[pad:pad + tb, :]
    for i in range(CONV_K - 1):
        off = pad - (CONV_K - 1) + i
        acc = acc + w[i:i + 1, :] * xe_ref[off:off + tb, :]
    xe_ref[0:pad, :] = xe_ref[tb:tb + pad, :]
    return _silu(acc)


def _gdn_kernel(q_ref, k_ref, v_ref, z_ref, wq_ref, wk_ref, wv_ref, gt_ref, gh_ref, nw_ref,
                o_ref, xq_ref, xk_ref, xv_ref, s_ref, *, pairs):
    gi = pl.program_id(0)
    tb = q_ref.shape[0]
    hd = GDN_HEAD_DIM
    ck = GDN_CHUNK
    nck = tb // ck
    n_sq = int(math.log2(ck)) - 1

    @pl.when(pl.program_id(1) == 0)
    def _():
        for xe_ref in (xq_ref, xk_ref, xv_ref):
            xe_ref[0:SUBLANES, :] = jnp.zeros((SUBLANES, xe_ref.shape[1]), F32)
        s_ref[...] = jnp.zeros_like(s_ref)

    q_all = _conv_silu(q_ref, xq_ref, wq_ref)
    k_all = _conv_silu(k_ref, xk_ref, wk_ref)
    v_all = _conv_silu(v_ref, xv_ref, wv_ref)

    head0 = 2 * pairs * gi
    gt = pltpu.roll(gt_ref[...], lax.rem(LANES - head0, LANES), axis=1)
    r_idx = lax.broadcasted_iota(jnp.int32, (tb, tb), 0)
    c_idx = lax.broadcasted_iota(jnp.int32, (tb, tb), 1)
    same = (r_idx // ck) == (c_idx // ck)
    causal = same & (r_idx >= c_idx)
    strict = same & (r_idx > c_idx)
    same_b = jnp.where(same, 1.0, 0.0).astype(BF16)
    nw = nw_ref[...]

    def block_diag(packed):
        return jnp.concatenate([packed.astype(BF16)] * nck, axis=0) * same_b

    heads = range(2 * pairs)
    qs, ks, kks, qks = [], [], [], []
    for pi in range(pairs):
        q = q_all[:, pi * hd:(pi + 1) * hd]
        k = k_all[:, pi * hd:(pi + 1) * hd]
        q = q * (lax.rsqrt(jnp.sum(q * q, axis=-1, keepdims=True) + NORM_EPS) * hd ** -0.5)
        k = k * lax.rsqrt(jnp.sum(k * k, axis=-1, keepdims=True) + NORM_EPS)
        kb = k.astype(BF16)
        qs.append(q)
        ks.append(k)
        kks.append(lax.dot_general(kb, kb, NT_DIMS, preferred_element_type=F32))
        qks.append(lax.dot_general(q.astype(BF16), kb, NT_DIMS, preferred_element_type=F32))

    bcs, gccs, attns, ps = [], [], [], []
    for h in heads:
        bc = gt[:, h:h + 1]
        gcc = gt[:, GDN_V_HEADS + h:GDN_V_HEADS + h + 1]
        gcr = gh_ref[pl.ds(GDN_V_HEADS + head0 + h, 1), :]
        decay = jnp.exp(jnp.where(causal, gcc - gcr, -jnp.inf))
        a_full = jnp.where(strict, kks[h // 2] * decay * bc, 0.0)
        attn = qks[h // 2] * decay
        attns.append([attn[c * ck:(c + 1) * ck, c * ck:(c + 1) * ck].astype(BF16)
                      for c in range(nck)])
        p = -a_full[0:ck]
        for c in range(1, nck):
            p = p - a_full[c * ck:(c + 1) * ck]
        bcs.append(bc)
        gccs.append(gcc)
        ps.append(p)

    rs = list(ps)
    p_bds = [block_diag(p) for p in ps]
    for _ in range(n_sq):
        ps = [jnp.dot(ps[h].astype(BF16), p_bds[h], preferred_element_type=F32) for h in heads]
        p_bds = [block_diag(p) for p in ps]
        rs = [rs[h] + ps[h] + jnp.dot(rs[h].astype(BF16), p_bds[h], preferred_element_type=F32)
              for h in heads]

    egs = [jnp.exp(gcc) for gcc in gccs]
    us, wqs = [], []
    for h in heads:
        k = ks[h // 2]
        rhs = bcs[h] * jnp.concatenate([v_all[:, h * hd:(h + 1) * hd], k * egs[h]], axis=1)
        uw = rhs + jnp.dot(block_diag(rs[h]), rhs.astype(BF16), preferred_element_type=F32)
        us.append(uw[:, :hd])
        q_dec = qs[h // 2] * egs[h]
        wqs.append([jnp.concatenate([uw[c * ck:(c + 1) * ck, hd:], q_dec[c * ck:(c + 1) * ck]],
                                    axis=0).astype(BF16) for c in range(nck)])

    states = [s_ref[h] for h in heads]
    outs = [[] for _ in heads]
    for ci in range(nck):
        rows = slice(ci * ck, (ci + 1) * ck)
        wq_s = [jnp.dot(wqs[h][ci], states[h].astype(BF16), preferred_element_type=F32)
                for h in heads]
        v_news = [(us[h][rows] - wq_s[h][:ck]).astype(BF16) for h in heads]
        for h in heads:
            outs[h].append(wq_s[h][ck:] + jnp.dot(attns[h][ci], v_news[h],
                                                  preferred_element_type=F32))
        for h in heads:
            g_last = gccs[h][(ci + 1) * ck - 1:(ci + 1) * ck, :]
            k_dec = (ks[h // 2][rows] * jnp.exp(g_last - gccs[h][rows])).astype(BF16)
            states[h] = states[h] * jnp.exp(g_last) + lax.dot_general(
                k_dec, v_news[h], TN_DIMS, preferred_element_type=F32)

    for h in heads:
        s_ref[h] = states[h]
        o = jnp.concatenate(outs[h], axis=0)
        on = o * lax.rsqrt(jnp.mean(o * o, axis=-1, keepdims=True) + NORM_EPS) * nw
        cols = slice(h * hd, (h + 1) * hd)
        o_ref[:, cols] = (on * _silu(z_ref[:, cols])).astype(o_ref.dtype)


def gdn_core(proj, conv_w, gt, gh, norm_w, tb=256, pairs=GDN_PAIRS):
    l = proj.shape[0]
    hd = GDN_HEAD_DIM
    tb = min(tb, l)
    ng = GDN_QK_HEADS // pairs
    qw, vw = pairs * hd, 2 * pairs * hd
    return pl.pallas_call(
        functools.partial(_gdn_kernel, pairs=pairs),
        grid=(ng, l // tb),
        in_specs=[
            pl.BlockSpec((tb, qw), lambda g, b: (b, g)),
            pl.BlockSpec((tb, qw), lambda g, b: (b, ng + g)),
            pl.BlockSpec((tb, vw), lambda g, b: (b, ng + g)),
            pl.BlockSpec((tb, vw), lambda g, b: (b, 2 * ng + g)),
            pl.BlockSpec((CONV_K, qw), lambda g, b: (0, g)),
            pl.BlockSpec((CONV_K, qw), lambda g, b: (0, ng + g)),
            pl.BlockSpec((CONV_K, vw), lambda g, b: (0, ng + g)),
            pl.BlockSpec((tb, LANES), lambda g, b: (b, 0)),
            pl.BlockSpec((LANES, tb), lambda g, b: (0, b)),
            pl.BlockSpec((1, hd), lambda g, b: (0, 0)),
        ],
        out_specs=pl.BlockSpec((tb, vw), lambda g, b: (b, g)),
        out_shape=jax.ShapeDtypeStruct((l, GDN_V_HEADS * hd), BF16),
        scratch_shapes=[pltpu.VMEM((tb + SUBLANES, qw), F32),
                        pltpu.VMEM((tb + SUBLANES, qw), F32),
                        pltpu.VMEM((tb + SUBLANES, vw), F32),
                        pltpu.VMEM((2 * pairs, hd, hd), F32)],
        compiler_params=_cparams(("arbitrary", "arbitrary")),
        name="gdn_core",
    )(proj, proj, proj, proj, conv_w, conv_w, conv_w, gt, gh, norm_w)


def _kv_prep_kernel(kv_ref, fl_ref, knw_ref, fb_ref, kt_ref, v_ref, fh_ref, carry_ref):
    hd = FOX_HEAD_DIM
    nkv = FOX_KV_HEADS
    tm = kv_ref.shape[0]

    @pl.when(pl.program_id(0) == 0)
    def _():
        carry_ref[...] = jnp.zeros_like(carry_ref)

    for h in range(nkv):
        kh = kv_ref[:, h * hd:(h + 1) * hd]
        ms = jnp.mean(kh * kh, axis=-1, keepdims=True)
        kn = kh * lax.rsqrt(ms + NORM_EPS) * knw_ref[...]
        kt_ref[h * hd:(h + 1) * hd, :] = kn.T.astype(BF16)
    v_ref[...] = kv_ref[:, nkv * hd:2 * nkv * hd].astype(BF16)

    log_f = _log_sigmoid(fl_ref[...] + fb_ref[...])
    r = lax.broadcasted_iota(jnp.int32, (tm, tm), 0)
    c = lax.broadcasted_iota(jnp.int32, (tm, tm), 1)
    tril = jnp.where(r >= c, 1.0, 0.0).astype(F32)
    cs = jnp.dot(tril, log_f, preferred_element_type=F32,
                 precision=lax.Precision.HIGHEST) + carry_ref[...]
    carry_ref[...] = cs[tm - 1:tm, :]
    fh_ref[...] = (cs * LOG2E).T[0:fh_ref.shape[0], :]


def kv_prep(kv, fl, knw, fb_row, tm=512):
    l = kv.shape[0]
    tm = min(tm, l)
    kvd = FOX_KV_HEADS * FOX_HEAD_DIM
    nh = FOX_KV_HEADS * FOX_GROUP
    row = lambda i: (0, 0)
    return pl.pallas_call(
        _kv_prep_kernel,
        grid=(l // tm,),
        in_specs=[pl.BlockSpec((tm, 2 * kvd), lambda i: (i, 0)),
                  pl.BlockSpec((tm, LANES), lambda i: (i, 0)),
                  pl.BlockSpec((1, FOX_HEAD_DIM), row), pl.BlockSpec((1, LANES), row)],
        out_specs=[pl.BlockSpec((kvd, tm), lambda i: (0, i)),
                   pl.BlockSpec((tm, kvd), lambda i: (i, 0)),
                   pl.BlockSpec((nh, tm), lambda i: (0, i))],
        out_shape=[jax.ShapeDtypeStruct((kvd, l), BF16), jax.ShapeDtypeStruct((l, kvd), BF16),
                   jax.ShapeDtypeStruct((nh, l), F32)],
        scratch_shapes=[pltpu.VMEM((1, LANES), F32)],
        compiler_params=_cparams(("arbitrary",)),
        name="kv_prep",
    )(kv, fl, knw, fb_row)


def _fox_kernel(q_ref, gate_ref, kt_ref, v_ref, fh_ref, qnw_ref, o_ref, qn_ref, m_ref, l_ref,
                acc_ref, *, tk):
    i = pl.program_id(1)
    tq = q_ref.shape[0]
    hd = FOX_HEAD_DIM
    scale = hd ** -0.5 * LOG2E
    for g in range(FOX_GROUP):
        qh = q_ref[:, g * hd:(g + 1) * hd]
        ms = jnp.mean(qh * qh, axis=-1, keepdims=True)
        qn_ref[g] = (qh * (lax.rsqrt(ms + NORM_EPS) * scale) * qnw_ref[...]).astype(BF16)
    m_ref[...] = jnp.full(m_ref.shape, -jnp.inf, F32)
    l_ref[...] = jnp.zeros_like(l_ref)
    acc_ref[...] = jnp.zeros_like(acc_ref)
    n_full = (i * tq + 1) // tk
    key_minus_query = (lax.broadcasted_iota(jnp.int32, (tq, tk), 1)
                       - lax.broadcasted_iota(jnp.int32, (tq, tk), 0))
    causal = key_minus_query <= i * tq - n_full * tk

    def tile(j, masked):
        start = pl.multiple_of(j * tk, tk)
        kj = kt_ref[:, pl.ds(start, tk)]
        vj = v_ref[pl.ds(start, tk), :]
        heads = range(FOX_GROUP)
        ss = [jnp.dot(qn_ref[g], kj, preferred_element_type=F32)
              - fh_ref[g:g + 1, pl.ds(start, tk)] for g in heads]
        if masked:
            ss = [jnp.where(causal, s, -jnp.inf) for s in ss]
        sc = [[s[:, c * LANES:(c + 1) * LANES] for c in range(tk // LANES)] for s in ss]
        m_olds = [m_ref[g] for g in heads]
        m_news = [jnp.maximum(m_olds[g], jnp.max(functools.reduce(jnp.maximum, sc[g]),
                                                 axis=-1, keepdims=True)) for g in heads]
        alphas = [jnp.exp2(m_olds[g] - m_news[g]) for g in heads]
        pc = [[jnp.exp2(s - m_news[g]) for s in sc[g]] for g in heads]
        for g in heads:
            row_sum = jnp.sum(functools.reduce(jnp.add, pc[g]), axis=-1, keepdims=True)
            l_ref[g] = alphas[g] * l_ref[g] + row_sum
            m_ref[g] = m_news[g]
        pvs = [jnp.dot(jnp.concatenate(pc[g], axis=1).astype(BF16), vj,
                       preferred_element_type=F32) for g in heads]
        for g in heads:
            for c in range(hd // LANES):
                cols = slice(c * LANES, (c + 1) * LANES)
                acc_ref[g, :, cols] = alphas[g] * acc_ref[g, :, cols] + pvs[g][:, cols]

    def body(j, carry):
        tile(j, False)
        return carry

    lax.fori_loop(0, n_full, body, 0)
    tile(n_full, True)
    for g in range(FOX_GROUP):
        inv_l = 1.0 / l_ref[g]
        for c in range(hd // LANES):
            cols = slice(g * hd + c * LANES, g * hd + (c + 1) * LANES)
            o_ref[:, cols] = (acc_ref[g, :, c * LANES:(c + 1) * LANES] * inv_l
                              * jax.nn.sigmoid(gate_ref[:, cols])).astype(o_ref.dtype)


def fox_attention(qg, kt, v, fh, qnw, tq=256, tk=512):
    l = qg.shape[0]
    hd = FOX_HEAD_DIM
    gw = FOX_GROUP * hd
    tk = min(tk, l)
    tq = min(tq, tk)
    assert tk % tq == 0 and l % tk == 0
    return pl.pallas_call(
        functools.partial(_fox_kernel, tk=tk),
        grid=(FOX_KV_HEADS, l // tq),
        in_specs=[
            pl.BlockSpec((tq, gw), lambda h, i: (i, h)),
            pl.BlockSpec((tq, gw), lambda h, i: (i, FOX_KV_HEADS + h)),
            pl.BlockSpec((hd, l), lambda h, i: (h, 0), pipeline_mode=pl.Buffered(1)),
            pl.BlockSpec((l, hd), lambda h, i: (0, h), pipeline_mode=pl.Buffered(1)),
            pl.BlockSpec((FOX_GROUP, l), lambda h, i: (h, 0)),
            pl.BlockSpec((1, hd), lambda h, i: (0, 0)),
        ],
        out_specs=pl.BlockSpec((tq, gw), lambda h, i: (i, h)),
        out_shape=jax.ShapeDtypeStruct((l, FOX_KV_HEADS * gw), BF16),
        scratch_shapes=[pltpu.VMEM((FOX_GROUP, tq, hd), BF16),
                        pltpu.VMEM((FOX_GROUP, tq, LANES), F32),
                        pltpu.VMEM((FOX_GROUP, tq, LANES), F32),
                        pltpu.VMEM((FOX_GROUP, tq, hd), F32)],
        compiler_params=_cparams(("arbitrary", "arbitrary")),
        name="fox_attention",
    )(qg, qg, kt, v, fh, qnw)


def _pad_lanes(w):
    return jnp.pad(w, ((0, 0), (0, LANES - w.shape[1])))


def kernel(x, c, ada_w, ada_b, norm_mix, norm_ffn, ffn_w_in, ffn_w_out, gdn_w_in, gdn_conv,
           gdn_a_log, gdn_dt_bias, gdn_norm, gdn_w_out, kv_ada_w, kv_ada_b, kv_norm, kv_w, k_norm,
           forget_b, fox_w_in, q_norm, fox_w_out, out_ada_w, out_ada_b, out_norm):
    bsz, l, d = x.shape
    assert bsz == 1 and ada_w.shape[0] == 2 and gdn_w_in.shape[0] == 1 and fox_w_in.shape[0] == 1
    xs = x.reshape(l, d)
    c_col = c.reshape(d, 1)
    row = lambda t: t.reshape(1, -1)

    def mods(w3, b3, layer, n):
        m = adaln(c_col, w3, b3, layer)
        return [m[:, i * d:(i + 1) * d] for i in range(n)]

    ada_b3 = ada_b[:, None, :]
    conv_dim = gdn_conv.shape[2]
    gdn_main = conv_dim + GDN_V_HEADS * GDN_HEAD_DIM
    kvd = FOX_KV_HEADS * FOX_HEAD_DIM

    sh_m, sc_m, g_m, sh_f, sc_f, g_f = mods(ada_w, ada_b3, 0, 6)
    ffn_w_in_b, ffn_w_out_b = ffn_w_in.astype(BF16), ffn_w_out.astype(BF16)
    proj, small = nm_proj(xs, row(norm_mix[0]), sh_m, sc_m, gdn_w_in, 0, gdn_main,
                          _pad_lanes(gdn_w_in[0, :, gdn_main:]).astype(BF16))
    pad32 = lambda t: jnp.pad(t, (GDN_V_HEADS, LANES - 2 * GDN_V_HEADS)).reshape(1, LANES)
    gt, gh = gdn_gates(small, pad32(gdn_a_log[0]), pad32(gdn_dt_bias[0]))
    o = gdn_core(proj, gdn_conv[0], gt, gh, row(gdn_norm[0]))
    xs = proj_res(o, gdn_w_out, 0, xs, g_m)
    one = row(out_norm)
    xs = ffn(xs, row(norm_ffn[0]), sh_f, sc_f, g_f, ffn_w_in_b, ffn_w_out_b, 0, one, one, one,
             final=False)

    sh_k, sc_k = mods(kv_ada_w[None], kv_ada_b[None, None, :], 0, 2)
    kvp, fl = nm_proj(xs, row(kv_norm), sh_k, sc_k, kv_w[None], 0, 2 * kvd,
                      _pad_lanes(kv_w[:, 2 * kvd:]).astype(BF16))
    fb_row = jnp.pad(forget_b, (0, LANES - forget_b.shape[0])).reshape(1, LANES)
    k_sh, v_sh, fh = kv_prep(kvp, fl, row(k_norm), fb_row)

    sh_m, sc_m, g_m, sh_f, sc_f, g_f = mods(ada_w, ada_b3, 1, 6)
    qg = nm_proj(xs, row(norm_mix[1]), sh_m, sc_m, fox_w_in, 0, fox_w_in.shape[2])
    a = fox_attention(qg, k_sh, v_sh, fh, row(q_norm[0]))
    xs = proj_res(a, fox_w_out, 0, xs, g_m)
    sh_o, sc_o = mods(out_ada_w[None], out_ada_b[None, None, :], 0, 2)
    xs = ffn(xs, row(norm_ffn[1]), sh_f, sc_f, g_f, ffn_w_in_b, ffn_w_out_b, 1, row(out_norm),
             sh_o, sc_o, final=True)
    return xs.reshape(bsz, l, d)
```

```python
import functools
import math

import jax
import jax.numpy as jnp
from jax import lax
from jax.experimental import pallas as pl
from jax.experimental.pallas import tpu as pltpu

F32 = jnp.float32
BF16 = jnp.bfloat16
NORM_EPS = 1e-6
LOG2E = math.log2(math.e)

V7X_VMEM_BYTES = 64 * 1024 * 1024
VMEM_LIMIT_BYTES = 56 * 1024 * 1024
LANES = 128
SUBLANES = 8

GDN_CHUNK = 64
GDN_HEAD_DIM = 128
GDN_V_HEADS = 32
GDN_QK_HEADS = 16
CONV_K = 4
FOX_HEAD_DIM = 256
FOX_KV_HEADS = 2
FOX_GROUP = 8

NT_DIMS = (((1,), (1,)), ((), ()))
TN_DIMS = (((0,), (0,)), ((), ()))


def _cparams(sem):
    return pltpu.CompilerParams(dimension_semantics=sem, vmem_limit_bytes=VMEM_LIMIT_BYTES)


def _silu(x):
    return x * jax.nn.sigmoid(x)


def _softplus(x):
    return jnp.maximum(x, 0.0) + jnp.log1p(jnp.exp(-jnp.abs(x)))


def _log_sigmoid(x):
    return jnp.minimum(x, 0.0) - jnp.log1p(jnp.exp(-jnp.abs(x)))


def _norm_mod(x, nw, sh, sc):
    ms = jnp.mean(x * x, axis=-1, keepdims=True)
    y = x * lax.rsqrt(ms + NORM_EPS) * nw
    return y * (1.0 + sc) + sh


NORM_ROWS = 256


def _norm_mod_rows(dst_ref, src_fn, nw, sh, sc):
    tm = dst_ref.shape[0]
    step = min(NORM_ROWS, tm)
    for r in range(0, tm, step):
        rows = slice(r, r + step)
        dst_ref[rows, :] = _norm_mod(src_fn(rows), nw, sh, sc).astype(dst_ref.dtype)


def _adaln_kernel(c_ref, w_ref, b_ref, o_ref):
    cond = _silu(c_ref[...])
    o_ref[...] = jnp.sum(w_ref[...] * cond, axis=0, keepdims=True) + b_ref[...]


def adaln(c_col, w3, b3, layer, tn=1024):
    _, d, n = w3.shape
    tn = min(tn, n)
    return pl.pallas_call(
        _adaln_kernel,
        grid=(n // tn,),
        in_specs=[
            pl.BlockSpec((d, 1), lambda j: (0, 0)),
            pl.BlockSpec((None, d, tn), lambda j: (layer, 0, j)),
            pl.BlockSpec((None, 1, tn), lambda j: (layer, 0, j)),
        ],
        out_specs=pl.BlockSpec((1, tn), lambda j: (0, j)),
        out_shape=jax.ShapeDtypeStruct((1, n), F32),
        compiler_params=_cparams(("arbitrary",)),
        name="adaln",
    )(c_col, w3, b3)


def _nm_proj_kernel(x_ref, nw_ref, sh_ref, sc_ref, w_ref, o_ref, h_ref):
    @pl.when(pl.program_id(1) == 0)
    def _():
        _norm_mod_rows(h_ref, lambda rows: x_ref[rows, :], nw_ref[...], sh_ref[...], sc_ref[...])

    o_ref[...] = jnp.dot(h_ref[...], w_ref[...].astype(BF16), preferred_element_type=F32)


def nm_proj(x, nw, sh, sc, w3, layer, tm=1024, tn=1024):
    l, d = x.shape
    n = w3.shape[2]
    tm, tn = min(tm, l), min(tn, n)
    row = lambda i, j: (0, 0)
    return pl.pallas_call(
        _nm_proj_kernel,
        grid=(l // tm, n // tn),
        in_specs=[
            pl.BlockSpec((tm, d), lambda i, j: (i, 0)),
            pl.BlockSpec((1, d), row), pl.BlockSpec((1, d), row), pl.BlockSpec((1, d), row),
            pl.BlockSpec((None, d, tn), lambda i, j: (layer, 0, j)),
        ],
        out_specs=pl.BlockSpec((tm, tn), lambda i, j: (i, j)),
        out_shape=jax.ShapeDtypeStruct((l, n), F32),
        scratch_shapes=[pltpu.VMEM((tm, d), BF16)],
        compiler_params=_cparams(("arbitrary", "arbitrary")),
        name="nm_proj",
    )(x, nw, sh, sc, w3)


def _nm_proj_t_kernel(x_ref, nw_ref, sh_ref, sc_ref, wt_ref, *rest, side):
    if side:
        wst_ref, o_ref, os_ref, h_ref = rest
    else:
        o_ref, h_ref = rest
    tm = o_ref.shape[0]

    @pl.when(pl.program_id(1) == 0)
    def _():
        _norm_mod_rows(h_ref, lambda rows: x_ref[rows, :], nw_ref[...], sh_ref[...], sc_ref[...])
        if side:
            ns = wst_ref.shape[0]
            os_ref[:, 0:ns] = lax.dot_general(h_ref[...], wst_ref[...].astype(BF16), NT_DIMS,
                                              preferred_element_type=F32)
            os_ref[:, ns:] = jnp.zeros((tm, LANES - ns), F32)

    o_ref[...] = lax.dot_general(h_ref[...], wt_ref[...].astype(BF16), NT_DIMS,
                                 preferred_element_type=F32)


def nm_proj_t(x, nw, sh, sc, wt3, layer, row0, n, ns=0, tm=1024, tn=1024):
    l, d = x.shape
    tm, tn = min(tm, l), min(tn, n)
    assert n % tn == 0 and row0 % tn == 0
    side = ns > 0
    row = lambda i, j: (0, 0)
    in_specs = [
        pl.BlockSpec((tm, d), lambda i, j: (i, 0)),
        pl.BlockSpec((1, d), row), pl.BlockSpec((1, d), row), pl.BlockSpec((1, d), row),
        pl.BlockSpec((None, tn, d), lambda i, j: (layer, row0 // tn + j, 0)),
    ]
    out_specs = [pl.BlockSpec((tm, tn), lambda i, j: (i, j))]
    out_shape = [jax.ShapeDtypeStruct((l, n), F32)]
    args = [x, nw, sh, sc, wt3]
    if side:
        assert (row0 + n) % ns == 0
        in_specs.append(pl.BlockSpec((None, ns, d), lambda i, j: (layer, (row0 + n) // ns, 0)))
        out_specs.append(pl.BlockSpec((tm, LANES), lambda i, j: (i, 0)))
        out_shape.append(jax.ShapeDtypeStruct((l, LANES), F32))
        args.append(wt3)
    out = pl.pallas_call(
        functools.partial(_nm_proj_t_kernel, side=side),
        grid=(l // tm, n // tn),
        in_specs=in_specs,
        out_specs=out_specs,
        out_shape=out_shape,
        scratch_shapes=[pltpu.VMEM((tm, d), BF16)],
        compiler_params=_cparams(("arbitrary", "arbitrary")),
        name="nm_proj_t",
    )(*args)
    return out if side else out[0]


def _proj_res_kernel(a_ref, w_ref, x_ref, g_ref, o_ref):
    y = jnp.dot(a_ref[...], w_ref[...].astype(BF16), preferred_element_type=F32)
    o_ref[...] = x_ref[...] + g_ref[...] * y


def proj_res(a, w3, layer, x, g, tm=1024, tn=512):
    l, k = a.shape
    d = w3.shape[2]
    tm, tn = min(tm, l), min(tn, d)
    return pl.pallas_call(
        _proj_res_kernel,
        grid=(l // tm, d // tn),
        in_specs=[
            pl.BlockSpec((tm, k), lambda i, j: (i, 0)),
            pl.BlockSpec((None, k, tn), lambda i, j: (layer, 0, j)),
            pl.BlockSpec((tm, tn), lambda i, j: (i, j)),
            pl.BlockSpec((1, tn), lambda i, j: (0, j)),
        ],
        out_specs=pl.BlockSpec((tm, tn), lambda i, j: (i, j)),
        out_shape=jax.ShapeDtypeStruct((l, d), F32),
        compiler_params=_cparams(("arbitrary", "arbitrary")),
        name="proj_res",
    )(a, w3, x, g)


def _ffn_kernel(x_ref, nw_ref, sh_ref, sc_ref, g_ref, wg_ref, wu_ref, wo_ref, fnw_ref, fsh_ref,
                fsc_ref, o_ref, h_ref, acc_ref, *, final):
    j = pl.program_id(1)

    @pl.when(j == 0)
    def _():
        _norm_mod_rows(h_ref, lambda rows: x_ref[rows, :], nw_ref[...], sh_ref[...], sc_ref[...])
        acc_ref[...] = jnp.zeros_like(acc_ref)

    h = h_ref[...]
    gate = jnp.dot(h, wg_ref[...], preferred_element_type=F32)
    up = jnp.dot(h, wu_ref[...], preferred_element_type=F32)
    act = (_silu(gate) * up).astype(BF16)
    acc_ref[...] += jnp.dot(act, wo_ref[...], preferred_element_type=F32)

    @pl.when(j == pl.num_programs(1) - 1)
    def _():
        res = lambda rows: x_ref[rows, :] + g_ref[...] * acc_ref[rows, :]
        if final:
            _norm_mod_rows(o_ref, res, fnw_ref[...], fsh_ref[...], fsc_ref[...])
        else:
            o_ref[...] = res(slice(None))


def ffn(x, nw, sh, sc, g, w_in, w_out, layer, fnw, fsh, fsc, final, tm=512, th=512):
    l, d = x.shape
    hdim = w_out.shape[1]
    tm, th = min(tm, l), min(th, hdim)
    nh = hdim // th
    row = lambda i, j: (0, 0)
    vec = pl.BlockSpec((1, d), row)
    return pl.pallas_call(
        functools.partial(_ffn_kernel, final=final),
        grid=(l // tm, nh),
        in_specs=[
            pl.BlockSpec((tm, d), lambda i, j: (i, 0)),
            vec, vec, vec, vec,
            pl.BlockSpec((None, d, th), lambda i, j: (layer, 0, j)),
            pl.BlockSpec((None, d, th), lambda i, j: (layer, 0, j + nh)),
            pl.BlockSpec((None, th, d), lambda i, j: (layer, j, 0)),
            vec, vec, vec,
        ],
        out_specs=pl.BlockSpec((tm, d), lambda i, j: (i, 0)),
        out_shape=jax.ShapeDtypeStruct((l, d), F32),
        scratch_shapes=[pltpu.VMEM((tm, d), BF16), pltpu.VMEM((tm, d), F32)],
        compiler_params=_cparams(("arbitrary", "arbitrary")),
        name="ffn",
    )(x, nw, sh, sc, g, w_in, w_in, w_out, fnw, fsh, fsc)


def _gdn_gates_kernel(s_ref, alog_ref, dt_ref, gt_ref, gh_ref):
    x = s_ref[...]
    tm = x.shape[0]
    lane = lax.broadcasted_iota(jnp.int32, x.shape, 1)
    beta = jax.nn.sigmoid(x)
    g = -jnp.exp(alog_ref[...]) * _softplus(x + dt_ref[...])
    g = jnp.where((lane >= GDN_V_HEADS) & (lane < 2 * GDN_V_HEADS), g, 0.0)
    r = lax.broadcasted_iota(jnp.int32, (tm, tm), 0)
    c = lax.broadcasted_iota(jnp.int32, (tm, tm), 1)
    same_chunk = (r // GDN_CHUNK) == (c // GDN_CHUNK)
    tril = jnp.where((r >= c) & same_chunk, 1.0, 0.0).astype(F32)
    gcum = jnp.dot(tril, g, preferred_element_type=F32, precision=lax.Precision.HIGHEST)
    gt = jnp.where(lane < GDN_V_HEADS, beta, gcum)
    gt_ref[...] = gt
    gh_ref[...] = gt.T


def gdn_gates(small, alog_row, dt_row, tm=512):
    l = small.shape[0]
    tm = min(tm, l)
    row = lambda i: (0, 0)
    return pl.pallas_call(
        _gdn_gates_kernel,
        grid=(l // tm,),
        in_specs=[pl.BlockSpec((tm, LANES), lambda i: (i, 0)),
                  pl.BlockSpec((1, LANES), row), pl.BlockSpec((1, LANES), row)],
        out_specs=[pl.BlockSpec((tm, LANES), lambda i: (i, 0)),
                   pl.BlockSpec((LANES, tm), lambda i: (0, i))],
        out_shape=[jax.ShapeDtypeStruct((l, LANES), F32), jax.ShapeDtypeStruct((LANES, l), F32)],
        compiler_params=_cparams(("arbitrary",)),
        name="gdn_gates",
    )(small, alog_row, dt_row)


GDN_PAIRS = 4


def _conv_silu(x_ref, xe_ref, w_ref):
    tb = x_ref.shape[0]
    pad = SUBLANES
    xe_ref[pad:pad + tb, :] = x_ref[...]
    w = w_ref[...]
    acc = w[CONV_K - 1:CONV_K, :] * xe_ref[pad:pad + tb, :]
    for i in range(CONV_K - 1):
        off = pad - (CONV_K - 1) + i
        acc = acc + w[i:i + 1, :] * xe_ref[off:off + tb, :]
    xe_ref[0:pad, :] = xe_ref[tb:tb + pad, :]
    return _silu(acc)


def _gdn_kernel(q_ref, k_ref, v_ref, z_ref, wq_ref, wk_ref, wv_ref, gt_ref, gh_ref, nw_ref,
                o_ref, xq_ref, xk_ref, xv_ref, s_ref, *, pairs):
    gi = pl.program_id(0)
    tb = q_ref.shape[0]
    hd = GDN_HEAD_DIM
    ck = GDN_CHUNK
    nck = tb // ck
    n_sq = int(math.log2(ck)) - 1

    @pl.when(pl.program_id(1) == 0)
    def _():
        for xe_ref in (xq_ref, xk_ref, xv_ref):
            xe_ref[0:SUBLANES, :] = jnp.zeros((SUBLANES, xe_ref.shape[1]), F32)
        s_ref[...] = jnp.zeros_like(s_ref)

    q_all = _conv_silu(q_ref, xq_ref, wq_ref)
    k_all = _conv_silu(k_ref, xk_ref, wk_ref)
    v_all = _conv_silu(v_ref, xv_ref, wv_ref)

    head0 = 2 * pairs * gi
    gt = pltpu.roll(gt_ref[...], lax.rem(LANES - head0, LANES), axis=1)
    r_idx = lax.broadcasted_iota(jnp.int32, (tb, tb), 0)
    c_idx = lax.broadcasted_iota(jnp.int32, (tb, tb), 1)
    same = (r_idx // ck) == (c_idx // ck)
    causal = same & (r_idx >= c_idx)
    strict = same & (r_idx > c_idx)
    same_b = jnp.where(same, 1.0, 0.0).astype(BF16)
    nw = nw_ref[...]

    def block_diag(packed):
        return jnp.concatenate([packed.astype(BF16)] * nck, axis=0) * same_b

    heads = range(2 * pairs)
    qs, ks, kks, qks = [], [], [], []
    for pi in range(pairs):
        q = q_all[:, pi * hd:(pi + 1) * hd]
        k = k_all[:, pi * hd:(pi + 1) * hd]
        q = q * (lax.rsqrt(jnp.sum(q * q, axis=-1, keepdims=True) + NORM_EPS) * hd ** -0.5)
        k = k * lax.rsqrt(jnp.sum(k * k, axis=-1, keepdims=True) + NORM_EPS)
        kb = k.astype(BF16)
        qs.append(q)
        ks.append(k)
        kks.append(lax.dot_general(kb, kb, NT_DIMS, preferred_element_type=F32))
        qks.append(lax.dot_general(q.astype(BF16), kb, NT_DIMS, preferred_element_type=F32))

    bcs, gccs, attns, ps = [], [], [], []
    for h in heads:
        bc = gt[:, h:h + 1]
        gcc = gt[:, GDN_V_HEADS + h:GDN_V_HEADS + h + 1]
        gcr = gh_ref[pl.ds(GDN_V_HEADS + head0 + h, 1), :]
        decay = jnp.exp(jnp.where(causal, gcc - gcr, -jnp.inf))
        a_full = jnp.where(strict, kks[h // 2] * decay * bc, 0.0)
        attn = qks[h // 2] * decay
        attns.append([attn[c * ck:(c + 1) * ck, c * ck:(c + 1) * ck].astype(BF16)
                      for c in range(nck)])
        p = -a_full[0:ck]
        for c in range(1, nck):
            p = p - a_full[c * ck:(c + 1) * ck]
        bcs.append(bc)
        gccs.append(gcc)
        ps.append(p)

    rs = list(ps)
    p_bds = [block_diag(p) for p in ps]
    for _ in range(n_sq):
        ps = [jnp.dot(ps[h].astype(BF16), p_bds[h], preferred_element_type=F32) for h in heads]
        p_bds = [block_diag(p) for p in ps]
        rs = [rs[h] + ps[h] + jnp.dot(rs[h].astype(BF16), p_bds[h], preferred_element_type=F32)
              for h in heads]

    egs = [jnp.exp(gcc) for gcc in gccs]
    us, wqs = [], []
    for h in heads:
        k = ks[h // 2]
        rhs = bcs[h] * jnp.concatenate([v_all[:, h * hd:(h + 1) * hd], k * egs[h]], axis=1)
        uw = rhs + jnp.dot(block_diag(rs[h]), rhs.astype(BF16), preferred_element_type=F32)
        us.append(uw[:, :hd])
        q_dec = qs[h // 2] * egs[h]
        wqs.append([jnp.concatenate([uw[c * ck:(c + 1) * ck, hd:], q_dec[c * ck:(c + 1) * ck]],
                                    axis=0).astype(BF16) for c in range(nck)])

    states = [s_ref[h] for h in heads]
    outs = [[] for _ in heads]
    for ci in range(nck):
        rows = slice(ci * ck, (ci + 1) * ck)
        wq_s = [jnp.dot(wqs[h][ci], states[h].astype(BF16), preferred_element_type=F32)
                for h in heads]
        v_news = [(us[h][rows] - wq_s[h][:ck]).astype(BF16) for h in heads]
        for h in heads:
            outs[h].append(wq_s[h][ck:] + jnp.dot(attns[h][ci], v_news[h],
                                                  preferred_element_type=F32))
        for h in heads:
            g_last = gccs[h][(ci + 1) * ck - 1:(ci + 1) * ck, :]
            k_dec = (ks[h // 2][rows] * jnp.exp(g_last - gccs[h][rows])).astype(BF16)
            states[h] = states[h] * jnp.exp(g_last) + lax.dot_general(
                k_dec, v_news[h], TN_DIMS, preferred_element_type=F32)

    for h in heads:
        s_ref[h] = states[h]
        o = jnp.concatenate(outs[h], axis=0)
        on = o * lax.rsqrt(jnp.mean(o * o, axis=-1, keepdims=True) + NORM_EPS) * nw
        cols = slice(h * hd, (h + 1) * hd)
        o_ref[:, cols] = (on * _silu(z_ref[:, cols])).astype(o_ref.dtype)


def gdn_core(proj, conv_w, gt, gh, norm_w, tb=256, pairs=GDN_PAIRS):
    l = proj.shape[0]
    hd = GDN_HEAD_DIM
    tb = min(tb, l)
    ng = GDN_QK_HEADS // pairs
    qw, vw = pairs * hd, 2 * pairs * hd
    return pl.pallas_call(
        functools.partial(_gdn_kernel, pairs=pairs),
        grid=(ng, l // tb),
        in_specs=[
            pl.BlockSpec((tb, qw), lambda g, b: (b, g)),
            pl.BlockSpec((tb, qw), lambda g, b: (b, ng + g)),
            pl.BlockSpec((tb, vw), lambda g, b: (b, ng + g)),
            pl.BlockSpec((tb, vw), lambda g, b: (b, 2 * ng + g)),
            pl.BlockSpec((CONV_K, qw), lambda g, b: (0, g)),
            pl.BlockSpec((CONV_K, qw), lambda g, b: (0, ng + g)),
            pl.BlockSpec((CONV_K, vw), lambda g, b: (0, ng + g)),
            pl.BlockSpec((tb, LANES), lambda g, b: (b, 0)),
            pl.BlockSpec((LANES, tb), lambda g, b: (0, b)),
            pl.BlockSpec((1, hd), lambda g, b: (0, 0)),
        ],
        out_specs=pl.BlockSpec((tb, vw), lambda g, b: (b, g)),
        out_shape=jax.ShapeDtypeStruct((l, GDN_V_HEADS * hd), BF16),
        scratch_shapes=[pltpu.VMEM((tb + SUBLANES, qw), F32),
                        pltpu.VMEM((tb + SUBLANES, qw), F32),
                        pltpu.VMEM((tb + SUBLANES, vw), F32),
                        pltpu.VMEM((2 * pairs, hd, hd), F32)],
        compiler_params=_cparams(("arbitrary", "arbitrary")),
        name="gdn_core",
    )(proj, proj, proj, proj, conv_w, conv_w, conv_w, gt, gh, norm_w)


def _kv_prep_kernel(kv_ref, fl_ref, knw_ref, fb_ref, kt_ref, v_ref, fh_ref, carry_ref):
    hd = FOX_HEAD_DIM
    nkv = FOX_KV_HEADS
    tm = kv_ref.shape[0]

    @pl.when(pl.program_id(0) == 0)
    def _():
        carry_ref[...] = jnp.zeros_like(carry_ref)

    for h in range(nkv):
        kh = kv_ref[:, h * hd:(h + 1) * hd]
        ms = jnp.mean(kh * kh, axis=-1, keepdims=True)
        kn = kh * lax.rsqrt(ms + NORM_EPS) * knw_ref[...]
        kt_ref[h * hd:(h + 1) * hd, :] = kn.T.astype(BF16)
    v_ref[...] = kv_ref[:, nkv * hd:2 * nkv * hd].astype(BF16)

    log_f = _log_sigmoid(fl_ref[...] + fb_ref[...])
    r = lax.broadcasted_iota(jnp.int32, (tm, tm), 0)
    c = lax.broadcasted_iota(jnp.int32, (tm, tm), 1)
    tril = jnp.where(r >= c, 1.0, 0.0).astype(F32)
    cs = jnp.dot(tril, log_f, preferred_element_type=F32,
                 precision=lax.Precision.HIGHEST) + carry_ref[...]
    carry_ref[...] = cs[tm - 1:tm, :]
    fh_ref[...] = (cs * LOG2E).T[0:fh_ref.shape[0], :]


def kv_prep(kv, fl, knw, fb_row, tm=512):
    l = kv.shape[0]
    tm = min(tm, l)
    kvd = FOX_KV_HEADS * FOX_HEAD_DIM
    nh = FOX_KV_HEADS * FOX_GROUP
    row = lambda i: (0, 0)
    return pl.pallas_call(
        _kv_prep_kernel,
        grid=(l // tm,),
        in_specs=[pl.BlockSpec((tm, 2 * kvd), lambda i: (i, 0)),
                  pl.BlockSpec((tm, LANES), lambda i: (i, 0)),
                  pl.BlockSpec((1, FOX_HEAD_DIM), row), pl.BlockSpec((1, LANES), row)],
        out_specs=[pl.BlockSpec((kvd, tm), lambda i: (0, i)),
                   pl.BlockSpec((tm, kvd), lambda i: (i, 0)),
                   pl.BlockSpec((nh, tm), lambda i: (0, i))],
        out_shape=[jax.ShapeDtypeStruct((kvd, l), BF16), jax.ShapeDtypeStruct((l, kvd), BF16),
                   jax.ShapeDtypeStruct((nh, l), F32)],
        scratch_shapes=[pltpu.VMEM((1, LANES), F32)],
        compiler_params=_cparams(("arbitrary",)),
        name="kv_prep",
    )(kv, fl, knw, fb_row)


def _fox_kernel(q_ref, gate_ref, kt_ref, v_ref, fh_ref, qnw_ref, o_ref, qn_ref, m_ref, l_ref,
                acc_ref, *, tk):
    i = pl.program_id(1)
    tq = q_ref.shape[0]
    hd = FOX_HEAD_DIM
    scale = hd ** -0.5 * LOG2E
    for g in range(FOX_GROUP):
        qh = q_ref[:, g * hd:(g + 1) * hd]
        ms = jnp.mean(qh * qh, axis=-1, keepdims=True)
        qn_ref[g] = (qh * (lax.rsqrt(ms + NORM_EPS) * scale) * qnw_ref[...]).astype(BF16)
    m_ref[...] = jnp.full(m_ref.shape, -jnp.inf, F32)
    l_ref[...] = jnp.zeros_like(l_ref)
    acc_ref[...] = jnp.zeros_like(acc_ref)
    n_full = (i * tq + 1) // tk
    key_minus_query = (lax.broadcasted_iota(jnp.int32, (tq, tk), 1)
                       - lax.broadcasted_iota(jnp.int32, (tq, tk), 0))
    causal = key_minus_query <= i * tq - n_full * tk

    def tile(j, masked):
        start = pl.multiple_of(j * tk, tk)
        kj = kt_ref[:, pl.ds(start, tk)]
        vj = v_ref[pl.ds(start, tk), :]
        heads = range(FOX_GROUP)
        ss = [jnp.dot(qn_ref[g], kj, preferred_element_type=F32)
              - fh_ref[g:g + 1, pl.ds(start, tk)] for g in heads]
        if masked:
            ss = [jnp.where(causal, s, -jnp.inf) for s in ss]
        sc = [[s[:, c * LANES:(c + 1) * LANES] for c in range(tk // LANES)] for s in ss]
        m_olds = [m_ref[g] for g in heads]
        m_news = [jnp.maximum(m_olds[g], jnp.max(functools.reduce(jnp.maximum, sc[g]),
                                                 axis=-1, keepdims=True)) for g in heads]
        alphas = [jnp.exp2(m_olds[g] - m_news[g]) for g in heads]
        pc = [[jnp.exp2(s - m_news[g]) for s in sc[g]] for g in heads]
        for g in heads:
            row_sum = jnp.sum(functools.reduce(jnp.add, pc[g]), axis=-1, keepdims=True)
            l_ref[g] = alphas[g] * l_ref[g] + row_sum
            m_ref[g] = m_news[g]
        pvs = [jnp.dot(jnp.concatenate(pc[g], axis=1).astype(BF16), vj,
                       preferred_element_type=F32) for g in heads]
        for g in heads:
            for c in range(hd // LANES):
                cols = slice(c * LANES, (c + 1) * LANES)
                acc_ref[g, :, cols] = alphas[g] * acc_ref[g, :, cols] + pvs[g][:, cols]

    def body(j, carry):
        tile(j, False)
        return carry

    lax.fori_loop(0, n_full, body, 0)
    tile(n_full, True)
    for g in range(FOX_GROUP):
        inv_l = 1.0 / l_ref[g]
        for c in range(hd // LANES):
            cols = slice(g * hd + c * LANES, g * hd + (c + 1) * LANES)
            o_ref[:, cols] = (acc_ref[g, :, c * LANES:(c + 1) * LANES] * inv_l
                              * jax.nn.sigmoid(gate_ref[:, cols])).astype(o_ref.dtype)


def fox_attention(qg, kt, v, fh, qnw, tq=256, tk=1024):
    l = qg.shape[0]
    hd = FOX_HEAD_DIM
    gw = FOX_GROUP * hd
    tk = min(tk, l)
    tq = min(tq, tk)
    assert tk % tq == 0 and l % tk == 0
    return pl.pallas_call(
        functools.partial(_fox_kernel, tk=tk),
        grid=(FOX_KV_HEADS, l // tq),
        in_specs=[
            pl.BlockSpec((tq, gw), lambda h, i: (i, h)),
            pl.BlockSpec((tq, gw), lambda h, i: (i, FOX_KV_HEADS + h)),
            pl.BlockSpec((hd, l), lambda h, i: (h, 0), pipeline_mode=pl.Buffered(1)),
            pl.BlockSpec((l, hd), lambda h, i: (0, h), pipeline_mode=pl.Buffered(1)),
            pl.BlockSpec((FOX_GROUP, l), lambda h, i: (h, 0)),
            pl.BlockSpec((1, hd), lambda h, i: (0, 0)),
        ],
        out_specs=pl.BlockSpec((tq, gw), lambda h, i: (i, h)),
        out_shape=jax.ShapeDtypeStruct((l, FOX_KV_HEADS * gw), BF16),
        scratch_shapes=[pltpu.VMEM((FOX_GROUP, tq, hd), BF16),
                        pltpu.VMEM((FOX_GROUP, tq, LANES), F32),
                        pltpu.VMEM((FOX_GROUP, tq, LANES), F32),
                        pltpu.VMEM((FOX_GROUP, tq, hd), F32)],
        compiler_params=_cparams(("arbitrary", "arbitrary")),
        name="fox_attention",
    )(qg, qg, kt, v, fh, qnw)


def kernel(x, c, ada_w, ada_b, norm_mix, norm_ffn, ffn_w_in, ffn_w_out, gdn_w_in, gdn_conv,
           gdn_a_log, gdn_dt_bias, gdn_norm, gdn_w_out, kv_ada_w, kv_ada_b, kv_norm, kv_w, k_norm,
           forget_b, fox_w_in, q_norm, fox_w_out, out_ada_w, out_ada_b, out_norm):
    bsz, l, d = x.shape
    assert bsz == 1 and ada_w.shape[0] == 2 and gdn_w_in.shape[0] == 1 and fox_w_in.shape[0] == 1
    xs = x.reshape(l, d)
    c_col = c.reshape(d, 1)
    row = lambda t: t.reshape(1, -1)

    def mods(w3, b3, layer, n):
        m = adaln(c_col, w3, b3, layer)
        return [m[:, i * d:(i + 1) * d] for i in range(n)]

    ada_b3 = ada_b[:, None, :]
    conv_dim = gdn_conv.shape[2]
    gdn_main = conv_dim + GDN_V_HEADS * GDN_HEAD_DIM
    kvd = FOX_KV_HEADS * FOX_HEAD_DIM

    sh_m, sc_m, g_m, sh_f, sc_f, g_f = mods(ada_w, ada_b3, 0, 6)
    ffn_w_in_b, ffn_w_out_b = ffn_w_in.astype(BF16), ffn_w_out.astype(BF16)
    proj, small = nm_proj_t(xs, row(norm_mix[0]), sh_m, sc_m, jnp.swapaxes(gdn_w_in, 1, 2), 0, 0,
                            gdn_main, gdn_w_in.shape[2] - gdn_main)
    pad32 = lambda t: jnp.pad(t, (GDN_V_HEADS, LANES - 2 * GDN_V_HEADS)).reshape(1, LANES)
    gt, gh = gdn_gates(small, pad32(gdn_a_log[0]), pad32(gdn_dt_bias[0]))
    o = gdn_core(proj, gdn_conv[0], gt, gh, row(gdn_norm[0]))
    xs = proj_res(o, gdn_w_out, 0, xs, g_m)
    one = row(out_norm)
    xs = ffn(xs, row(norm_ffn[0]), sh_f, sc_f, g_f, ffn_w_in_b, ffn_w_out_b, 0, one, one, one,
             final=False)

    sh_k, sc_k = mods(kv_ada_w[None], kv_ada_b[None, None, :], 0, 2)
    kvp, fl = nm_proj_t(xs, row(kv_norm), sh_k, sc_k, kv_w.T[None], 0, 0, 2 * kvd,
                        kv_w.shape[1] - 2 * kvd)
    fb_row = jnp.pad(forget_b, (0, LANES - forget_b.shape[0])).reshape(1, LANES)
    k_sh, v_sh, fh = kv_prep(kvp, fl, row(k_norm), fb_row)

    sh_m, sc_m, g_m, sh_f, sc_f, g_f = mods(ada_w, ada_b3, 1, 6)
    qg = nm_proj(xs, row(norm_mix[1]), sh_m, sc_m, fox_w_in, 0)
    a = fox_attention(qg, k_sh, v_sh, fh, row(q_norm[0]))
    xs = proj_res(a, fox_w_out, 0, xs, g_m)
    sh_o, sc_o = mods(out_ada_w[None], out_ada_b[None, None, :], 0, 2)
    xs = ffn(xs, row(norm_ffn[1]), sh_f, sc_f, g_f, ffn_w_in_b, ffn_w_out_b, 1, row(out_norm),
             sh_o, sc_o, final=True)
    return xs.reshape(bsz, l, d)
```

```python
import functools
import math

import jax
import jax.numpy as jnp
from jax import lax
from jax.experimental import pallas as pl
from jax.experimental.pallas import tpu as pltpu

F32 = jnp.float32
BF16 = jnp.bfloat16
NORM_EPS = 1e-6
LOG2E = math.log2(math.e)

V7X_VMEM_BYTES = 64 * 1024 * 1024
VMEM_LIMIT_BYTES = 56 * 1024 * 1024
LANES = 128
SUBLANES = 8

GDN_CHUNK = 64
GDN_HEAD_DIM = 128
GDN_V_HEADS = 32
GDN_QK_HEADS = 16
CONV_K = 4
FOX_HEAD_DIM = 256
FOX_KV_HEADS = 2
FOX_GROUP = 8

NT_DIMS = (((1,), (1,)), ((), ()))
TN_DIMS = (((0,), (0,)), ((), ()))


def _cparams(sem):
    return pltpu.CompilerParams(dimension_semantics=sem, vmem_limit_bytes=VMEM_LIMIT_BYTES)


def _silu(x):
    return x * jax.nn.sigmoid(x)


def _softplus(x):
    return jnp.maximum(x, 0.0) + jnp.log1p(jnp.exp(-jnp.abs(x)))


def _log_sigmoid(x):
    return jnp.minimum(x, 0.0) - jnp.log1p(jnp.exp(-jnp.abs(x)))


def _norm_mod(x, nw, sh, sc):
    ms = jnp.mean(x * x, axis=-1, keepdims=True)
    y = x * lax.rsqrt(ms + NORM_EPS) * nw
    return y * (1.0 + sc) + sh


NORM_ROWS = 256


def _norm_mod_rows(dst_ref, src_fn, nw, sh, sc):
    tm = dst_ref.shape[0]
    step = min(NORM_ROWS, tm)
    for r in range(0, tm, step):
        rows = slice(r, r + step)
        dst_ref[rows, :] = _norm_mod(src_fn(rows), nw, sh, sc).astype(dst_ref.dtype)


def _adaln_kernel(c_ref, w_ref, b_ref, o_ref):
    cond = _silu(c_ref[...])
    o_ref[...] = jnp.sum(w_ref[...] * cond, axis=0, keepdims=True) + b_ref[...]


def adaln(c_col, w3, b3, layer, tn=1024):
    _, d, n = w3.shape
    tn = min(tn, n)
    return pl.pallas_call(
        _adaln_kernel,
        grid=(n // tn,),
        in_specs=[
            pl.BlockSpec((d, 1), lambda j: (0, 0)),
            pl.BlockSpec((None, d, tn), lambda j: (layer, 0, j)),
            pl.BlockSpec((None, 1, tn), lambda j: (layer, 0, j)),
        ],
        out_specs=pl.BlockSpec((1, tn), lambda j: (0, j)),
        out_shape=jax.ShapeDtypeStruct((1, n), F32),
        compiler_params=_cparams(("arbitrary",)),
        name="adaln",
    )(c_col, w3, b3)


def _nm_proj_kernel(x_ref, nw_ref, sh_ref, sc_ref, w_ref, o_ref, h_ref):
    @pl.when(pl.program_id(1) == 0)
    def _():
        _norm_mod_rows(h_ref, lambda rows: x_ref[rows, :], nw_ref[...], sh_ref[...], sc_ref[...])

    o_ref[...] = jnp.dot(h_ref[...], w_ref[...].astype(BF16), preferred_element_type=F32)


def nm_proj(x, nw, sh, sc, w3, layer, tm=1024, tn=1024):
    l, d = x.shape
    n = w3.shape[2]
    tm, tn = min(tm, l), min(tn, n)
    row = lambda i, j: (0, 0)
    return pl.pallas_call(
        _nm_proj_kernel,
        grid=(l // tm, n // tn),
        in_specs=[
            pl.BlockSpec((tm, d), lambda i, j: (i, 0)),
            pl.BlockSpec((1, d), row), pl.BlockSpec((1, d), row), pl.BlockSpec((1, d), row),
            pl.BlockSpec((None, d, tn), lambda i, j: (layer, 0, j)),
        ],
        out_specs=pl.BlockSpec((tm, tn), lambda i, j: (i, j)),
        out_shape=jax.ShapeDtypeStruct((l, n), F32),
        scratch_shapes=[pltpu.VMEM((tm, d), BF16)],
        compiler_params=_cparams(("arbitrary", "arbitrary")),
        name="nm_proj",
    )(x, nw, sh, sc, w3)


def _nm_proj_t_kernel(x_ref, nw_ref, sh_ref, sc_ref, wt_ref, *rest, side):
    if side:
        wst_ref, o_ref, os_ref, h_ref = rest
    else:
        o_ref, h_ref = rest
    tm = o_ref.shape[0]

    @pl.when(pl.program_id(1) == 0)
    def _():
        _norm_mod_rows(h_ref, lambda rows: x_ref[rows, :], nw_ref[...], sh_ref[...], sc_ref[...])
        if side:
            ns = wst_ref.shape[0]
            os_ref[:, 0:ns] = lax.dot_general(h_ref[...], wst_ref[...].astype(BF16), NT_DIMS,
                                              preferred_element_type=F32)
            os_ref[:, ns:] = jnp.zeros((tm, LANES - ns), F32)

    o_ref[...] = lax.dot_general(h_ref[...], wt_ref[...].astype(BF16), NT_DIMS,
                                 preferred_element_type=F32)


def nm_proj_t(x, nw, sh, sc, wt3, layer, row0, n, ns=0, tm=1024, tn=1024):
    l, d = x.shape
    tm, tn = min(tm, l), min(tn, n)
    assert n % tn == 0 and row0 % tn == 0
    side = ns > 0
    row = lambda i, j: (0, 0)
    in_specs = [
        pl.BlockSpec((tm, d), lambda i, j: (i, 0)),
        pl.BlockSpec((1, d), row), pl.BlockSpec((1, d), row), pl.BlockSpec((1, d), row),
        pl.BlockSpec((None, tn, d), lambda i, j: (layer, row0 // tn + j, 0)),
    ]
    out_specs = [pl.BlockSpec((tm, tn), lambda i, j: (i, j))]
    out_shape = [jax.ShapeDtypeStruct((l, n), F32)]
    args = [x, nw, sh, sc, wt3]
    if side:
        assert (row0 + n) % ns == 0
        in_specs.append(pl.BlockSpec((None, ns, d), lambda i, j: (layer, (row0 + n) // ns, 0)))
        out_specs.append(pl.BlockSpec((tm, LANES), lambda i, j: (i, 0)))
        out_shape.append(jax.ShapeDtypeStruct((l, LANES), F32))
        args.append(wt3)
    out = pl.pallas_call(
        functools.partial(_nm_proj_t_kernel, side=side),
        grid=(l // tm, n // tn),
        in_specs=in_specs,
        out_specs=out_specs,
        out_shape=out_shape,
        scratch_shapes=[pltpu.VMEM((tm, d), BF16)],
        compiler_params=_cparams(("arbitrary", "arbitrary")),
        name="nm_proj_t",
    )(*args)
    return out if side else out[0]


def _proj_res_kernel(a_ref, w_ref, x_ref, g_ref, o_ref):
    y = jnp.dot(a_ref[...], w_ref[...].astype(BF16), preferred_element_type=F32)
    o_ref[...] = x_ref[...] + g_ref[...] * y


def proj_res(a, w3, layer, x, g, tm=1024, tn=512):
    l, k = a.shape
    d = w3.shape[2]
    tm, tn = min(tm, l), min(tn, d)
    return pl.pallas_call(
        _proj_res_kernel,
        grid=(l // tm, d // tn),
        in_specs=[
            pl.BlockSpec((tm, k), lambda i, j: (i, 0)),
            pl.BlockSpec((None, k, tn), lambda i, j: (layer, 0, j)),
            pl.BlockSpec((tm, tn), lambda i, j: (i, j)),
            pl.BlockSpec((1, tn), lambda i, j: (0, j)),
        ],
        out_specs=pl.BlockSpec((tm, tn), lambda i, j: (i, j)),
        out_shape=jax.ShapeDtypeStruct((l, d), F32),
        compiler_params=_cparams(("arbitrary", "arbitrary")),
        name="proj_res",
    )(a, w3, x, g)


def _ffn_kernel(x_ref, nw_ref, sh_ref, sc_ref, g_ref, wg_ref, wu_ref, wo_ref, fnw_ref, fsh_ref,
                fsc_ref, o_ref, h_ref, acc_ref, *, final):
    j = pl.program_id(1)

    @pl.when(j == 0)
    def _():
        _norm_mod_rows(h_ref, lambda rows: x_ref[rows, :], nw_ref[...], sh_ref[...], sc_ref[...])
        acc_ref[...] = jnp.zeros_like(acc_ref)

    h = h_ref[...]
    gate = jnp.dot(h, wg_ref[...], preferred_element_type=F32)
    up = jnp.dot(h, wu_ref[...], preferred_element_type=F32)
    act = (_silu(gate) * up).astype(BF16)
    acc_ref[...] += jnp.dot(act, wo_ref[...], preferred_element_type=F32)

    @pl.when(j == pl.num_programs(1) - 1)
    def _():
        res = lambda rows: x_ref[rows, :] + g_ref[...] * acc_ref[rows, :]
        if final:
            _norm_mod_rows(o_ref, res, fnw_ref[...], fsh_ref[...], fsc_ref[...])
        else:
            o_ref[...] = res(slice(None))


def ffn(x, nw, sh, sc, g, w_in, w_out, layer, fnw, fsh, fsc, final, tm=512, th=512):
    l, d = x.shape
    hdim = w_out.shape[1]
    tm, th = min(tm, l), min(th, hdim)
    nh = hdim // th
    row = lambda i, j: (0, 0)
    vec = pl.BlockSpec((1, d), row)
    return pl.pallas_call(
        functools.partial(_ffn_kernel, final=final),
        grid=(l // tm, nh),
        in_specs=[
            pl.BlockSpec((tm, d), lambda i, j: (i, 0)),
            vec, vec, vec, vec,
            pl.BlockSpec((None, d, th), lambda i, j: (layer, 0, j)),
            pl.BlockSpec((None, d, th), lambda i, j: (layer, 0, j + nh)),
            pl.BlockSpec((None, th, d), lambda i, j: (layer, j, 0)),
            vec, vec, vec,
        ],
        out_specs=pl.BlockSpec((tm, d), lambda i, j: (i, 0)),
        out_shape=jax.ShapeDtypeStruct((l, d), F32),
        scratch_shapes=[pltpu.VMEM((tm, d), BF16), pltpu.VMEM((tm, d), F32)],
        compiler_params=_cparams(("arbitrary", "arbitrary")),
        name="ffn",
    )(x, nw, sh, sc, g, w_in, w_in, w_out, fnw, fsh, fsc)


def _gdn_gates_kernel(s_ref, alog_ref, dt_ref, gt_ref, gh_ref):
    x = s_ref[...]
    tm = x.shape[0]
    lane = lax.broadcasted_iota(jnp.int32, x.shape, 1)
    beta = jax.nn.sigmoid(x)
    g = -jnp.exp(alog_ref[...]) * _softplus(x + dt_ref[...])
    g = jnp.where((lane >= GDN_V_HEADS) & (lane < 2 * GDN_V_HEADS), g, 0.0)
    r = lax.broadcasted_iota(jnp.int32, (tm, tm), 0)
    c = lax.broadcasted_iota(jnp.int32, (tm, tm), 1)
    same_chunk = (r // GDN_CHUNK) == (c // GDN_CHUNK)
    tril = jnp.where((r >= c) & same_chunk, 1.0, 0.0).astype(F32)
    gcum = jnp.dot(tril, g, preferred_element_type=F32, precision=lax.Precision.HIGHEST)
    gt = jnp.where(lane < GDN_V_HEADS, beta, gcum)
    gt_ref[...] = gt
    gh_ref[...] = gt.T


def gdn_gates(small, alog_row, dt_row, tm=512):
    l = small.shape[0]
    tm = min(tm, l)
    row = lambda i: (0, 0)
    return pl.pallas_call(
        _gdn_gates_kernel,
        grid=(l // tm,),
        in_specs=[pl.BlockSpec((tm, LANES), lambda i: (i, 0)),
                  pl.BlockSpec((1, LANES), row), pl.BlockSpec((1, LANES), row)],
        out_specs=[pl.BlockSpec((tm, LANES), lambda i: (i, 0)),
                   pl.BlockSpec((LANES, tm), lambda i: (0, i))],
        out_shape=[jax.ShapeDtypeStruct((l, LANES), F32), jax.ShapeDtypeStruct((LANES, l), F32)],
        compiler_params=_cparams(("arbitrary",)),
        name="gdn_gates",
    )(small, alog_row, dt_row)


GDN_PAIRS = 8


def _conv_silu(x_ref, xe_ref, w_ref):
    tb = x_ref.shape[0]
    pad = SUBLANES
    xe_ref[pad:pad + tb, :] = x_ref[...]
    w = w_ref[...]
    acc = w[CONV_K - 1:CONV_K, :] * xe_ref[pad:pad + tb, :]
    for i in range(CONV_K - 1):
        off = pad - (CONV_K - 1) + i
        acc = acc + w[i:i + 1, :] * xe_ref[off:off + tb, :]
    xe_ref[0:pad, :] = xe_ref[tb:tb + pad, :]
    return _silu(acc)


def _cast_passengers(in_refs, out_refs):
    for src, dst in zip(in_refs, out_refs):
        dst[...] = src[...].astype(dst.dtype)


def _cast_specs(arrays, layer, n_steps, step_of):
    in_specs, out_specs, out_shape = [], [], []
    for a in arrays:
        rows = a.shape[0] // 2
        blk = rows // n_steps
        assert a.shape[0] % 2 == 0 and rows % n_steps == 0 and blk % 16 == 0
        in_specs.append(pl.BlockSpec((blk, a.shape[1]),
                                     lambda *g, _n=n_steps: (layer * _n + step_of(*g), 0)))
        out_specs.append(pl.BlockSpec((blk, a.shape[1]), lambda *g: (step_of(*g), 0)))
        out_shape.append(jax.ShapeDtypeStruct((rows, a.shape[1]), BF16))
    return in_specs, out_specs, out_shape


def _gdn_kernel(q_ref, k_ref, v_ref, z_ref, wq_ref, wk_ref, wv_ref, gt_ref, gh_ref, nw_ref,
                c0_ref, c1_ref, o_ref, co0_ref, co1_ref, xq_ref, xk_ref, xv_ref, s_ref, *, pairs):
    _cast_passengers((c0_ref, c1_ref), (co0_ref, co1_ref))
    gi = pl.program_id(0)
    tb = q_ref.shape[0]
    hd = GDN_HEAD_DIM
    ck = GDN_CHUNK
    nck = tb // ck
    n_sq = int(math.log2(ck)) - 1

    @pl.when(pl.program_id(1) == 0)
    def _():
        for xe_ref in (xq_ref, xk_ref, xv_ref):
            xe_ref[0:SUBLANES, :] = jnp.zeros((SUBLANES, xe_ref.shape[1]), F32)
        s_ref[...] = jnp.zeros_like(s_ref)

    q_all = _conv_silu(q_ref, xq_ref, wq_ref)
    k_all = _conv_silu(k_ref, xk_ref, wk_ref)
    v_all = _conv_silu(v_ref, xv_ref, wv_ref)

    head0 = 2 * pairs * gi
    gt = pltpu.roll(gt_ref[...], lax.rem(LANES - head0, LANES), axis=1)
    r_idx = lax.broadcasted_iota(jnp.int32, (tb, tb), 0)
    c_idx = lax.broadcasted_iota(jnp.int32, (tb, tb), 1)
    same = (r_idx // ck) == (c_idx // ck)
    causal = same & (r_idx >= c_idx)
    strict = same & (r_idx > c_idx)
    same_b = jnp.where(same, 1.0, 0.0).astype(BF16)
    nw = nw_ref[...]

    def block_diag(packed):
        return jnp.concatenate([packed.astype(BF16)] * nck, axis=0) * same_b

    heads = range(2 * pairs)
    qs, ks, kks, qks = [], [], [], []
    for pi in range(pairs):
        q = q_all[:, pi * hd:(pi + 1) * hd]
        k = k_all[:, pi * hd:(pi + 1) * hd]
        q = q * (lax.rsqrt(jnp.sum(q * q, axis=-1, keepdims=True) + NORM_EPS) * hd ** -0.5)
        k = k * lax.rsqrt(jnp.sum(k * k, axis=-1, keepdims=True) + NORM_EPS)
        kb = k.astype(BF16)
        qs.append(q)
        ks.append(k)
        kks.append(lax.dot_general(kb, kb, NT_DIMS, preferred_element_type=F32))
        qks.append(lax.dot_general(q.astype(BF16), kb, NT_DIMS, preferred_element_type=F32))

    bcs, gccs, attns, ps = [], [], [], []
    for h in heads:
        bc = gt[:, h:h + 1]
        gcc = gt[:, GDN_V_HEADS + h:GDN_V_HEADS + h + 1]
        gcr = gh_ref[pl.ds(GDN_V_HEADS + head0 + h, 1), :]
        decay = jnp.exp(jnp.where(causal, gcc - gcr, -jnp.inf))
        a_full = jnp.where(strict, kks[h // 2] * decay * bc, 0.0)
        attn = qks[h // 2] * decay
        attns.append([attn[c * ck:(c + 1) * ck, c * ck:(c + 1) * ck].astype(BF16)
                      for c in range(nck)])
        p = -a_full[0:ck]
        for c in range(1, nck):
            p = p - a_full[c * ck:(c + 1) * ck]
        bcs.append(bc)
        gccs.append(gcc)
        ps.append(p)

    rs = list(ps)
    p_bds = [block_diag(p) for p in ps]
    for _ in range(n_sq):
        ps = [jnp.dot(ps[h].astype(BF16), p_bds[h], preferred_element_type=F32) for h in heads]
        p_bds = [block_diag(p) for p in ps]
        rs = [rs[h] + ps[h] + jnp.dot(rs[h].astype(BF16), p_bds[h], preferred_element_type=F32)
              for h in heads]

    egs = [jnp.exp(gcc) for gcc in gccs]
    us, wqs = [], []
    for h in heads:
        k = ks[h // 2]
        rhs = bcs[h] * jnp.concatenate([v_all[:, h * hd:(h + 1) * hd], k * egs[h]], axis=1)
        uw = rhs + jnp.dot(block_diag(rs[h]), rhs.astype(BF16), preferred_element_type=F32)
        us.append(uw[:, :hd])
        q_dec = qs[h // 2] * egs[h]
        wqs.append([jnp.concatenate([uw[c * ck:(c + 1) * ck, hd:], q_dec[c * ck:(c + 1) * ck]],
                                    axis=0).astype(BF16) for c in range(nck)])

    states = [s_ref[h] for h in heads]
    outs = [[] for _ in heads]
    for ci in range(nck):
        rows = slice(ci * ck, (ci + 1) * ck)
        wq_s = [jnp.dot(wqs[h][ci], states[h].astype(BF16), preferred_element_type=F32)
                for h in heads]
        v_news = [(us[h][rows] - wq_s[h][:ck]).astype(BF16) for h in heads]
        for h in heads:
            outs[h].append(wq_s[h][ck:] + jnp.dot(attns[h][ci], v_news[h],
                                                  preferred_element_type=F32))
        for h in heads:
            g_last = gccs[h][(ci + 1) * ck - 1:(ci + 1) * ck, :]
            k_dec = (ks[h // 2][rows] * jnp.exp(g_last - gccs[h][rows])).astype(BF16)
            states[h] = states[h] * jnp.exp(g_last) + lax.dot_general(
                k_dec, v_news[h], TN_DIMS, preferred_element_type=F32)

    for h in heads:
        s_ref[h] = states[h]
        o = jnp.concatenate(outs[h], axis=0)
        on = o * lax.rsqrt(jnp.mean(o * o, axis=-1, keepdims=True) + NORM_EPS) * nw
        cols = slice(h * hd, (h + 1) * hd)
        o_ref[:, cols] = (on * _silu(z_ref[:, cols])).astype(o_ref.dtype)


def gdn_core(proj, conv_w, gt, gh, norm_w, cast_srcs, cast_layer, tb=256, pairs=GDN_PAIRS):
    l = proj.shape[0]
    hd = GDN_HEAD_DIM
    tb = min(tb, l)
    ng = GDN_QK_HEADS // pairs
    nb = l // tb
    qw, vw = pairs * hd, 2 * pairs * hd
    c_in, c_out, c_shape = _cast_specs(cast_srcs, cast_layer, ng * nb, lambda g, b: g * nb + b)
    return pl.pallas_call(
        functools.partial(_gdn_kernel, pairs=pairs),
        grid=(ng, nb),
        in_specs=[
            pl.BlockSpec((tb, qw), lambda g, b: (b, g)),
            pl.BlockSpec((tb, qw), lambda g, b: (b, ng + g)),
            pl.BlockSpec((tb, vw), lambda g, b: (b, ng + g)),
            pl.BlockSpec((tb, vw), lambda g, b: (b, 2 * ng + g)),
            pl.BlockSpec((CONV_K, qw), lambda g, b: (0, g)),
            pl.BlockSpec((CONV_K, qw), lambda g, b: (0, ng + g)),
            pl.BlockSpec((CONV_K, vw), lambda g, b: (0, ng + g)),
            pl.BlockSpec((tb, LANES), lambda g, b: (b, 0)),
            pl.BlockSpec((LANES, tb), lambda g, b: (0, b)),
            pl.BlockSpec((1, hd), lambda g, b: (0, 0)),
        ] + c_in,
        out_specs=[pl.BlockSpec((tb, vw), lambda g, b: (b, g))] + c_out,
        out_shape=[jax.ShapeDtypeStruct((l, GDN_V_HEADS * hd), BF16)] + c_shape,
        scratch_shapes=[pltpu.VMEM((tb + SUBLANES, qw), F32),
                        pltpu.VMEM((tb + SUBLANES, qw), F32),
                        pltpu.VMEM((tb + SUBLANES, vw), F32),
                        pltpu.VMEM((2 * pairs, hd, hd), F32)],
        compiler_params=_cparams(("arbitrary", "arbitrary")),
        name="gdn_core",
    )(proj, proj, proj, proj, conv_w, conv_w, conv_w, gt, gh, norm_w, *cast_srcs)


def _kv_prep_kernel(kv_ref, fl_ref, knw_ref, fb_ref, kt_ref, v_ref, fh_ref, carry_ref):
    hd = FOX_HEAD_DIM
    nkv = FOX_KV_HEADS
    tm = kv_ref.shape[0]

    @pl.when(pl.program_id(0) == 0)
    def _():
        carry_ref[...] = jnp.zeros_like(carry_ref)

    for h in range(nkv):
        kh = kv_ref[:, h * hd:(h + 1) * hd]
        ms = jnp.mean(kh * kh, axis=-1, keepdims=True)
        kn = kh * lax.rsqrt(ms + NORM_EPS) * knw_ref[...]
        kt_ref[h * hd:(h + 1) * hd, :] = kn.T.astype(BF16)
    v_ref[...] = kv_ref[:, nkv * hd:2 * nkv * hd].astype(BF16)

    log_f = _log_sigmoid(fl_ref[...] + fb_ref[...])
    r = lax.broadcasted_iota(jnp.int32, (tm, tm), 0)
    c = lax.broadcasted_iota(jnp.int32, (tm, tm), 1)
    tril = jnp.where(r >= c, 1.0, 0.0).astype(F32)
    cs = jnp.dot(tril, log_f, preferred_element_type=F32,
                 precision=lax.Precision.HIGHEST) + carry_ref[...]
    carry_ref[...] = cs[tm - 1:tm, :]
    fh_ref[...] = (cs * LOG2E).T[0:fh_ref.shape[0], :]


def kv_prep(kv, fl, knw, fb_row, tm=512):
    l = kv.shape[0]
    tm = min(tm, l)
    kvd = FOX_KV_HEADS * FOX_HEAD_DIM
    nh = FOX_KV_HEADS * FOX_GROUP
    row = lambda i: (0, 0)
    return pl.pallas_call(
        _kv_prep_kernel,
        grid=(l // tm,),
        in_specs=[pl.BlockSpec((tm, 2 * kvd), lambda i: (i, 0)),
                  pl.BlockSpec((tm, LANES), lambda i: (i, 0)),
                  pl.BlockSpec((1, FOX_HEAD_DIM), row), pl.BlockSpec((1, LANES), row)],
        out_specs=[pl.BlockSpec((kvd, tm), lambda i: (0, i)),
                   pl.BlockSpec((tm, kvd), lambda i: (i, 0)),
                   pl.BlockSpec((nh, tm), lambda i: (0, i))],
        out_shape=[jax.ShapeDtypeStruct((kvd, l), BF16), jax.ShapeDtypeStruct((l, kvd), BF16),
                   jax.ShapeDtypeStruct((nh, l), F32)],
        scratch_shapes=[pltpu.VMEM((1, LANES), F32)],
        compiler_params=_cparams(("arbitrary",)),
        name="kv_prep",
    )(kv, fl, knw, fb_row)


def _fox_kernel(q_ref, gate_ref, kt_ref, v_ref, fh_ref, qnw_ref, c0_ref, c1_ref, o_ref, co0_ref,
                co1_ref, qn_ref, m_ref, l_ref, acc_ref, *, tk):
    _cast_passengers((c0_ref, c1_ref), (co0_ref, co1_ref))
    i = pl.program_id(1)
    tq = q_ref.shape[0]
    hd = FOX_HEAD_DIM
    scale = hd ** -0.5 * LOG2E
    for g in range(FOX_GROUP):
        qh = q_ref[:, g * hd:(g + 1) * hd]
        ms = jnp.mean(qh * qh, axis=-1, keepdims=True)
        qn_ref[g] = (qh * (lax.rsqrt(ms + NORM_EPS) * scale) * qnw_ref[...]).astype(BF16)
    m_ref[...] = jnp.full(m_ref.shape, -jnp.inf, F32)
    l_ref[...] = jnp.zeros_like(l_ref)
    acc_ref[...] = jnp.zeros_like(acc_ref)
    n_full = (i * tq + 1) // tk
    key_minus_query = (lax.broadcasted_iota(jnp.int32, (tq, tk), 1)
                       - lax.broadcasted_iota(jnp.int32, (tq, tk), 0))
    causal = key_minus_query <= i * tq - n_full * tk

    def tile(j, masked):
        start = pl.multiple_of(j * tk, tk)
        kj = kt_ref[:, pl.ds(start, tk)]
        vj = v_ref[pl.ds(start, tk), :]
        heads = range(FOX_GROUP)
        ss = [jnp.dot(qn_ref[g], kj, preferred_element_type=F32)
              - fh_ref[g:g + 1, pl.ds(start, tk)] for g in heads]
        if masked:
            ss = [jnp.where(causal, s, -jnp.inf) for s in ss]
        sc = [[s[:, c * LANES:(c + 1) * LANES] for c in range(tk // LANES)] for s in ss]
        m_olds = [m_ref[g] for g in heads]
        m_news = [jnp.maximum(m_olds[g], jnp.max(functools.reduce(jnp.maximum, sc[g]),
                                                 axis=-1, keepdims=True)) for g in heads]
        alphas = [jnp.exp2(m_olds[g] - m_news[g]) for g in heads]
        pc = [[jnp.exp2(s - m_news[g]) for s in sc[g]] for g in heads]
        for g in heads:
            row_sum = jnp.sum(functools.reduce(jnp.add, pc[g]), axis=-1, keepdims=True)
            l_ref[g] = alphas[g] * l_ref[g] + row_sum
            m_ref[g] = m_news[g]
        pvs = [jnp.dot(jnp.concatenate(pc[g], axis=1).astype(BF16), vj,
                       preferred_element_type=F32) for g in heads]
        for g in heads:
            for c in range(hd // LANES):
                cols = slice(c * LANES, (c + 1) * LANES)
                acc_ref[g, :, cols] = alphas[g] * acc_ref[g, :, cols] + pvs[g][:, cols]

    def body(j, carry):
        tile(j, False)
        return carry

    lax.fori_loop(0, n_full, body, 0)
    tile(n_full, True)
    for g in range(FOX_GROUP):
        inv_l = 1.0 / l_ref[g]
        for c in range(hd // LANES):
            cols = slice(g * hd + c * LANES, g * hd + (c + 1) * LANES)
            o_ref[:, cols] = (acc_ref[g, :, c * LANES:(c + 1) * LANES] * inv_l
                              * jax.nn.sigmoid(gate_ref[:, cols])).astype(o_ref.dtype)


def fox_attention(qg, kt, v, fh, qnw, cast_srcs, cast_layer, tq=256, tk=1024):
    l = qg.shape[0]
    hd = FOX_HEAD_DIM
    gw = FOX_GROUP * hd
    tk = min(tk, l)
    tq = min(tq, tk)
    assert tk % tq == 0 and l % tk == 0
    nq = l // tq
    c_in, c_out, c_shape = _cast_specs(cast_srcs, cast_layer, FOX_KV_HEADS * nq,
                                       lambda h, i: h * nq + i)
    return pl.pallas_call(
        functools.partial(_fox_kernel, tk=tk),
        grid=(FOX_KV_HEADS, l // tq),
        in_specs=[
            pl.BlockSpec((tq, gw), lambda h, i: (i, h)),
            pl.BlockSpec((tq, gw), lambda h, i: (i, FOX_KV_HEADS + h)),
            pl.BlockSpec((hd, l), lambda h, i: (h, 0), pipeline_mode=pl.Buffered(1)),
            pl.BlockSpec((l, hd), lambda h, i: (0, h), pipeline_mode=pl.Buffered(1)),
            pl.BlockSpec((FOX_GROUP, l), lambda h, i: (h, 0)),
            pl.BlockSpec((1, hd), lambda h, i: (0, 0)),
        ] + c_in,
        out_specs=[pl.BlockSpec((tq, gw), lambda h, i: (i, h))] + c_out,
        out_shape=[jax.ShapeDtypeStruct((l, FOX_KV_HEADS * gw), BF16)] + c_shape,
        scratch_shapes=[pltpu.VMEM((FOX_GROUP, tq, hd), BF16),
                        pltpu.VMEM((FOX_GROUP, tq, LANES), F32),
                        pltpu.VMEM((FOX_GROUP, tq, LANES), F32),
                        pltpu.VMEM((FOX_GROUP, tq, hd), F32)],
        compiler_params=_cparams(("arbitrary", "arbitrary")),
        name="fox_attention",
    )(qg, qg, kt, v, fh, qnw, *cast_srcs)


def kernel(x, c, ada_w, ada_b, norm_mix, norm_ffn, ffn_w_in, ffn_w_out, gdn_w_in, gdn_conv,
           gdn_a_log, gdn_dt_bias, gdn_norm, gdn_w_out, kv_ada_w, kv_ada_b, kv_norm, kv_w, k_norm,
           forget_b, fox_w_in, q_norm, fox_w_out, out_ada_w, out_ada_b, out_norm):
    bsz, l, d = x.shape
    assert bsz == 1 and ada_w.shape[0] == 2 and gdn_w_in.shape[0] == 1 and fox_w_in.shape[0] == 1
    xs = x.reshape(l, d)
    c_col = c.reshape(d, 1)
    row = lambda t: t.reshape(1, -1)

    def mods(w3, b3, layer, n):
        m = adaln(c_col, w3, b3, layer)
        return [m[:, i * d:(i + 1) * d] for i in range(n)]

    ada_b3 = ada_b[:, None, :]
    conv_dim = gdn_conv.shape[2]
    gdn_main = conv_dim + GDN_V_HEADS * GDN_HEAD_DIM
    kvd = FOX_KV_HEADS * FOX_HEAD_DIM

    sh_m, sc_m, g_m, sh_f, sc_f, g_f = mods(ada_w, ada_b3, 0, 6)
    hid = ffn_w_out.shape[1]
    ffn_srcs = (ffn_w_in.reshape(-1, 2 * hid), ffn_w_out.reshape(-1, hid))
    ffn_ws = lambda w_in_b, w_out_b: (w_in_b[None], w_out_b.reshape(1, hid, d))
    proj, small = nm_proj_t(xs, row(norm_mix[0]), sh_m, sc_m, jnp.swapaxes(gdn_w_in, 1, 2), 0, 0,
                            gdn_main, gdn_w_in.shape[2] - gdn_main)
    pad32 = lambda t: jnp.pad(t, (GDN_V_HEADS, LANES - 2 * GDN_V_HEADS)).reshape(1, LANES)
    gt, gh = gdn_gates(small, pad32(gdn_a_log[0]), pad32(gdn_dt_bias[0]))
    o, w_in_b, w_out_b = gdn_core(proj, gdn_conv[0], gt, gh, row(gdn_norm[0]), ffn_srcs, 0)
    xs = proj_res(o, gdn_w_out, 0, xs, g_m)
    one = row(out_norm)
    xs = ffn(xs, row(norm_ffn[0]), sh_f, sc_f, g_f, *ffn_ws(w_in_b, w_out_b), 0, one, one, one,
             final=False)

    sh_k, sc_k = mods(kv_ada_w[None], kv_ada_b[None, None, :], 0, 2)
    kvp, fl = nm_proj_t(xs, row(kv_norm), sh_k, sc_k, kv_w.T[None], 0, 0, 2 * kvd,
                        kv_w.shape[1] - 2 * kvd)
    fb_row = jnp.pad(forget_b, (0, LANES - forget_b.shape[0])).reshape(1, LANES)
    k_sh, v_sh, fh = kv_prep(kvp, fl, row(k_norm), fb_row)

    sh_m, sc_m, g_m, sh_f, sc_f, g_f = mods(ada_w, ada_b3, 1, 6)
    qg = nm_proj(xs, row(norm_mix[1]), sh_m, sc_m, fox_w_in, 0)
    a, w_in_b, w_out_b = fox_attention(qg, k_sh, v_sh, fh, row(q_norm[0]), ffn_srcs, 1)
    xs = proj_res(a, fox_w_out, 0, xs, g_m)
    sh_o, sc_o = mods(out_ada_w[None], out_ada_b[None, None, :], 0, 2)
    xs = ffn(xs, row(norm_ffn[1]), sh_f, sc_f, g_f, *ffn_ws(w_in_b, w_out_b), 0, row(out_norm),
             sh_o, sc_o, final=True)
    return xs.reshape(bsz, l, d)
```

```python
import functools
import math

import jax
import jax.numpy as jnp
from jax import lax
from jax.experimental import pallas as pl
from jax.experimental.pallas import tpu as pltpu

F32 = jnp.float32
BF16 = jnp.bfloat16
NORM_EPS = 1e-6
LOG2E = math.log2(math.e)

V7X_VMEM_BYTES = 64 * 1024 * 1024
VMEM_LIMIT_BYTES = 56 * 1024 * 1024
LANES = 128
SUBLANES = 8

GDN_CHUNK = 64
GDN_HEAD_DIM = 128
GDN_V_HEADS = 32
GDN_QK_HEADS = 16
CONV_K = 4
FOX_HEAD_DIM = 256
FOX_KV_HEADS = 2
FOX_GROUP = 8

NT_DIMS = (((1,), (1,)), ((), ()))
TN_DIMS = (((0,), (0,)), ((), ()))


def _cparams(sem):
    return pltpu.CompilerParams(dimension_semantics=sem, vmem_limit_bytes=VMEM_LIMIT_BYTES)


def _silu(x):
    return x * jax.nn.sigmoid(x)


def _softplus(x):
    return jnp.maximum(x, 0.0) + jnp.log1p(jnp.exp(-jnp.abs(x)))


def _log_sigmoid(x):
    return jnp.minimum(x, 0.0) - jnp.log1p(jnp.exp(-jnp.abs(x)))


def _norm_mod(x, nw, sh, sc):
    ms = jnp.mean(x * x, axis=-1, keepdims=True)
    y = x * lax.rsqrt(ms + NORM_EPS) * nw
    return y * (1.0 + sc) + sh


NORM_ROWS = 256


def _norm_mod_rows(dst_ref, src_fn, nw, sh, sc):
    tm = dst_ref.shape[0]
    step = min(NORM_ROWS, tm)
    for r in range(0, tm, step):
        rows = slice(r, r + step)
        dst_ref[rows, :] = _norm_mod(src_fn(rows), nw, sh, sc).astype(dst_ref.dtype)


def _adaln_kernel(c_ref, w_ref, b_ref, o_ref):
    cond = _silu(c_ref[...])
    o_ref[...] = jnp.sum(w_ref[...] * cond, axis=0, keepdims=True) + b_ref[...]


def adaln(c_col, w3, b3, layer, tn=1024):
    _, d, n = w3.shape
    tn = min(tn, n)
    return pl.pallas_call(
        _adaln_kernel,
        grid=(n // tn,),
        in_specs=[
            pl.BlockSpec((d, 1), lambda j: (0, 0)),
            pl.BlockSpec((None, d, tn), lambda j: (layer, 0, j)),
            pl.BlockSpec((None, 1, tn), lambda j: (layer, 0, j)),
        ],
        out_specs=pl.BlockSpec((1, tn), lambda j: (0, j)),
        out_shape=jax.ShapeDtypeStruct((1, n), F32),
        compiler_params=_cparams(("arbitrary",)),
        name="adaln",
    )(c_col, w3, b3)


def _nm_proj_kernel(x_ref, nw_ref, sh_ref, sc_ref, w_ref, o_ref, h_ref):
    @pl.when(pl.program_id(1) == 0)
    def _():
        _norm_mod_rows(h_ref, lambda rows: x_ref[rows, :], nw_ref[...], sh_ref[...], sc_ref[...])

    o_ref[...] = jnp.dot(h_ref[...], w_ref[...].astype(BF16), preferred_element_type=F32)


def nm_proj(x, nw, sh, sc, w3, layer, tm=1024, tn=1024):
    l, d = x.shape
    n = w3.shape[2]
    tm, tn = min(tm, l), min(tn, n)
    row = lambda i, j: (0, 0)
    return pl.pallas_call(
        _nm_proj_kernel,
        grid=(l // tm, n // tn),
        in_specs=[
            pl.BlockSpec((tm, d), lambda i, j: (i, 0)),
            pl.BlockSpec((1, d), row), pl.BlockSpec((1, d), row), pl.BlockSpec((1, d), row),
            pl.BlockSpec((None, d, tn), lambda i, j: (layer, 0, j)),
        ],
        out_specs=pl.BlockSpec((tm, tn), lambda i, j: (i, j)),
        out_shape=jax.ShapeDtypeStruct((l, n), F32),
        scratch_shapes=[pltpu.VMEM((tm, d), BF16)],
        compiler_params=_cparams(("arbitrary", "arbitrary")),
        name="nm_proj",
    )(x, nw, sh, sc, w3)


def _nm_proj_t_kernel(x_ref, nw_ref, sh_ref, sc_ref, wt_ref, *rest, side):
    if side:
        wst_ref, o_ref, os_ref, h_ref = rest
    else:
        o_ref, h_ref = rest
    tm = o_ref.shape[0]

    @pl.when(pl.program_id(1) == 0)
    def _():
        _norm_mod_rows(h_ref, lambda rows: x_ref[rows, :], nw_ref[...], sh_ref[...], sc_ref[...])
        if side:
            ns = wst_ref.shape[0]
            os_ref[:, 0:ns] = lax.dot_general(h_ref[...], wst_ref[...].astype(BF16), NT_DIMS,
                                              preferred_element_type=F32)
            os_ref[:, ns:] = jnp.zeros((tm, LANES - ns), F32)

    o_ref[...] = lax.dot_general(h_ref[...], wt_ref[...].astype(BF16), NT_DIMS,
                                 preferred_element_type=F32)


def nm_proj_t(x, nw, sh, sc, wt3, layer, row0, n, ns=0, tm=1024, tn=1024):
    l, d = x.shape
    tm, tn = min(tm, l), min(tn, n)
    assert n % tn == 0 and row0 % tn == 0
    side = ns > 0
    row = lambda i, j: (0, 0)
    in_specs = [
        pl.BlockSpec((tm, d), lambda i, j: (i, 0)),
        pl.BlockSpec((1, d), row), pl.BlockSpec((1, d), row), pl.BlockSpec((1, d), row),
        pl.BlockSpec((None, tn, d), lambda i, j: (layer, row0 // tn + j, 0)),
    ]
    out_specs = [pl.BlockSpec((tm, tn), lambda i, j: (i, j))]
    out_shape = [jax.ShapeDtypeStruct((l, n), F32)]
    args = [x, nw, sh, sc, wt3]
    if side:
        assert (row0 + n) % ns == 0
        in_specs.append(pl.BlockSpec((None, ns, d), lambda i, j: (layer, (row0 + n) // ns, 0)))
        out_specs.append(pl.BlockSpec((tm, LANES), lambda i, j: (i, 0)))
        out_shape.append(jax.ShapeDtypeStruct((l, LANES), F32))
        args.append(wt3)
    out = pl.pallas_call(
        functools.partial(_nm_proj_t_kernel, side=side),
        grid=(l // tm, n // tn),
        in_specs=in_specs,
        out_specs=out_specs,
        out_shape=out_shape,
        scratch_shapes=[pltpu.VMEM((tm, d), BF16)],
        compiler_params=_cparams(("arbitrary", "arbitrary")),
        name="nm_proj_t",
    )(*args)
    return out if side else out[0]


def _proj_res_kernel(a_ref, w_ref, x_ref, g_ref, o_ref):
    y = jnp.dot(a_ref[...], w_ref[...].astype(BF16), preferred_element_type=F32)
    o_ref[...] = x_ref[...] + g_ref[...] * y


def proj_res(a, w3, layer, x, g, tm=1024, tn=512):
    l, k = a.shape
    d = w3.shape[2]
    tm, tn = min(tm, l), min(tn, d)
    return pl.pallas_call(
        _proj_res_kernel,
        grid=(l // tm, d // tn),
        in_specs=[
            pl.BlockSpec((tm, k), lambda i, j: (i, 0)),
            pl.BlockSpec((None, k, tn), lambda i, j: (layer, 0, j)),
            pl.BlockSpec((tm, tn), lambda i, j: (i, j)),
            pl.BlockSpec((1, tn), lambda i, j: (0, j)),
        ],
        out_specs=pl.BlockSpec((tm, tn), lambda i, j: (i, j)),
        out_shape=jax.ShapeDtypeStruct((l, d), F32),
        compiler_params=_cparams(("arbitrary", "arbitrary")),
        name="proj_res",
    )(a, w3, x, g)


def _ffn_kernel(x_ref, nw_ref, sh_ref, sc_ref, g_ref, wg_ref, wu_ref, wo_ref, fnw_ref, fsh_ref,
                fsc_ref, o_ref, h_ref, acc_ref, *, final):
    j = pl.program_id(1)

    @pl.when(j == 0)
    def _():
        _norm_mod_rows(h_ref, lambda rows: x_ref[rows, :], nw_ref[...], sh_ref[...], sc_ref[...])
        acc_ref[...] = jnp.zeros_like(acc_ref)

    h = h_ref[...]
    gate = jnp.dot(h, wg_ref[...], preferred_element_type=F32)
    up = jnp.dot(h, wu_ref[...], preferred_element_type=F32)
    act = (_silu(gate) * up).astype(BF16)
    acc_ref[...] += jnp.dot(act, wo_ref[...], preferred_element_type=F32)

    @pl.when(j == pl.num_programs(1) - 1)
    def _():
        res = lambda rows: x_ref[rows, :] + g_ref[...] * acc_ref[rows, :]
        if final:
            _norm_mod_rows(o_ref, res, fnw_ref[...], fsh_ref[...], fsc_ref[...])
        else:
            o_ref[...] = res(slice(None))


def ffn(x, nw, sh, sc, g, w_in, w_out, layer, fnw, fsh, fsc, final, tm=512, th=512):
    l, d = x.shape
    hdim = w_out.shape[1]
    tm, th = min(tm, l), min(th, hdim)
    nh = hdim // th
    row = lambda i, j: (0, 0)
    vec = pl.BlockSpec((1, d), row)
    return pl.pallas_call(
        functools.partial(_ffn_kernel, final=final),
        grid=(l // tm, nh),
        in_specs=[
            pl.BlockSpec((tm, d), lambda i, j: (i, 0)),
            vec, vec, vec, vec,
            pl.BlockSpec((None, d, th), lambda i, j: (layer, 0, j)),
            pl.BlockSpec((None, d, th), lambda i, j: (layer, 0, j + nh)),
            pl.BlockSpec((None, th, d), lambda i, j: (layer, j, 0)),
            vec, vec, vec,
        ],
        out_specs=pl.BlockSpec((tm, d), lambda i, j: (i, 0)),
        out_shape=jax.ShapeDtypeStruct((l, d), F32),
        scratch_shapes=[pltpu.VMEM((tm, d), BF16), pltpu.VMEM((tm, d), F32)],
        compiler_params=_cparams(("arbitrary", "arbitrary")),
        name="ffn",
    )(x, nw, sh, sc, g, w_in, w_in, w_out, fnw, fsh, fsc)


def _gdn_gates_kernel(s_ref, alog_ref, dt_ref, gt_ref, gh_ref):
    x = s_ref[...]
    tm = x.shape[0]
    lane = lax.broadcasted_iota(jnp.int32, x.shape, 1)
    beta = jax.nn.sigmoid(x)
    g = -jnp.exp(alog_ref[...]) * _softplus(x + dt_ref[...])
    g = jnp.where((lane >= GDN_V_HEADS) & (lane < 2 * GDN_V_HEADS), g, 0.0)
    r = lax.broadcasted_iota(jnp.int32, (tm, tm), 0)
    c = lax.broadcasted_iota(jnp.int32, (tm, tm), 1)
    same_chunk = (r // GDN_CHUNK) == (c // GDN_CHUNK)
    tril = jnp.where((r >= c) & same_chunk, 1.0, 0.0).astype(F32)
    gcum = jnp.dot(tril, g, preferred_element_type=F32, precision=lax.Precision.HIGHEST)
    gt = jnp.where(lane < GDN_V_HEADS, beta, gcum)
    gt_ref[...] = gt
    gh_ref[...] = gt.T


def gdn_gates(small, alog_row, dt_row, tm=512):
    l = small.shape[0]
    tm = min(tm, l)
    row = lambda i: (0, 0)
    return pl.pallas_call(
        _gdn_gates_kernel,
        grid=(l // tm,),
        in_specs=[pl.BlockSpec((tm, LANES), lambda i: (i, 0)),
                  pl.BlockSpec((1, LANES), row), pl.BlockSpec((1, LANES), row)],
        out_specs=[pl.BlockSpec((tm, LANES), lambda i: (i, 0)),
                   pl.BlockSpec((LANES, tm), lambda i: (0, i))],
        out_shape=[jax.ShapeDtypeStruct((l, LANES), F32), jax.ShapeDtypeStruct((LANES, l), F32)],
        compiler_params=_cparams(("arbitrary",)),
        name="gdn_gates",
    )(small, alog_row, dt_row)


GDN_PAIRS = 8


def _conv_silu(x_ref, xe_ref, w_ref):
    tb = x_ref.shape[0]
    pad = SUBLANES
    xe_ref[pad:pad + tb, :] = x_ref[...]
    w = w_ref[...]
    acc = w[CONV_K - 1:CONV_K, :] * xe_ref[pad:pad + tb, :]
    for i in range(CONV_K - 1):
        off = pad - (CONV_K - 1) + i
        acc = acc + w[i:i + 1, :] * xe_ref[off:off + tb, :]
    xe_ref[0:pad, :] = xe_ref[tb:tb + pad, :]
    return _silu(acc)


def _cast_passengers(in_refs, out_refs):
    for src, dst in zip(in_refs, out_refs):
        dst[...] = src[...].astype(dst.dtype)


def _cast_specs(arrays, layer, n_steps, step_of):
    in_specs, out_specs, out_shape = [], [], []
    bf16_rows = 16
    for a in arrays:
        rows = a.shape[0] // 2
        assert a.shape[0] % 2 == 0 and rows % bf16_rows == 0
        n_blk = max(n for n in range(1, n_steps + 1) if (rows // bf16_rows) % n == 0)
        blk = rows // n_blk
        blk_of = lambda *g, _n=n_blk: jnp.minimum(step_of(*g), _n - 1)
        in_specs.append(pl.BlockSpec((blk, a.shape[1]),
                                     lambda *g, _n=n_blk, _b=blk_of: (layer * _n + _b(*g), 0)))
        out_specs.append(pl.BlockSpec((blk, a.shape[1]), lambda *g, _b=blk_of: (_b(*g), 0)))
        out_shape.append(jax.ShapeDtypeStruct((rows, a.shape[1]), BF16))
    return in_specs, out_specs, out_shape


def _gdn_kernel(q_ref, k_ref, v_ref, z_ref, wq_ref, wk_ref, wv_ref, gt_ref, gh_ref, nw_ref,
                c0_ref, c1_ref, o_ref, co0_ref, co1_ref, xq_ref, xk_ref, xv_ref, s_ref, *, pairs):
    _cast_passengers((c0_ref, c1_ref), (co0_ref, co1_ref))
    gi = pl.program_id(0)
    tb = q_ref.shape[0]
    hd = GDN_HEAD_DIM
    ck = GDN_CHUNK
    nck = tb // ck
    n_sq = int(math.log2(ck)) - 1

    @pl.when(pl.program_id(1) == 0)
    def _():
        for xe_ref in (xq_ref, xk_ref, xv_ref):
            xe_ref[0:SUBLANES, :] = jnp.zeros((SUBLANES, xe_ref.shape[1]), F32)
        s_ref[...] = jnp.zeros_like(s_ref)

    q_all = _conv_silu(q_ref, xq_ref, wq_ref)
    k_all = _conv_silu(k_ref, xk_ref, wk_ref)
    v_all = _conv_silu(v_ref, xv_ref, wv_ref)

    head0 = 2 * pairs * gi
    gt = pltpu.roll(gt_ref[...], lax.rem(LANES - head0, LANES), axis=1)
    r_idx = lax.broadcasted_iota(jnp.int32, (tb, tb), 0)
    c_idx = lax.broadcasted_iota(jnp.int32, (tb, tb), 1)
    same = (r_idx // ck) == (c_idx // ck)
    causal = same & (r_idx >= c_idx)
    strict = same & (r_idx > c_idx)
    same_b = jnp.where(same, 1.0, 0.0).astype(BF16)
    nw = nw_ref[...]

    def block_diag(packed):
        return jnp.concatenate([packed.astype(BF16)] * nck, axis=0) * same_b

    heads = range(2 * pairs)
    qs, ks, kks, qks = [], [], [], []
    for pi in range(pairs):
        q = q_all[:, pi * hd:(pi + 1) * hd]
        k = k_all[:, pi * hd:(pi + 1) * hd]
        q = q * (lax.rsqrt(jnp.sum(q * q, axis=-1, keepdims=True) + NORM_EPS) * hd ** -0.5)
        k = k * lax.rsqrt(jnp.sum(k * k, axis=-1, keepdims=True) + NORM_EPS)
        kb = k.astype(BF16)
        qs.append(q)
        ks.append(k)
        kks.append(lax.dot_general(kb, kb, NT_DIMS, preferred_element_type=F32))
        qks.append(lax.dot_general(q.astype(BF16), kb, NT_DIMS, preferred_element_type=F32))

    bcs, gccs, attns, ps = [], [], [], []
    for h in heads:
        bc = gt[:, h:h + 1]
        gcc = gt[:, GDN_V_HEADS + h:GDN_V_HEADS + h + 1]
        gcr = gh_ref[pl.ds(GDN_V_HEADS + head0 + h, 1), :]
        decay = jnp.exp(jnp.where(causal, gcc - gcr, -jnp.inf))
        a_full = jnp.where(strict, kks[h // 2] * decay * bc, 0.0)
        attn = qks[h // 2] * decay
        attns.append([attn[c * ck:(c + 1) * ck, c * ck:(c + 1) * ck].astype(BF16)
                      for c in range(nck)])
        p = -a_full[0:ck]
        for c in range(1, nck):
            p = p - a_full[c * ck:(c + 1) * ck]
        bcs.append(bc)
        gccs.append(gcc)
        ps.append(p)

    rs = list(ps)
    p_bds = [block_diag(p) for p in ps]
    for _ in range(n_sq):
        ps = [jnp.dot(ps[h].astype(BF16), p_bds[h], preferred_element_type=F32) for h in heads]
        p_bds = [block_diag(p) for p in ps]
        rs = [rs[h] + ps[h] + jnp.dot(rs[h].astype(BF16), p_bds[h], preferred_element_type=F32)
              for h in heads]

    egs = [jnp.exp(gcc) for gcc in gccs]
    us, wqs = [], []
    for h in heads:
        k = ks[h // 2]
        rhs = bcs[h] * jnp.concatenate([v_all[:, h * hd:(h + 1) * hd], k * egs[h]], axis=1)
        uw = rhs + jnp.dot(block_diag(rs[h]), rhs.astype(BF16), preferred_element_type=F32)
        us.append(uw[:, :hd])
        q_dec = qs[h // 2] * egs[h]
        wqs.append([jnp.concatenate([uw[c * ck:(c + 1) * ck, hd:], q_dec[c * ck:(c + 1) * ck]],
                                    axis=0).astype(BF16) for c in range(nck)])

    states = [s_ref[h] for h in heads]
    outs = [[] for _ in heads]
    for ci in range(nck):
        rows = slice(ci * ck, (ci + 1) * ck)
        wq_s = [jnp.dot(wqs[h][ci], states[h].astype(BF16), preferred_element_type=F32)
                for h in heads]
        v_news = [(us[h][rows] - wq_s[h][:ck]).astype(BF16) for h in heads]
        for h in heads:
            outs[h].append(wq_s[h][ck:] + jnp.dot(attns[h][ci], v_news[h],
                                                  preferred_element_type=F32))
        for h in heads:
            g_last = gccs[h][(ci + 1) * ck - 1:(ci + 1) * ck, :]
            k_dec = (ks[h // 2][rows] * jnp.exp(g_last - gccs[h][rows])).astype(BF16)
            states[h] = states[h] * jnp.exp(g_last) + lax.dot_general(
                k_dec, v_news[h], TN_DIMS, preferred_element_type=F32)

    for h in heads:
        s_ref[h] = states[h]
        o = jnp.concatenate(outs[h], axis=0)
        on = o * lax.rsqrt(jnp.mean(o * o, axis=-1, keepdims=True) + NORM_EPS) * nw
        cols = slice(h * hd, (h + 1) * hd)
        o_ref[:, cols] = (on * _silu(z_ref[:, cols])).astype(o_ref.dtype)


def gdn_core(proj, conv_w, gt, gh, norm_w, cast_srcs, cast_layer, tb=256, pairs=GDN_PAIRS):
    l = proj.shape[0]
    hd = GDN_HEAD_DIM
    tb = min(tb, l)
    ng = GDN_QK_HEADS // pairs
    nb = l // tb
    qw, vw = pairs * hd, 2 * pairs * hd
    c_in, c_out, c_shape = _cast_specs(cast_srcs, cast_layer, ng * nb, lambda g, b: g * nb + b)
    return pl.pallas_call(
        functools.partial(_gdn_kernel, pairs=pairs),
        grid=(ng, nb),
        in_specs=[
            pl.BlockSpec((tb, qw), lambda g, b: (b, g)),
            pl.BlockSpec((tb, qw), lambda g, b: (b, ng + g)),
            pl.BlockSpec((tb, vw), lambda g, b: (b, ng + g)),
            pl.BlockSpec((tb, vw), lambda g, b: (b, 2 * ng + g)),
            pl.BlockSpec((CONV_K, qw), lambda g, b: (0, g)),
            pl.BlockSpec((CONV_K, qw), lambda g, b: (0, ng + g)),
            pl.BlockSpec((CONV_K, vw), lambda g, b: (0, ng + g)),
            pl.BlockSpec((tb, LANES), lambda g, b: (b, 0)),
            pl.BlockSpec((LANES, tb), lambda g, b: (0, b)),
            pl.BlockSpec((1, hd), lambda g, b: (0, 0)),
        ] + c_in,
        out_specs=[pl.BlockSpec((tb, vw), lambda g, b: (b, g))] + c_out,
        out_shape=[jax.ShapeDtypeStruct((l, GDN_V_HEADS * hd), BF16)] + c_shape,
        scratch_shapes=[pltpu.VMEM((tb + SUBLANES, qw), F32),
                        pltpu.VMEM((tb + SUBLANES, qw), F32),
                        pltpu.VMEM((tb + SUBLANES, vw), F32),
                        pltpu.VMEM((2 * pairs, hd, hd), F32)],
        compiler_params=_cparams(("arbitrary", "arbitrary")),
        name="gdn_core",
    )(proj, proj, proj, proj, conv_w, conv_w, conv_w, gt, gh, norm_w, *cast_srcs)


def _kv_prep_kernel(kv_ref, fl_ref, knw_ref, fb_ref, kt_ref, v_ref, fh_ref, carry_ref):
    hd = FOX_HEAD_DIM
    nkv = FOX_KV_HEADS
    tm = kv_ref.shape[0]

    @pl.when(pl.program_id(0) == 0)
    def _():
        carry_ref[...] = jnp.zeros_like(carry_ref)

    for h in range(nkv):
        kh = kv_ref[:, h * hd:(h + 1) * hd]
        ms = jnp.mean(kh * kh, axis=-1, keepdims=True)
        kn = kh * lax.rsqrt(ms + NORM_EPS) * knw_ref[...]
        kt_ref[h * hd:(h + 1) * hd, :] = kn.T.astype(BF16)
    v_ref[...] = kv_ref[:, nkv * hd:2 * nkv * hd].astype(BF16)

    log_f = _log_sigmoid(fl_ref[...] + fb_ref[...])
    r = lax.broadcasted_iota(jnp.int32, (tm, tm), 0)
    c = lax.broadcasted_iota(jnp.int32, (tm, tm), 1)
    tril = jnp.where(r >= c, 1.0, 0.0).astype(F32)
    cs = jnp.dot(tril, log_f, preferred_element_type=F32,
                 precision=lax.Precision.HIGHEST) + carry_ref[...]
    carry_ref[...] = cs[tm - 1:tm, :]
    fh_ref[...] = (cs * LOG2E).T[0:fh_ref.shape[0], :]


def kv_prep(kv, fl, knw, fb_row, tm=512):
    l = kv.shape[0]
    tm = min(tm, l)
    kvd = FOX_KV_HEADS * FOX_HEAD_DIM
    nh = FOX_KV_HEADS * FOX_GROUP
    row = lambda i: (0, 0)
    return pl.pallas_call(
        _kv_prep_kernel,
        grid=(l // tm,),
        in_specs=[pl.BlockSpec((tm, 2 * kvd), lambda i: (i, 0)),
                  pl.BlockSpec((tm, LANES), lambda i: (i, 0)),
                  pl.BlockSpec((1, FOX_HEAD_DIM), row), pl.BlockSpec((1, LANES), row)],
        out_specs=[pl.BlockSpec((kvd, tm), lambda i: (0, i)),
                   pl.BlockSpec((tm, kvd), lambda i: (i, 0)),
                   pl.BlockSpec((nh, tm), lambda i: (0, i))],
        out_shape=[jax.ShapeDtypeStruct((kvd, l), BF16), jax.ShapeDtypeStruct((l, kvd), BF16),
                   jax.ShapeDtypeStruct((nh, l), F32)],
        scratch_shapes=[pltpu.VMEM((1, LANES), F32)],
        compiler_params=_cparams(("arbitrary",)),
        name="kv_prep",
    )(kv, fl, knw, fb_row)


def _fox_kernel(q_ref, gate_ref, kt_ref, v_ref, fh_ref, qnw_ref, c0_ref, c1_ref, o_ref, co0_ref,
                co1_ref, qn_ref, m_ref, l_ref, acc_ref, s_ref, p_ref, *, tk):
    _cast_passengers((c0_ref, c1_ref), (co0_ref, co1_ref))
    i = pl.program_id(1)
    tq = q_ref.shape[0]
    hd = FOX_HEAD_DIM
    scale = hd ** -0.5 * LOG2E
    for g in range(FOX_GROUP):
        qh = q_ref[:, g * hd:(g + 1) * hd]
        ms = jnp.mean(qh * qh, axis=-1, keepdims=True)
        qn_ref[g] = (qh * (lax.rsqrt(ms + NORM_EPS) * scale) * qnw_ref[...]).astype(BF16)
    m_ref[...] = jnp.full(m_ref.shape, -jnp.inf, F32)
    l_ref[...] = jnp.zeros_like(l_ref)
    acc_ref[...] = jnp.zeros_like(acc_ref)
    n_full = (i * tq + 1) // tk
    key_minus_query = (lax.broadcasted_iota(jnp.int32, (tq, tk), 1)
                       - lax.broadcasted_iota(jnp.int32, (tq, tk), 0))
    causal = key_minus_query <= i * tq - n_full * tk

    def tile(j, masked):
        start = pl.multiple_of(j * tk, tk)
        kj = kt_ref[:, pl.ds(start, tk)]
        vj = v_ref[pl.ds(start, tk), :]
        heads = range(FOX_GROUP)
        lane_tiles = [slice(c * LANES, (c + 1) * LANES) for c in range(tk // LANES)]
        for g in heads:
            s = (jnp.dot(qn_ref[g], kj, preferred_element_type=F32)
                 - fh_ref[g:g + 1, pl.ds(start, tk)])
            s_ref[g] = jnp.where(causal, s, -jnp.inf) if masked else s
        half = FOX_GROUP // 2
        for sub in (range(0, half), range(half, FOX_GROUP)):
            m_olds = {g: m_ref[g] for g in sub}
            m_news = {g: jnp.maximum(m_olds[g], jnp.max(
                functools.reduce(jnp.maximum, [s_ref[g, :, c] for c in lane_tiles]),
                axis=-1, keepdims=True)) for g in sub}
            alphas = {g: jnp.exp2(m_olds[g] - m_news[g]) for g in sub}
            for g in sub:
                row_sum = jnp.zeros((tq, LANES), F32)
                for c in lane_tiles:
                    pc = jnp.exp2(s_ref[g, :, c] - m_news[g])
                    row_sum = row_sum + pc
                    p_ref[g, :, c] = pc.astype(BF16)
                l_ref[g] = alphas[g] * l_ref[g] + jnp.sum(row_sum, axis=-1, keepdims=True)
                m_ref[g] = m_news[g]
            pvs = {g: jnp.dot(p_ref[g], vj, preferred_element_type=F32) for g in sub}
            for g in sub:
                for c in range(hd // LANES):
                    cols = slice(c * LANES, (c + 1) * LANES)
                    acc_ref[g, :, cols] = alphas[g] * acc_ref[g, :, cols] + pvs[g][:, cols]

    def body(j, carry):
        tile(j, False)
        return carry

    lax.fori_loop(0, n_full, body, 0)
    tile(n_full, True)
    for g in range(FOX_GROUP):
        inv_l = 1.0 / l_ref[g]
        for c in range(hd // LANES):
            cols = slice(g * hd + c * LANES, g * hd + (c + 1) * LANES)
            o_ref[:, cols] = (acc_ref[g, :, c * LANES:(c + 1) * LANES] * inv_l
                              * jax.nn.sigmoid(gate_ref[:, cols])).astype(o_ref.dtype)


def fox_attention(qg, kt, v, fh, qnw, cast_srcs, cast_layer, tq=256, tk=1024):
    l = qg.shape[0]
    hd = FOX_HEAD_DIM
    gw = FOX_GROUP * hd
    tk = min(tk, l)
    tq = min(tq, tk)
    assert tk % tq == 0 and l % tk == 0
    nq = l // tq
    c_in, c_out, c_shape = _cast_specs(cast_srcs, cast_layer, FOX_KV_HEADS * nq,
                                       lambda h, i: h * nq + i)
    return pl.pallas_call(
        functools.partial(_fox_kernel, tk=tk),
        grid=(FOX_KV_HEADS, l // tq),
        in_specs=[
            pl.BlockSpec((tq, gw), lambda h, i: (i, h)),
            pl.BlockSpec((tq, gw), lambda h, i: (i, FOX_KV_HEADS + h)),
            pl.BlockSpec((hd, l), lambda h, i: (h, 0), pipeline_mode=pl.Buffered(1)),
            pl.BlockSpec((l, hd), lambda h, i: (0, h), pipeline_mode=pl.Buffered(1)),
            pl.BlockSpec((FOX_GROUP, l), lambda h, i: (h, 0)),
            pl.BlockSpec((1, hd), lambda h, i: (0, 0)),
        ] + c_in,
        out_specs=[pl.BlockSpec((tq, gw), lambda h, i: (i, h))] + c_out,
        out_shape=[jax.ShapeDtypeStruct((l, FOX_KV_HEADS * gw), BF16)] + c_shape,
        scratch_shapes=[pltpu.VMEM((FOX_GROUP, tq, hd), BF16),
                        pltpu.VMEM((FOX_GROUP, tq, LANES), F32),
                        pltpu.VMEM((FOX_GROUP, tq, LANES), F32),
                        pltpu.VMEM((FOX_GROUP, tq, hd), F32),
                        pltpu.VMEM((FOX_GROUP, tq, tk), F32),
                        pltpu.VMEM((FOX_GROUP, tq, tk), BF16)],
        compiler_params=_cparams(("arbitrary", "arbitrary")),
        name="fox_attention",
    )(qg, qg, kt, v, fh, qnw, *cast_srcs)


def kernel(x, c, ada_w, ada_b, norm_mix, norm_ffn, ffn_w_in, ffn_w_out, gdn_w_in, gdn_conv,
           gdn_a_log, gdn_dt_bias, gdn_norm, gdn_w_out, kv_ada_w, kv_ada_b, kv_norm, kv_w, k_norm,
           forget_b, fox_w_in, q_norm, fox_w_out, out_ada_w, out_ada_b, out_norm):
    bsz, l, d = x.shape
    assert bsz == 1 and ada_w.shape[0] == 2 and gdn_w_in.shape[0] == 1 and fox_w_in.shape[0] == 1
    xs = x.reshape(l, d)
    c_col = c.reshape(d, 1)
    row = lambda t: t.reshape(1, -1)

    def mods(w3, b3, layer, n):
        m = adaln(c_col, w3, b3, layer)
        return [m[:, i * d:(i + 1) * d] for i in range(n)]

    ada_b3 = ada_b[:, None, :]
    conv_dim = gdn_conv.shape[2]
    gdn_main = conv_dim + GDN_V_HEADS * GDN_HEAD_DIM
    kvd = FOX_KV_HEADS * FOX_HEAD_DIM

    sh_m, sc_m, g_m, sh_f, sc_f, g_f = mods(ada_w, ada_b3, 0, 6)
    ffn_srcs = (ffn_w_in.reshape(-1, ffn_w_in.shape[2]), ffn_w_out.reshape(-1, d))
    ffn_ws = lambda w_in_b, w_out_b: (w_in_b[None], w_out_b[None])
    proj, small = nm_proj_t(xs, row(norm_mix[0]), sh_m, sc_m, jnp.swapaxes(gdn_w_in, 1, 2), 0, 0,
                            gdn_main, gdn_w_in.shape[2] - gdn_main)
    pad32 = lambda t: jnp.pad(t, (GDN_V_HEADS, LANES - 2 * GDN_V_HEADS)).reshape(1, LANES)
    gt, gh = gdn_gates(small, pad32(gdn_a_log[0]), pad32(gdn_dt_bias[0]))
    o, w_in_b, w_out_b = gdn_core(proj, gdn_conv[0], gt, gh, row(gdn_norm[0]), ffn_srcs, 0)
    xs = proj_res(o, gdn_w_out, 0, xs, g_m)
    one = row(out_norm)
    xs = ffn(xs, row(norm_ffn[0]), sh_f, sc_f, g_f, *ffn_ws(w_in_b, w_out_b), 0, one, one, one,
             final=False)

    sh_k, sc_k = mods(kv_ada_w[None], kv_ada_b[None, None, :], 0, 2)
    kvp, fl = nm_proj_t(xs, row(kv_norm), sh_k, sc_k, kv_w.T[None], 0, 0, 2 * kvd,
                        kv_w.shape[1] - 2 * kvd)
    fb_row = jnp.pad(forget_b, (0, LANES - forget_b.shape[0])).reshape(1, LANES)
    k_sh, v_sh, fh = kv_prep(kvp, fl, row(k_norm), fb_row)

    sh_m, sc_m, g_m, sh_f, sc_f, g_f = mods(ada_w, ada_b3, 1, 6)
    qg = nm_proj(xs, row(norm_mix[1]), sh_m, sc_m, fox_w_in, 0)
    a, w_in_b, w_out_b = fox_attention(qg, k_sh, v_sh, fh, row(q_norm[0]), ffn_srcs, 1)
    xs = proj_res(a, fox_w_out, 0, xs, g_m)
    sh_o, sc_o = mods(out_ada_w[None], out_ada_b[None, None, :], 0, 2)
    xs = ffn(xs, row(norm_ffn[1]), sh_f, sc_f, g_f, *ffn_ws(w_in_b, w_out_b), 0, row(out_norm),
             sh_o, sc_o, final=True)
    return xs.reshape(bsz, l, d)
```

```python
import functools
import math

import jax
import jax.numpy as jnp
from jax import lax
from jax.experimental import pallas as pl
from jax.experimental.pallas import tpu as pltpu

F32 = jnp.float32
BF16 = jnp.bfloat16
NORM_EPS = 1e-6
LOG2E = math.log2(math.e)

V7X_VMEM_BYTES = 64 * 1024 * 1024
VMEM_LIMIT_BYTES = 56 * 1024 * 1024
LANES = 128
SUBLANES = 8

GDN_CHUNK = 64
GDN_HEAD_DIM = 128
GDN_V_HEADS = 32
GDN_QK_HEADS = 16
CONV_K = 4
FOX_HEAD_DIM = 256
FOX_KV_HEADS = 2
FOX_GROUP = 8

NT_DIMS = (((1,), (1,)), ((), ()))
TN_DIMS = (((0,), (0,)), ((), ()))


def _cparams(sem):
    return pltpu.CompilerParams(dimension_semantics=sem, vmem_limit_bytes=VMEM_LIMIT_BYTES)


def _silu(x):
    return x * jax.nn.sigmoid(x)


def _softplus(x):
    return jnp.maximum(x, 0.0) + jnp.log1p(jnp.exp(-jnp.abs(x)))


def _log_sigmoid(x):
    return jnp.minimum(x, 0.0) - jnp.log1p(jnp.exp(-jnp.abs(x)))


def _norm_mod(x, nw, sh, sc):
    ms = jnp.mean(x * x, axis=-1, keepdims=True)
    y = x * lax.rsqrt(ms + NORM_EPS) * nw
    return y * (1.0 + sc) + sh


NORM_ROWS = 256


def _norm_mod_rows(dst_ref, src_fn, nw, sh, sc):
    tm = dst_ref.shape[0]
    step = min(NORM_ROWS, tm)
    for r in range(0, tm, step):
        rows = slice(r, r + step)
        dst_ref[rows, :] = _norm_mod(src_fn(rows), nw, sh, sc).astype(dst_ref.dtype)


def _adaln_kernel(c_ref, w_ref, b_ref, o_ref):
    cond = _silu(c_ref[...])
    o_ref[...] = jnp.sum(w_ref[...] * cond, axis=0, keepdims=True) + b_ref[...]


def adaln(c_col, w3, b3, layer, tn=1024):
    _, d, n = w3.shape
    tn = min(tn, n)
    return pl.pallas_call(
        _adaln_kernel,
        grid=(n // tn,),
        in_specs=[
            pl.BlockSpec((d, 1), lambda j: (0, 0)),
            pl.BlockSpec((None, d, tn), lambda j: (layer, 0, j)),
            pl.BlockSpec((None, 1, tn), lambda j: (layer, 0, j)),
        ],
        out_specs=pl.BlockSpec((1, tn), lambda j: (0, j)),
        out_shape=jax.ShapeDtypeStruct((1, n), F32),
        compiler_params=_cparams(("arbitrary",)),
        name="adaln",
    )(c_col, w3, b3)


def _nm_proj_kernel(x_ref, nw_ref, sh_ref, sc_ref, w_ref, c_ref, o_ref, co_ref, h_ref):
    _cast_passengers((c_ref,), (co_ref,))

    @pl.when(pl.program_id(1) == 0)
    def _():
        _norm_mod_rows(h_ref, lambda rows: x_ref[rows, :], nw_ref[...], sh_ref[...], sc_ref[...])

    o_ref[...] = jnp.dot(h_ref[...], w_ref[...].astype(BF16), preferred_element_type=F32)


def nm_proj(x, nw, sh, sc, w3, layer, cast_src, tm=1024, tn=1024):
    l, d = x.shape
    n = w3.shape[2]
    tm, tn = min(tm, l), min(tn, n)
    nj = n // tn
    row = lambda i, j: (0, 0)
    c_in, c_out, c_shape = _cast_specs([cast_src], 0, (l // tm) * nj, lambda i, j: i * nj + j, 1)
    return pl.pallas_call(
        _nm_proj_kernel,
        grid=(l // tm, nj),
        in_specs=[
            pl.BlockSpec((tm, d), lambda i, j: (i, 0)),
            pl.BlockSpec((1, d), row), pl.BlockSpec((1, d), row), pl.BlockSpec((1, d), row),
            pl.BlockSpec((None, d, tn), lambda i, j: (layer, 0, j)),
        ] + c_in,
        out_specs=[pl.BlockSpec((tm, tn), lambda i, j: (i, j))] + c_out,
        out_shape=[jax.ShapeDtypeStruct((l, n), F32)] + c_shape,
        scratch_shapes=[pltpu.VMEM((tm, d), BF16)],
        compiler_params=_cparams(("arbitrary", "arbitrary")),
        name="nm_proj",
    )(x, nw, sh, sc, w3, cast_src)


def _nm_proj_t_kernel(x_ref, nw_ref, sh_ref, sc_ref, wt_ref, wst_ref, *rest, cast):
    if cast:
        c_ref, o_ref, os_ref, co_ref, h_ref = rest
        _cast_passengers((c_ref,), (co_ref,))
    else:
        o_ref, os_ref, h_ref = rest
    tm = o_ref.shape[0]

    @pl.when(pl.program_id(1) == 0)
    def _():
        _norm_mod_rows(h_ref, lambda rows: x_ref[rows, :], nw_ref[...], sh_ref[...], sc_ref[...])
        ns = wst_ref.shape[0]
        os_ref[:, 0:ns] = lax.dot_general(h_ref[...], wst_ref[...].astype(BF16), NT_DIMS,
                                          preferred_element_type=F32)
        os_ref[:, ns:] = jnp.zeros((tm, LANES - ns), F32)

    o_ref[...] = lax.dot_general(h_ref[...], wt_ref[...].astype(BF16), NT_DIMS,
                                 preferred_element_type=F32)


def nm_proj_t(x, nw, sh, sc, wt3, layer, n, ns, cast_src=None, tm=1024, tn=1024):
    l, d = x.shape
    tm, tn = min(tm, l), min(tn, n)
    assert n % tn == 0 and n % ns == 0
    nj = n // tn
    cast = cast_src is not None
    row = lambda i, j: (0, 0)
    in_specs = [
        pl.BlockSpec((tm, d), lambda i, j: (i, 0)),
        pl.BlockSpec((1, d), row), pl.BlockSpec((1, d), row), pl.BlockSpec((1, d), row),
        pl.BlockSpec((None, tn, d), lambda i, j: (layer, j, 0)),
        pl.BlockSpec((None, ns, d), lambda i, j: (layer, n // ns, 0)),
    ]
    out_specs = [pl.BlockSpec((tm, tn), lambda i, j: (i, j)),
                 pl.BlockSpec((tm, LANES), lambda i, j: (i, 0))]
    out_shape = [jax.ShapeDtypeStruct((l, n), F32), jax.ShapeDtypeStruct((l, LANES), F32)]
    args = [x, nw, sh, sc, wt3, wt3]
    if cast:
        c_in, c_out, c_shape = _cast_specs([cast_src], 0, (l // tm) * nj,
                                           lambda i, j: i * nj + j, 1)
        in_specs, out_specs, out_shape = in_specs + c_in, out_specs + c_out, out_shape + c_shape
        args.append(cast_src)
    return pl.pallas_call(
        functools.partial(_nm_proj_t_kernel, cast=cast),
        grid=(l // tm, nj),
        in_specs=in_specs,
        out_specs=out_specs,
        out_shape=out_shape,
        scratch_shapes=[pltpu.VMEM((tm, d), BF16)],
        compiler_params=_cparams(("arbitrary", "arbitrary")),
        name="nm_proj_t",
    )(*args)


def _proj_res_kernel(a_ref, w_ref, x_ref, g_ref, o_ref):
    y = jnp.dot(a_ref[...], w_ref[...], preferred_element_type=F32)
    o_ref[...] = x_ref[...] + g_ref[...] * y


def proj_res(a, w3, layer, x, g, tm=1024, tn=1024):
    l, k = a.shape
    d = w3.shape[2]
    tm, tn = min(tm, l), min(tn, d)
    return pl.pallas_call(
        _proj_res_kernel,
        grid=(l // tm, d // tn),
        in_specs=[
            pl.BlockSpec((tm, k), lambda i, j: (i, 0)),
            pl.BlockSpec((None, k, tn), lambda i, j: (layer, 0, j)),
            pl.BlockSpec((tm, tn), lambda i, j: (i, j)),
            pl.BlockSpec((1, tn), lambda i, j: (0, j)),
        ],
        out_specs=pl.BlockSpec((tm, tn), lambda i, j: (i, j)),
        out_shape=jax.ShapeDtypeStruct((l, d), F32),
        compiler_params=_cparams(("arbitrary", "arbitrary")),
        name="proj_res",
    )(a, w3, x, g)


def _ffn_kernel(x_ref, nw_ref, sh_ref, sc_ref, g_ref, wg_ref, wu_ref, wo_ref, fnw_ref, fsh_ref,
                fsc_ref, o_ref, h_ref, acc_ref, *, final):
    j = pl.program_id(1)

    @pl.when(j == 0)
    def _():
        _norm_mod_rows(h_ref, lambda rows: x_ref[rows, :], nw_ref[...], sh_ref[...], sc_ref[...])
        acc_ref[...] = jnp.zeros_like(acc_ref)

    h = h_ref[...]
    gate = jnp.dot(h, wg_ref[...], preferred_element_type=F32)
    up = jnp.dot(h, wu_ref[...], preferred_element_type=F32)
    act = (_silu(gate) * up).astype(BF16)
    acc_ref[...] += jnp.dot(act, wo_ref[...], preferred_element_type=F32)

    @pl.when(j == pl.num_programs(1) - 1)
    def _():
        res = lambda rows: x_ref[rows, :] + g_ref[...] * acc_ref[rows, :]
        if final:
            _norm_mod_rows(o_ref, res, fnw_ref[...], fsh_ref[...], fsc_ref[...])
        else:
            o_ref[...] = res(slice(None))


def ffn(x, nw, sh, sc, g, w_in, w_out, layer, fnw, fsh, fsc, final, tm=512, th=512):
    l, d = x.shape
    hdim = w_out.shape[1]
    tm, th = min(tm, l), min(th, hdim)
    nh = hdim // th
    row = lambda i, j: (0, 0)
    vec = pl.BlockSpec((1, d), row)
    return pl.pallas_call(
        functools.partial(_ffn_kernel, final=final),
        grid=(l // tm, nh),
        in_specs=[
            pl.BlockSpec((tm, d), lambda i, j: (i, 0)),
            vec, vec, vec, vec,
            pl.BlockSpec((None, d, th), lambda i, j: (layer, 0, j)),
            pl.BlockSpec((None, d, th), lambda i, j: (layer, 0, j + nh)),
            pl.BlockSpec((None, th, d), lambda i, j: (layer, j, 0)),
            vec, vec, vec,
        ],
        out_specs=pl.BlockSpec((tm, d), lambda i, j: (i, 0)),
        out_shape=jax.ShapeDtypeStruct((l, d), F32),
        scratch_shapes=[pltpu.VMEM((tm, d), BF16), pltpu.VMEM((tm, d), F32)],
        compiler_params=_cparams(("arbitrary", "arbitrary")),
        name="ffn",
    )(x, nw, sh, sc, g, w_in, w_in, w_out, fnw, fsh, fsc)


def _gdn_gates_kernel(s_ref, alog_ref, dt_ref, gt_ref, gh_ref):
    x = s_ref[...]
    tm = x.shape[0]
    lane = lax.broadcasted_iota(jnp.int32, x.shape, 1)
    beta = jax.nn.sigmoid(x)
    g = -jnp.exp(alog_ref[...]) * _softplus(x + dt_ref[...])
    g = jnp.where((lane >= GDN_V_HEADS) & (lane < 2 * GDN_V_HEADS), g, 0.0)
    r = lax.broadcasted_iota(jnp.int32, (tm, tm), 0)
    c = lax.broadcasted_iota(jnp.int32, (tm, tm), 1)
    same_chunk = (r // GDN_CHUNK) == (c // GDN_CHUNK)
    tril = jnp.where((r >= c) & same_chunk, 1.0, 0.0).astype(F32)
    gcum = jnp.dot(tril, g, preferred_element_type=F32, precision=lax.Precision.HIGHEST)
    gt = jnp.where(lane < GDN_V_HEADS, beta, gcum)
    gt_ref[...] = gt
    gh_ref[...] = gt.T


def gdn_gates(small, alog_row, dt_row, tm=512):
    l = small.shape[0]
    tm = min(tm, l)
    row = lambda i: (0, 0)
    return pl.pallas_call(
        _gdn_gates_kernel,
        grid=(l // tm,),
        in_specs=[pl.BlockSpec((tm, LANES), lambda i: (i, 0)),
                  pl.BlockSpec((1, LANES), row), pl.BlockSpec((1, LANES), row)],
        out_specs=[pl.BlockSpec((tm, LANES), lambda i: (i, 0)),
                   pl.BlockSpec((LANES, tm), lambda i: (0, i))],
        out_shape=[jax.ShapeDtypeStruct((l, LANES), F32), jax.ShapeDtypeStruct((LANES, l), F32)],
        compiler_params=_cparams(("arbitrary",)),
        name="gdn_gates",
    )(small, alog_row, dt_row)


GDN_PAIRS = 8


def _conv_silu(x_ref, xe_ref, w_ref):
    tb = x_ref.shape[0]
    pad = SUBLANES
    xe_ref[pad:pad + tb, :] = x_ref[...]
    w = w_ref[...]
    acc = w[CONV_K - 1:CONV_K, :] * xe_ref[pad:pad + tb, :]
    for i in range(CONV_K - 1):
        off = pad - (CONV_K - 1) + i
        acc = acc + w[i:i + 1, :] * xe_ref[off:off + tb, :]
    xe_ref[0:pad, :] = xe_ref[tb:tb + pad, :]
    return _silu(acc)


def _cast_passengers(in_refs, out_refs):
    for src, dst in zip(in_refs, out_refs):
        dst[...] = src[...].astype(dst.dtype)


def _cast_specs(arrays, layer, n_steps, step_of, n_layers=2):
    in_specs, out_specs, out_shape = [], [], []
    bf16_rows = 16
    for a in arrays:
        rows = a.shape[0] // n_layers
        assert a.shape[0] % n_layers == 0 and rows % bf16_rows == 0
        n_blk = max(n for n in range(1, n_steps + 1) if (rows // bf16_rows) % n == 0)
        blk = rows // n_blk
        blk_of = lambda *g, _n=n_blk: jnp.minimum(step_of(*g), _n - 1)
        in_specs.append(pl.BlockSpec((blk, a.shape[1]),
                                     lambda *g, _n=n_blk, _b=blk_of: (layer * _n + _b(*g), 0)))
        out_specs.append(pl.BlockSpec((blk, a.shape[1]), lambda *g, _b=blk_of: (_b(*g), 0)))
        out_shape.append(jax.ShapeDtypeStruct((rows, a.shape[1]), BF16))
    return in_specs, out_specs, out_shape


def _gdn_kernel(q_ref, k_ref, v_ref, z_ref, wq_ref, wk_ref, wv_ref, gt_ref, gh_ref, nw_ref,
                c0_ref, c1_ref, o_ref, co0_ref, co1_ref, xq_ref, xk_ref, xv_ref, s_ref, *, pairs):
    _cast_passengers((c0_ref, c1_ref), (co0_ref, co1_ref))
    gi = pl.program_id(0)
    tb = q_ref.shape[0]
    hd = GDN_HEAD_DIM
    ck = GDN_CHUNK
    nck = tb // ck
    n_sq = int(math.log2(ck)) - 1

    @pl.when(pl.program_id(1) == 0)
    def _():
        for xe_ref in (xq_ref, xk_ref, xv_ref):
            xe_ref[0:SUBLANES, :] = jnp.zeros((SUBLANES, xe_ref.shape[1]), F32)
        s_ref[...] = jnp.zeros_like(s_ref)

    q_all = _conv_silu(q_ref, xq_ref, wq_ref)
    k_all = _conv_silu(k_ref, xk_ref, wk_ref)
    v_all = _conv_silu(v_ref, xv_ref, wv_ref)

    head0 = 2 * pairs * gi
    gt = pltpu.roll(gt_ref[...], lax.rem(LANES - head0, LANES), axis=1)
    r_idx = lax.broadcasted_iota(jnp.int32, (tb, tb), 0)
    c_idx = lax.broadcasted_iota(jnp.int32, (tb, tb), 1)
    same = (r_idx // ck) == (c_idx // ck)
    causal = same & (r_idx >= c_idx)
    strict = same & (r_idx > c_idx)
    same_b = jnp.where(same, 1.0, 0.0).astype(BF16)
    nw = nw_ref[...]

    def block_diag(packed):
        return jnp.concatenate([packed.astype(BF16)] * nck, axis=0) * same_b

    heads = range(2 * pairs)
    qs, ks, kks, qks = [], [], [], []
    for pi in range(pairs):
        q = q_all[:, pi * hd:(pi + 1) * hd]
        k = k_all[:, pi * hd:(pi + 1) * hd]
        q = q * (lax.rsqrt(jnp.sum(q * q, axis=-1, keepdims=True) + NORM_EPS) * hd ** -0.5)
        k = k * lax.rsqrt(jnp.sum(k * k, axis=-1, keepdims=True) + NORM_EPS)
        kb = k.astype(BF16)
        qs.append(q)
        ks.append(k)
        kks.append(lax.dot_general(kb, kb, NT_DIMS, preferred_element_type=F32))
        qks.append(lax.dot_general(q.astype(BF16), kb, NT_DIMS, preferred_element_type=F32))

    bcs, gccs, attns, ps = [], [], [], []
    for h in heads:
        bc = gt[:, h:h + 1]
        gcc = gt[:, GDN_V_HEADS + h:GDN_V_HEADS + h + 1]
        gcr = gh_ref[pl.ds(GDN_V_HEADS + head0 + h, 1), :]
        decay = jnp.exp(jnp.where(causal, gcc - gcr, -jnp.inf))
        a_full = jnp.where(strict, kks[h // 2] * decay * bc, 0.0)
        attn = qks[h // 2] * decay
        attns.append([attn[c * ck:(c + 1) * ck, c * ck:(c + 1) * ck].astype(BF16)
                      for c in range(nck)])
        p = -a_full[0:ck]
        for c in range(1, nck):
            p = p - a_full[c * ck:(c + 1) * ck]
        bcs.append(bc)
        gccs.append(gcc)
        ps.append(p)

    rs = list(ps)
    p_bds = [block_diag(p) for p in ps]
    for _ in range(n_sq):
        ps = [jnp.dot(ps[h].astype(BF16), p_bds[h], preferred_element_type=F32) for h in heads]
        p_bds = [block_diag(p) for p in ps]
        rs = [rs[h] + ps[h] + jnp.dot(rs[h].astype(BF16), p_bds[h], preferred_element_type=F32)
              for h in heads]

    egs = [jnp.exp(gcc) for gcc in gccs]
    us, wqs = [], []
    for h in heads:
        k = ks[h // 2]
        rhs = bcs[h] * jnp.concatenate([v_all[:, h * hd:(h + 1) * hd], k * egs[h]], axis=1)
        uw = rhs + jnp.dot(block_diag(rs[h]), rhs.astype(BF16), preferred_element_type=F32)
        us.append(uw[:, :hd])
        q_dec = qs[h // 2] * egs[h]
        wqs.append([jnp.concatenate([uw[c * ck:(c + 1) * ck, hd:], q_dec[c * ck:(c + 1) * ck]],
                                    axis=0).astype(BF16) for c in range(nck)])

    states = [s_ref[h] for h in heads]
    outs = [[] for _ in heads]
    for ci in range(nck):
        rows = slice(ci * ck, (ci + 1) * ck)
        wq_s = [jnp.dot(wqs[h][ci], states[h].astype(BF16), preferred_element_type=F32)
                for h in heads]
        v_news = [(us[h][rows] - wq_s[h][:ck]).astype(BF16) for h in heads]
        for h in heads:
            outs[h].append(wq_s[h][ck:] + jnp.dot(attns[h][ci], v_news[h],
                                                  preferred_element_type=F32))
        for h in heads:
            g_last = gccs[h][(ci + 1) * ck - 1:(ci + 1) * ck, :]
            k_dec = (ks[h // 2][rows] * jnp.exp(g_last - gccs[h][rows])).astype(BF16)
            states[h] = states[h] * jnp.exp(g_last) + lax.dot_general(
                k_dec, v_news[h], TN_DIMS, preferred_element_type=F32)

    for h in heads:
        s_ref[h] = states[h]
        o = jnp.concatenate(outs[h], axis=0)
        on = o * lax.rsqrt(jnp.mean(o * o, axis=-1, keepdims=True) + NORM_EPS) * nw
        cols = slice(h * hd, (h + 1) * hd)
        o_ref[:, cols] = (on * _silu(z_ref[:, cols])).astype(o_ref.dtype)


def gdn_core(proj, conv_w, gt, gh, norm_w, cast_srcs, cast_layer, tb=256, pairs=GDN_PAIRS):
    l = proj.shape[0]
    hd = GDN_HEAD_DIM
    tb = min(tb, l)
    ng = GDN_QK_HEADS // pairs
    nb = l // tb
    qw, vw = pairs * hd, 2 * pairs * hd
    c_in, c_out, c_shape = _cast_specs(cast_srcs, cast_layer, ng * nb, lambda g, b: g * nb + b)
    return pl.pallas_call(
        functools.partial(_gdn_kernel, pairs=pairs),
        grid=(ng, nb),
        in_specs=[
            pl.BlockSpec((tb, qw), lambda g, b: (b, g)),
            pl.BlockSpec((tb, qw), lambda g, b: (b, ng + g)),
            pl.BlockSpec((tb, vw), lambda g, b: (b, ng + g)),
            pl.BlockSpec((tb, vw), lambda g, b: (b, 2 * ng + g)),
            pl.BlockSpec((CONV_K, qw), lambda g, b: (0, g)),
            pl.BlockSpec((CONV_K, qw), lambda g, b: (0, ng + g)),
            pl.BlockSpec((CONV_K, vw), lambda g, b: (0, ng + g)),
            pl.BlockSpec((tb, LANES), lambda g, b: (b, 0)),
            pl.BlockSpec((LANES, tb), lambda g, b: (0, b)),
            pl.BlockSpec((1, hd), lambda g, b: (0, 0)),
        ] + c_in,
        out_specs=[pl.BlockSpec((tb, vw), lambda g, b: (b, g))] + c_out,
        out_shape=[jax.ShapeDtypeStruct((l, GDN_V_HEADS * hd), BF16)] + c_shape,
        scratch_shapes=[pltpu.VMEM((tb + SUBLANES, qw), F32),
                        pltpu.VMEM((tb + SUBLANES, qw), F32),
                        pltpu.VMEM((tb + SUBLANES, vw), F32),
                        pltpu.VMEM((2 * pairs, hd, hd), F32)],
        compiler_params=_cparams(("arbitrary", "arbitrary")),
        name="gdn_core",
    )(proj, proj, proj, proj, conv_w, conv_w, conv_w, gt, gh, norm_w, *cast_srcs)


def _kv_prep_kernel(kv_ref, fl_ref, knw_ref, fb_ref, kt_ref, v_ref, fh_ref, carry_ref):
    hd = FOX_HEAD_DIM
    nkv = FOX_KV_HEADS
    tm = kv_ref.shape[0]

    @pl.when(pl.program_id(0) == 0)
    def _():
        carry_ref[...] = jnp.zeros_like(carry_ref)

    for h in range(nkv):
        kh = kv_ref[:, h * hd:(h + 1) * hd]
        ms = jnp.mean(kh * kh, axis=-1, keepdims=True)
        kn = kh * lax.rsqrt(ms + NORM_EPS) * knw_ref[...]
        kt_ref[h * hd:(h + 1) * hd, :] = kn.T.astype(BF16)
    v_ref[...] = kv_ref[:, nkv * hd:2 * nkv * hd].astype(BF16)

    log_f = _log_sigmoid(fl_ref[...] + fb_ref[...])
    r = lax.broadcasted_iota(jnp.int32, (tm, tm), 0)
    c = lax.broadcasted_iota(jnp.int32, (tm, tm), 1)
    tril = jnp.where(r >= c, 1.0, 0.0).astype(F32)
    cs = jnp.dot(tril, log_f, preferred_element_type=F32,
                 precision=lax.Precision.HIGHEST) + carry_ref[...]
    carry_ref[...] = cs[tm - 1:tm, :]
    fh_ref[...] = (cs * LOG2E).T[0:fh_ref.shape[0], :]


def kv_prep(kv, fl, knw, fb_row, tm=512):
    l = kv.shape[0]
    tm = min(tm, l)
    kvd = FOX_KV_HEADS * FOX_HEAD_DIM
    nh = FOX_KV_HEADS * FOX_GROUP
    row = lambda i: (0, 0)
    return pl.pallas_call(
        _kv_prep_kernel,
        grid=(l // tm,),
        in_specs=[pl.BlockSpec((tm, 2 * kvd), lambda i: (i, 0)),
                  pl.BlockSpec((tm, LANES), lambda i: (i, 0)),
                  pl.BlockSpec((1, FOX_HEAD_DIM), row), pl.BlockSpec((1, LANES), row)],
        out_specs=[pl.BlockSpec((kvd, tm), lambda i: (0, i)),
                   pl.BlockSpec((tm, kvd), lambda i: (i, 0)),
                   pl.BlockSpec((nh, tm), lambda i: (0, i))],
        out_shape=[jax.ShapeDtypeStruct((kvd, l), BF16), jax.ShapeDtypeStruct((l, kvd), BF16),
                   jax.ShapeDtypeStruct((nh, l), F32)],
        scratch_shapes=[pltpu.VMEM((1, LANES), F32)],
        compiler_params=_cparams(("arbitrary",)),
        name="kv_prep",
    )(kv, fl, knw, fb_row)


def _fox_kernel(q_ref, gate_ref, kt_ref, v_ref, fh_ref, qnw_ref, c0_ref, c1_ref, o_ref, co0_ref,
                co1_ref, qn_ref, m_ref, l_ref, acc_ref, *, tk):
    _cast_passengers((c0_ref, c1_ref), (co0_ref, co1_ref))
    i = pl.program_id(1)
    tq = q_ref.shape[0]
    hd = FOX_HEAD_DIM
    scale = hd ** -0.5 * LOG2E
    for g in range(FOX_GROUP):
        qh = q_ref[:, g * hd:(g + 1) * hd]
        ms = jnp.mean(qh * qh, axis=-1, keepdims=True)
        qn_ref[g] = (qh * (lax.rsqrt(ms + NORM_EPS) * scale) * qnw_ref[...]).astype(BF16)
    m_ref[...] = jnp.full(m_ref.shape, -jnp.inf, F32)
    l_ref[...] = jnp.zeros_like(l_ref)
    acc_ref[...] = jnp.zeros_like(acc_ref)
    n_full = (i * tq + 1) // tk
    key_minus_query = (lax.broadcasted_iota(jnp.int32, (tq, tk), 1)
                       - lax.broadcasted_iota(jnp.int32, (tq, tk), 0))
    causal = key_minus_query <= i * tq - n_full * tk

    def tile(j, masked):
        start = pl.multiple_of(j * tk, tk)
        kj = kt_ref[:, pl.ds(start, tk)]
        vj = v_ref[pl.ds(start, tk), :]
        heads = range(FOX_GROUP)
        ss = [jnp.dot(qn_ref[g], kj, preferred_element_type=F32)
              - fh_ref[g:g + 1, pl.ds(start, tk)] for g in heads]
        if masked:
            ss = [jnp.where(causal, s, -jnp.inf) for s in ss]
        sc = [[s[:, c * LANES:(c + 1) * LANES] for c in range(tk // LANES)] for s in ss]
        m_olds = [m_ref[g] for g in heads]
        m_news = [jnp.maximum(m_olds[g], jnp.max(functools.reduce(jnp.maximum, sc[g]),
                                                 axis=-1, keepdims=True)) for g in heads]
        alphas = [jnp.exp2(m_olds[g] - m_news[g]) for g in heads]
        pc = [[jnp.exp2(s - m_news[g]) for s in sc[g]] for g in heads]
        for g in heads:
            row_sum = jnp.sum(functools.reduce(jnp.add, pc[g]), axis=-1, keepdims=True)
            l_ref[g] = alphas[g] * l_ref[g] + row_sum
            m_ref[g] = m_news[g]
        pvs = [jnp.dot(jnp.concatenate(pc[g], axis=1).astype(BF16), vj,
                       preferred_element_type=F32) for g in heads]
        for g in heads:
            for c in range(hd // LANES):
                cols = slice(c * LANES, (c + 1) * LANES)
                acc_ref[g, :, cols] = alphas[g] * acc_ref[g, :, cols] + pvs[g][:, cols]

    def body(j, carry):
        tile(j, False)
        return carry

    lax.fori_loop(0, n_full, body, 0)
    tile(n_full, True)
    for g in range(FOX_GROUP):
        inv_l = 1.0 / l_ref[g]
        for c in range(hd // LANES):
            cols = slice(g * hd + c * LANES, g * hd + (c + 1) * LANES)
            o_ref[:, cols] = (acc_ref[g, :, c * LANES:(c + 1) * LANES] * inv_l
                              * jax.nn.sigmoid(gate_ref[:, cols])).astype(o_ref.dtype)


def fox_attention(qg, kt, v, fh, qnw, cast_srcs, cast_layer, tq=256, tk=1024):
    l = qg.shape[0]
    hd = FOX_HEAD_DIM
    gw = FOX_GROUP * hd
    tk = min(tk, l)
    tq = min(tq, tk)
    assert tk % tq == 0 and l % tk == 0
    nq = l // tq
    c_in, c_out, c_shape = _cast_specs(cast_srcs, cast_layer, FOX_KV_HEADS * nq,
                                       lambda h, i: h * nq + i)
    return pl.pallas_call(
        functools.partial(_fox_kernel, tk=tk),
        grid=(FOX_KV_HEADS, l // tq),
        in_specs=[
            pl.BlockSpec((tq, gw), lambda h, i: (i, h)),
            pl.BlockSpec((tq, gw), lambda h, i: (i, FOX_KV_HEADS + h)),
            pl.BlockSpec((hd, l), lambda h, i: (h, 0), pipeline_mode=pl.Buffered(1)),
            pl.BlockSpec((l, hd), lambda h, i: (0, h), pipeline_mode=pl.Buffered(1)),
            pl.BlockSpec((FOX_GROUP, l), lambda h, i: (h, 0)),
            pl.BlockSpec((1, hd), lambda h, i: (0, 0)),
        ] + c_in,
        out_specs=[pl.BlockSpec((tq, gw), lambda h, i: (i, h))] + c_out,
        out_shape=[jax.ShapeDtypeStruct((l, FOX_KV_HEADS * gw), BF16)] + c_shape,
        scratch_shapes=[pltpu.VMEM((FOX_GROUP, tq, hd), BF16),
                        pltpu.VMEM((FOX_GROUP, tq, LANES), F32),
                        pltpu.VMEM((FOX_GROUP, tq, LANES), F32),
                        pltpu.VMEM((FOX_GROUP, tq, hd), F32)],
        compiler_params=_cparams(("arbitrary", "arbitrary")),
        name="fox_attention",
    )(qg, qg, kt, v, fh, qnw, *cast_srcs)


def kernel(x, c, ada_w, ada_b, norm_mix, norm_ffn, ffn_w_in, ffn_w_out, gdn_w_in, gdn_conv,
           gdn_a_log, gdn_dt_bias, gdn_norm, gdn_w_out, kv_ada_w, kv_ada_b, kv_norm, kv_w, k_norm,
           forget_b, fox_w_in, q_norm, fox_w_out, out_ada_w, out_ada_b, out_norm):
    bsz, l, d = x.shape
    assert bsz == 1 and ada_w.shape[0] == 2 and gdn_w_in.shape[0] == 1 and fox_w_in.shape[0] == 1
    xs = x.reshape(l, d)
    c_col = c.reshape(d, 1)
    row = lambda t: t.reshape(1, -1)

    def mods(w3, b3, layer, n):
        m = adaln(c_col, w3, b3, layer)
        return [m[:, i * d:(i + 1) * d] for i in range(n)]

    ada_b3 = ada_b[:, None, :]
    conv_dim = gdn_conv.shape[2]
    gdn_main = conv_dim + GDN_V_HEADS * GDN_HEAD_DIM
    kvd = FOX_KV_HEADS * FOX_HEAD_DIM

    sh_m, sc_m, g_m, sh_f, sc_f, g_f = mods(ada_w, ada_b3, 0, 6)
    ffn_srcs = (ffn_w_in.reshape(-1, ffn_w_in.shape[2]), ffn_w_out.reshape(-1, d))
    ffn_ws = lambda w_in_b, w_out_b: (w_in_b[None], w_out_b[None])
    proj, small, gdn_w_out_b = nm_proj_t(
        xs, row(norm_mix[0]), sh_m, sc_m, jnp.swapaxes(gdn_w_in, 1, 2), 0, gdn_main,
        gdn_w_in.shape[2] - gdn_main, cast_src=gdn_w_out.reshape(-1, d))
    pad32 = lambda t: jnp.pad(t, (GDN_V_HEADS, LANES - 2 * GDN_V_HEADS)).reshape(1, LANES)
    gt, gh = gdn_gates(small, pad32(gdn_a_log[0]), pad32(gdn_dt_bias[0]))
    o, w_in_b, w_out_b = gdn_core(proj, gdn_conv[0], gt, gh, row(gdn_norm[0]), ffn_srcs, 0)
    xs = proj_res(o, gdn_w_out_b[None], 0, xs, g_m)
    one = row(out_norm)
    xs = ffn(xs, row(norm_ffn[0]), sh_f, sc_f, g_f, *ffn_ws(w_in_b, w_out_b), 0, one, one, one,
             final=False)

    sh_k, sc_k = mods(kv_ada_w[None], kv_ada_b[None, None, :], 0, 2)
    kvp, fl = nm_proj_t(xs, row(kv_norm), sh_k, sc_k, kv_w.T[None], 0, 2 * kvd,
                        kv_w.shape[1] - 2 * kvd)
    fb_row = jnp.pad(forget_b, (0, LANES - forget_b.shape[0])).reshape(1, LANES)
    k_sh, v_sh, fh = kv_prep(kvp, fl, row(k_norm), fb_row)

    sh_m, sc_m, g_m, sh_f, sc_f, g_f = mods(ada_w, ada_b3, 1, 6)
    qg, fox_w_out_b = nm_proj(xs, row(norm_mix[1]), sh_m, sc_m, fox_w_in, 0,
                              fox_w_out.reshape(-1, d))
    a, w_in_b, w_out_b = fox_attention(qg, k_sh, v_sh, fh, row(q_norm[0]), ffn_srcs, 1)
    xs = proj_res(a, fox_w_out_b[None], 0, xs, g_m)
    sh_o, sc_o = mods(out_ada_w[None], out_ada_b[None, None, :], 0, 2)
    xs = ffn(xs, row(norm_ffn[1]), sh_f, sc_f, g_f, *ffn_ws(w_in_b, w_out_b), 0, row(out_norm),
             sh_o, sc_o, final=True)
    return xs.reshape(bsz, l, d)
```

```python
import functools
import math

import jax
import jax.numpy as jnp
from jax import lax
from jax.experimental import pallas as pl
from jax.experimental.pallas import tpu as pltpu

F32 = jnp.float32
BF16 = jnp.bfloat16
NORM_EPS = 1e-6
LOG2E = math.log2(math.e)

V7X_VMEM_BYTES = 64 * 1024 * 1024
VMEM_LIMIT_BYTES = 56 * 1024 * 1024
LANES = 128
SUBLANES = 8

GDN_CHUNK = 64
GDN_HEAD_DIM = 128
GDN_V_HEADS = 32
GDN_QK_HEADS = 16
CONV_K = 4
FOX_HEAD_DIM = 256
FOX_KV_HEADS = 2
FOX_GROUP = 8

NT_DIMS = (((1,), (1,)), ((), ()))
TN_DIMS = (((0,), (0,)), ((), ()))


def _cparams(sem):
    return pltpu.CompilerParams(dimension_semantics=sem, vmem_limit_bytes=VMEM_LIMIT_BYTES)


def _silu(x):
    return x * jax.nn.sigmoid(x)


def _softplus(x):
    return jnp.maximum(x, 0.0) + jnp.log1p(jnp.exp(-jnp.abs(x)))


def _log_sigmoid(x):
    return jnp.minimum(x, 0.0) - jnp.log1p(jnp.exp(-jnp.abs(x)))


def _norm_mod(x, nw, sh, sc):
    ms = jnp.mean(x * x, axis=-1, keepdims=True)
    y = x * lax.rsqrt(ms + NORM_EPS) * nw
    return y * (1.0 + sc) + sh


NORM_ROWS = 256


def _norm_mod_rows(dst_ref, src_fn, nw, sh, sc):
    tm = dst_ref.shape[0]
    step = min(NORM_ROWS, tm)
    for r in range(0, tm, step):
        rows = slice(r, r + step)
        dst_ref[rows, :] = _norm_mod(src_fn(rows), nw, sh, sc).astype(dst_ref.dtype)


def _adaln_kernel(c_ref, w_ref, b_ref, o_ref):
    cond = _silu(c_ref[...])
    o_ref[...] = jnp.sum(w_ref[...] * cond, axis=0, keepdims=True) + b_ref[...]


def adaln(c_col, w3, b3, layer, tn=1024):
    _, d, n = w3.shape
    tn = min(tn, n)
    return pl.pallas_call(
        _adaln_kernel,
        grid=(n // tn,),
        in_specs=[
            pl.BlockSpec((d, 1), lambda j: (0, 0)),
            pl.BlockSpec((None, d, tn), lambda j: (layer, 0, j)),
            pl.BlockSpec((None, 1, tn), lambda j: (layer, 0, j)),
        ],
        out_specs=pl.BlockSpec((1, tn), lambda j: (0, j)),
        out_shape=jax.ShapeDtypeStruct((1, n), F32),
        compiler_params=_cparams(("arbitrary",)),
        name="adaln",
    )(c_col, w3, b3)


def _nm_proj_kernel(x_ref, nw_ref, sh_ref, sc_ref, w_ref, c_ref, o_ref, co_ref, h_ref):
    _cast_passengers((c_ref,), (co_ref,))

    @pl.when(pl.program_id(1) == 0)
    def _():
        _norm_mod_rows(h_ref, lambda rows: x_ref[rows, :], nw_ref[...], sh_ref[...], sc_ref[...])

    o_ref[...] = jnp.dot(h_ref[...], w_ref[...], preferred_element_type=F32)


def nm_proj(x, nw, sh, sc, w3, layer, cast_src, tm=1024, tn=1024):
    l, d = x.shape
    n = w3.shape[2]
    tm, tn = min(tm, l), min(tn, n)
    nj = n // tn
    row = lambda i, j: (0, 0)
    c_in, c_out, c_shape = _cast_specs([cast_src], 0, (l // tm) * nj, lambda i, j: i * nj + j, 1)
    return pl.pallas_call(
        _nm_proj_kernel,
        grid=(l // tm, nj),
        in_specs=[
            pl.BlockSpec((tm, d), lambda i, j: (i, 0)),
            pl.BlockSpec((1, d), row), pl.BlockSpec((1, d), row), pl.BlockSpec((1, d), row),
            pl.BlockSpec((None, d, tn), lambda i, j: (layer, 0, j)),
        ] + c_in,
        out_specs=[pl.BlockSpec((tm, tn), lambda i, j: (i, j))] + c_out,
        out_shape=[jax.ShapeDtypeStruct((l, n), F32)] + c_shape,
        scratch_shapes=[pltpu.VMEM((tm, d), BF16)],
        compiler_params=_cparams(("arbitrary", "arbitrary")),
        name="nm_proj",
    )(x, nw, sh, sc, w3, cast_src)


def _nm_proj_t_kernel(x_ref, nw_ref, sh_ref, sc_ref, wt_ref, wst_ref, *rest, cast):
    if cast:
        c_ref, o_ref, os_ref, co_ref, h_ref = rest
        _cast_passengers((c_ref,), (co_ref,))
    else:
        o_ref, os_ref, h_ref = rest
    tm = o_ref.shape[0]

    @pl.when(pl.program_id(1) == 0)
    def _():
        _norm_mod_rows(h_ref, lambda rows: x_ref[rows, :], nw_ref[...], sh_ref[...], sc_ref[...])
        ns = wst_ref.shape[0]
        os_ref[:, 0:ns] = lax.dot_general(h_ref[...], wst_ref[...].astype(BF16), NT_DIMS,
                                          preferred_element_type=F32)
        os_ref[:, ns:] = jnp.zeros((tm, LANES - ns), F32)

    o_ref[...] = lax.dot_general(h_ref[...], wt_ref[...].astype(BF16), NT_DIMS,
                                 preferred_element_type=F32)


def nm_proj_t(x, nw, sh, sc, wt3, layer, n, ns, cast_src=None, tm=1024, tn=1024):
    l, d = x.shape
    tm, tn = min(tm, l), min(tn, n)
    assert n % tn == 0 and n % ns == 0
    nj = n // tn
    cast = cast_src is not None
    row = lambda i, j: (0, 0)
    in_specs = [
        pl.BlockSpec((tm, d), lambda i, j: (i, 0)),
        pl.BlockSpec((1, d), row), pl.BlockSpec((1, d), row), pl.BlockSpec((1, d), row),
        pl.BlockSpec((None, tn, d), lambda i, j: (layer, j, 0)),
        pl.BlockSpec((None, ns, d), lambda i, j: (layer, n // ns, 0)),
    ]
    out_specs = [pl.BlockSpec((tm, tn), lambda i, j: (i, j)),
                 pl.BlockSpec((tm, LANES), lambda i, j: (i, 0))]
    out_shape = [jax.ShapeDtypeStruct((l, n), F32), jax.ShapeDtypeStruct((l, LANES), F32)]
    args = [x, nw, sh, sc, wt3, wt3]
    if cast:
        c_in, c_out, c_shape = _cast_specs([cast_src], 0, (l // tm) * nj,
                                           lambda i, j: i * nj + j, 1)
        in_specs, out_specs, out_shape = in_specs + c_in, out_specs + c_out, out_shape + c_shape
        args.append(cast_src)
    return pl.pallas_call(
        functools.partial(_nm_proj_t_kernel, cast=cast),
        grid=(l // tm, nj),
        in_specs=in_specs,
        out_specs=out_specs,
        out_shape=out_shape,
        scratch_shapes=[pltpu.VMEM((tm, d), BF16)],
        compiler_params=_cparams(("arbitrary", "arbitrary")),
        name="nm_proj_t",
    )(*args)


def _proj_res_kernel(a_ref, w_ref, x_ref, g_ref, o_ref):
    y = jnp.dot(a_ref[...], w_ref[...], preferred_element_type=F32)
    o_ref[...] = x_ref[...] + g_ref[...] * y


def proj_res(a, w3, layer, x, g, tm=1024, tn=1024):
    l, k = a.shape
    d = w3.shape[2]
    tm, tn = min(tm, l), min(tn, d)
    return pl.pallas_call(
        _proj_res_kernel,
        grid=(l // tm, d // tn),
        in_specs=[
            pl.BlockSpec((tm, k), lambda i, j: (i, 0)),
            pl.BlockSpec((None, k, tn), lambda i, j: (layer, 0, j)),
            pl.BlockSpec((tm, tn), lambda i, j: (i, j)),
            pl.BlockSpec((1, tn), lambda i, j: (0, j)),
        ],
        out_specs=pl.BlockSpec((tm, tn), lambda i, j: (i, j)),
        out_shape=jax.ShapeDtypeStruct((l, d), F32),
        compiler_params=_cparams(("arbitrary", "arbitrary")),
        name="proj_res",
    )(a, w3, x, g)


def _ffn_kernel(x_ref, nw_ref, sh_ref, sc_ref, g_ref, wg_ref, wu_ref, wo_ref, *rest, final, cast):
    rest = list(rest)
    fnw_ref, fsh_ref, fsc_ref = (rest.pop(0), rest.pop(0), rest.pop(0)) if final else (None,) * 3
    c_ref = rest.pop(0) if cast else None
    o_ref = rest.pop(0)
    co_ref = rest.pop(0) if cast else None
    h_ref, acc_ref = rest
    if cast:
        _cast_passengers((c_ref,), (co_ref,))
    j = pl.program_id(1)

    @pl.when(j == 0)
    def _():
        _norm_mod_rows(h_ref, lambda rows: x_ref[rows, :], nw_ref[...], sh_ref[...], sc_ref[...])
        acc_ref[...] = jnp.zeros_like(acc_ref)

    h = h_ref[...]
    gate = jnp.dot(h, wg_ref[...], preferred_element_type=F32)
    up = jnp.dot(h, wu_ref[...], preferred_element_type=F32)
    act = (_silu(gate) * up).astype(BF16)
    acc_ref[...] += jnp.dot(act, wo_ref[...], preferred_element_type=F32)

    @pl.when(j == pl.num_programs(1) - 1)
    def _():
        res = lambda rows: x_ref[rows, :] + g_ref[...] * acc_ref[rows, :]
        if final:
            _norm_mod_rows(o_ref, res, fnw_ref[...], fsh_ref[...], fsc_ref[...])
        else:
            o_ref[...] = res(slice(None))


def ffn(x, nw, sh, sc, g, w_in, w_out, final_mod=None, cast_src=None, tm=512, th=512):
    l, d = x.shape
    hdim = w_out.shape[0]
    tm, th = min(tm, l), min(th, hdim)
    nh = hdim // th
    final, cast = final_mod is not None, cast_src is not None
    row = lambda i, j: (0, 0)
    vec = pl.BlockSpec((1, d), row)
    in_specs = [
        pl.BlockSpec((tm, d), lambda i, j: (i, 0)),
        vec, vec, vec, vec,
        pl.BlockSpec((d, th), lambda i, j: (0, j)),
        pl.BlockSpec((d, th), lambda i, j: (0, j + nh)),
        pl.BlockSpec((th, d), lambda i, j: (j, 0)),
    ]
    out_specs = [pl.BlockSpec((tm, d), lambda i, j: (i, 0))]
    out_shape = [jax.ShapeDtypeStruct((l, d), F32)]
    args = [x, nw, sh, sc, g, w_in, w_in, w_out]
    if final:
        in_specs += [vec, vec, vec]
        args += list(final_mod)
    if cast:
        c_in, c_out, c_shape = _cast_specs([cast_src], 0, (l // tm) * nh,
                                           lambda i, j: i * nh + j, 1)
        in_specs, out_specs, out_shape = in_specs + c_in, out_specs + c_out, out_shape + c_shape
        args.append(cast_src)
    out = pl.pallas_call(
        functools.partial(_ffn_kernel, final=final, cast=cast),
        grid=(l // tm, nh),
        in_specs=in_specs,
        out_specs=out_specs,
        out_shape=out_shape,
        scratch_shapes=[pltpu.VMEM((tm, d), BF16), pltpu.VMEM((tm, d), F32)],
        compiler_params=_cparams(("arbitrary", "arbitrary")),
        name="ffn",
    )(*args)
    return out if cast else out[0]


def _gdn_gates_kernel(s_ref, alog_ref, dt_ref, gt_ref, gh_ref):
    x = s_ref[...]
    tm = x.shape[0]
    lane = lax.broadcasted_iota(jnp.int32, x.shape, 1)
    beta = jax.nn.sigmoid(x)
    g = -jnp.exp(alog_ref[...]) * _softplus(x + dt_ref[...])
    g = jnp.where((lane >= GDN_V_HEADS) & (lane < 2 * GDN_V_HEADS), g, 0.0)
    r = lax.broadcasted_iota(jnp.int32, (tm, tm), 0)
    c = lax.broadcasted_iota(jnp.int32, (tm, tm), 1)
    same_chunk = (r // GDN_CHUNK) == (c // GDN_CHUNK)
    tril = jnp.where((r >= c) & same_chunk, 1.0, 0.0).astype(F32)
    gcum = jnp.dot(tril, g, preferred_element_type=F32, precision=lax.Precision.HIGHEST)
    gt = jnp.where(lane < GDN_V_HEADS, beta, gcum)
    gt_ref[...] = gt
    gh_ref[...] = gt.T


def gdn_gates(small, alog_row, dt_row, tm=512):
    l = small.shape[0]
    tm = min(tm, l)
    row = lambda i: (0, 0)
    return pl.pallas_call(
        _gdn_gates_kernel,
        grid=(l // tm,),
        in_specs=[pl.BlockSpec((tm, LANES), lambda i: (i, 0)),
                  pl.BlockSpec((1, LANES), row), pl.BlockSpec((1, LANES), row)],
        out_specs=[pl.BlockSpec((tm, LANES), lambda i: (i, 0)),
                   pl.BlockSpec((LANES, tm), lambda i: (0, i))],
        out_shape=[jax.ShapeDtypeStruct((l, LANES), F32), jax.ShapeDtypeStruct((LANES, l), F32)],
        compiler_params=_cparams(("arbitrary",)),
        name="gdn_gates",
    )(small, alog_row, dt_row)


GDN_PAIRS = 8


def _conv_silu(x_ref, xe_ref, w_ref):
    tb = x_ref.shape[0]
    pad = SUBLANES
    xe_ref[pad:pad + tb, :] = x_ref[...]
    w = w_ref[...]
    acc = w[CONV_K - 1:CONV_K, :] * xe_ref[pad:pad + tb, :]
    for i in range(CONV_K - 1):
        off = pad - (CONV_K - 1) + i
        acc = acc + w[i:i + 1, :] * xe_ref[off:off + tb, :]
    xe_ref[0:pad, :] = xe_ref[tb:tb + pad, :]
    return _silu(acc)


def _cast_passengers(in_refs, out_refs):
    for src, dst in zip(in_refs, out_refs):
        dst[...] = src[...].astype(dst.dtype)


def _cast_specs(arrays, layer, n_steps, step_of, n_layers=2):
    in_specs, out_specs, out_shape = [], [], []
    bf16_rows = 16
    for a in arrays:
        rows = a.shape[0] // n_layers
        assert a.shape[0] % n_layers == 0 and rows % bf16_rows == 0
        n_blk = max(n for n in range(1, n_steps + 1) if (rows // bf16_rows) % n == 0)
        blk = rows // n_blk
        blk_of = lambda *g, _n=n_blk: jnp.minimum(step_of(*g), _n - 1)
        in_specs.append(pl.BlockSpec((blk, a.shape[1]),
                                     lambda *g, _n=n_blk, _b=blk_of: (layer * _n + _b(*g), 0)))
        out_specs.append(pl.BlockSpec((blk, a.shape[1]), lambda *g, _b=blk_of: (_b(*g), 0)))
        out_shape.append(jax.ShapeDtypeStruct((rows, a.shape[1]), BF16))
    return in_specs, out_specs, out_shape


def _gdn_kernel(q_ref, k_ref, v_ref, z_ref, wq_ref, wk_ref, wv_ref, gt_ref, gh_ref, nw_ref,
                c0_ref, c1_ref, o_ref, co0_ref, co1_ref, xq_ref, xk_ref, xv_ref, s_ref, *, pairs):
    _cast_passengers((c0_ref, c1_ref), (co0_ref, co1_ref))
    gi = pl.program_id(0)
    tb = q_ref.shape[0]
    hd = GDN_HEAD_DIM
    ck = GDN_CHUNK
    nck = tb // ck
    n_sq = int(math.log2(ck)) - 1

    @pl.when(pl.program_id(1) == 0)
    def _():
        for xe_ref in (xq_ref, xk_ref, xv_ref):
            xe_ref[0:SUBLANES, :] = jnp.zeros((SUBLANES, xe_ref.shape[1]), F32)
        s_ref[...] = jnp.zeros_like(s_ref)

    q_all = _conv_silu(q_ref, xq_ref, wq_ref)
    k_all = _conv_silu(k_ref, xk_ref, wk_ref)
    v_all = _conv_silu(v_ref, xv_ref, wv_ref)

    head0 = 2 * pairs * gi
    gt = pltpu.roll(gt_ref[...], lax.rem(LANES - head0, LANES), axis=1)
    r_idx = lax.broadcasted_iota(jnp.int32, (tb, tb), 0)
    c_idx = lax.broadcasted_iota(jnp.int32, (tb, tb), 1)
    same = (r_idx // ck) == (c_idx // ck)
    causal = same & (r_idx >= c_idx)
    strict = same & (r_idx > c_idx)
    same_b = jnp.where(same, 1.0, 0.0).astype(BF16)
    nw = nw_ref[...]

    def block_diag(packed):
        return jnp.concatenate([packed.astype(BF16)] * nck, axis=0) * same_b

    heads = range(2 * pairs)
    qs, ks, kks, qks = [], [], [], []
    for pi in range(pairs):
        q = q_all[:, pi * hd:(pi + 1) * hd]
        k = k_all[:, pi * hd:(pi + 1) * hd]
        q = q * (lax.rsqrt(jnp.sum(q * q, axis=-1, keepdims=True) + NORM_EPS) * hd ** -0.5)
        k = k * lax.rsqrt(jnp.sum(k * k, axis=-1, keepdims=True) + NORM_EPS)
        kb = k.astype(BF16)
        qs.append(q)
        ks.append(k)
        kks.append(lax.dot_general(kb, kb, NT_DIMS, preferred_element_type=F32))
        qks.append(lax.dot_general(q.astype(BF16), kb, NT_DIMS, preferred_element_type=F32))

    bcs, gccs, attns, ps = [], [], [], []
    for h in heads:
        bc = gt[:, h:h + 1]
        gcc = gt[:, GDN_V_HEADS + h:GDN_V_HEADS + h + 1]
        gcr = gh_ref[pl.ds(GDN_V_HEADS + head0 + h, 1), :]
        decay = jnp.exp(jnp.where(causal, gcc - gcr, -jnp.inf))
        a_full = jnp.where(strict, kks[h // 2] * decay * bc, 0.0)
        attn = qks[h // 2] * decay
        attns.append([attn[c * ck:(c + 1) * ck, c * ck:(c + 1) * ck].astype(BF16)
                      for c in range(nck)])
        p = -a_full[0:ck]
        for c in range(1, nck):
            p = p - a_full[c * ck:(c + 1) * ck]
        bcs.append(bc)
        gccs.append(gcc)
        ps.append(p)

    rs = list(ps)
    p_bds = [block_diag(p) for p in ps]
    for _ in range(n_sq):
        ps = [jnp.dot(ps[h].astype(BF16), p_bds[h], preferred_element_type=F32) for h in heads]
        p_bds = [block_diag(p) for p in ps]
        rs = [rs[h] + ps[h] + jnp.dot(rs[h].astype(BF16), p_bds[h], preferred_element_type=F32)
              for h in heads]

    egs = [jnp.exp(gcc) for gcc in gccs]
    us, wqs = [], []
    for h in heads:
        k = ks[h // 2]
        rhs = bcs[h] * jnp.concatenate([v_all[:, h * hd:(h + 1) * hd], k * egs[h]], axis=1)
        uw = rhs + jnp.dot(block_diag(rs[h]), rhs.astype(BF16), preferred_element_type=F32)
        us.append(uw[:, :hd])
        q_dec = qs[h // 2] * egs[h]
        wqs.append([jnp.concatenate([uw[c * ck:(c + 1) * ck, hd:], q_dec[c * ck:(c + 1) * ck]],
                                    axis=0).astype(BF16) for c in range(nck)])

    states = [s_ref[h] for h in heads]
    outs = [[] for _ in heads]
    for ci in range(nck):
        rows = slice(ci * ck, (ci + 1) * ck)
        wq_s = [jnp.dot(wqs[h][ci], states[h].astype(BF16), preferred_element_type=F32)
                for h in heads]
        v_news = [(us[h][rows] - wq_s[h][:ck]).astype(BF16) for h in heads]
        for h in heads:
            outs[h].append(wq_s[h][ck:] + jnp.dot(attns[h][ci], v_news[h],
                                                  preferred_element_type=F32))
        for h in heads:
            g_last = gccs[h][(ci + 1) * ck - 1:(ci + 1) * ck, :]
            k_dec = (ks[h // 2][rows] * jnp.exp(g_last - gccs[h][rows])).astype(BF16)
            states[h] = states[h] * jnp.exp(g_last) + lax.dot_general(
                k_dec, v_news[h], TN_DIMS, preferred_element_type=F32)

    for h in heads:
        s_ref[h] = states[h]
        o = jnp.concatenate(outs[h], axis=0)
        on = o * lax.rsqrt(jnp.mean(o * o, axis=-1, keepdims=True) + NORM_EPS) * nw
        cols = slice(h * hd, (h + 1) * hd)
        o_ref[:, cols] = (on * _silu(z_ref[:, cols])).astype(o_ref.dtype)


def gdn_core(proj, conv_w, gt, gh, norm_w, cast_srcs, cast_layer, tb=256, pairs=GDN_PAIRS):
    l = proj.shape[0]
    hd = GDN_HEAD_DIM
    tb = min(tb, l)
    ng = GDN_QK_HEADS // pairs
    nb = l // tb
    qw, vw = pairs * hd, 2 * pairs * hd
    c_in, c_out, c_shape = _cast_specs(cast_srcs, cast_layer, ng * nb, lambda g, b: g * nb + b)
    return pl.pallas_call(
        functools.partial(_gdn_kernel, pairs=pairs),
        grid=(ng, nb),
        in_specs=[
            pl.BlockSpec((tb, qw), lambda g, b: (b, g)),
            pl.BlockSpec((tb, qw), lambda g, b: (b, ng + g)),
            pl.BlockSpec((tb, vw), lambda g, b: (b, ng + g)),
            pl.BlockSpec((tb, vw), lambda g, b: (b, 2 * ng + g)),
            pl.BlockSpec((CONV_K, qw), lambda g, b: (0, g)),
            pl.BlockSpec((CONV_K, qw), lambda g, b: (0, ng + g)),
            pl.BlockSpec((CONV_K, vw), lambda g, b: (0, ng + g)),
            pl.BlockSpec((tb, LANES), lambda g, b: (b, 0)),
            pl.BlockSpec((LANES, tb), lambda g, b: (0, b)),
            pl.BlockSpec((1, hd), lambda g, b: (0, 0)),
        ] + c_in,
        out_specs=[pl.BlockSpec((tb, vw), lambda g, b: (b, g))] + c_out,
        out_shape=[jax.ShapeDtypeStruct((l, GDN_V_HEADS * hd), BF16)] + c_shape,
        scratch_shapes=[pltpu.VMEM((tb + SUBLANES, qw), F32),
                        pltpu.VMEM((tb + SUBLANES, qw), F32),
                        pltpu.VMEM((tb + SUBLANES, vw), F32),
                        pltpu.VMEM((2 * pairs, hd, hd), F32)],
        compiler_params=_cparams(("arbitrary", "arbitrary")),
        name="gdn_core",
    )(proj, proj, proj, proj, conv_w, conv_w, conv_w, gt, gh, norm_w, *cast_srcs)


def _kv_prep_kernel(kv_ref, fl_ref, knw_ref, fb_ref, kt_ref, v_ref, fh_ref, carry_ref):
    hd = FOX_HEAD_DIM
    nkv = FOX_KV_HEADS
    tm = kv_ref.shape[0]

    @pl.when(pl.program_id(0) == 0)
    def _():
        carry_ref[...] = jnp.zeros_like(carry_ref)

    for h in range(nkv):
        kh = kv_ref[:, h * hd:(h + 1) * hd]
        ms = jnp.mean(kh * kh, axis=-1, keepdims=True)
        kn = kh * lax.rsqrt(ms + NORM_EPS) * knw_ref[...]
        kt_ref[h * hd:(h + 1) * hd, :] = kn.T.astype(BF16)
    v_ref[...] = kv_ref[:, nkv * hd:2 * nkv * hd].astype(BF16)

    log_f = _log_sigmoid(fl_ref[...] + fb_ref[...])
    r = lax.broadcasted_iota(jnp.int32, (tm, tm), 0)
    c = lax.broadcasted_iota(jnp.int32, (tm, tm), 1)
    tril = jnp.where(r >= c, 1.0, 0.0).astype(F32)
    cs = jnp.dot(tril, log_f, preferred_element_type=F32,
                 precision=lax.Precision.HIGHEST) + carry_ref[...]
    carry_ref[...] = cs[tm - 1:tm, :]
    fh_ref[...] = (cs * LOG2E).T[0:fh_ref.shape[0], :]


def kv_prep(kv, fl, knw, fb_row, tm=512):
    l = kv.shape[0]
    tm = min(tm, l)
    kvd = FOX_KV_HEADS * FOX_HEAD_DIM
    nh = FOX_KV_HEADS * FOX_GROUP
    row = lambda i: (0, 0)
    return pl.pallas_call(
        _kv_prep_kernel,
        grid=(l // tm,),
        in_specs=[pl.BlockSpec((tm, 2 * kvd), lambda i: (i, 0)),
                  pl.BlockSpec((tm, LANES), lambda i: (i, 0)),
                  pl.BlockSpec((1, FOX_HEAD_DIM), row), pl.BlockSpec((1, LANES), row)],
        out_specs=[pl.BlockSpec((kvd, tm), lambda i: (0, i)),
                   pl.BlockSpec((tm, kvd), lambda i: (i, 0)),
                   pl.BlockSpec((nh, tm), lambda i: (0, i))],
        out_shape=[jax.ShapeDtypeStruct((kvd, l), BF16), jax.ShapeDtypeStruct((l, kvd), BF16),
                   jax.ShapeDtypeStruct((nh, l), F32)],
        scratch_shapes=[pltpu.VMEM((1, LANES), F32)],
        compiler_params=_cparams(("arbitrary",)),
        name="kv_prep",
    )(kv, fl, knw, fb_row)


def _fox_kernel(q_ref, gate_ref, kt_ref, v_ref, fh_ref, qnw_ref, c0_ref, c1_ref, o_ref, co0_ref,
                co1_ref, qn_ref, m_ref, l_ref, acc_ref, *, tk):
    _cast_passengers((c0_ref, c1_ref), (co0_ref, co1_ref))
    i = pl.program_id(1)
    tq = q_ref.shape[0]
    hd = FOX_HEAD_DIM
    scale = hd ** -0.5 * LOG2E
    for g in range(FOX_GROUP):
        qh = q_ref[:, g * hd:(g + 1) * hd]
        ms = jnp.mean(qh * qh, axis=-1, keepdims=True)
        qn_ref[g] = (qh * (lax.rsqrt(ms + NORM_EPS) * scale) * qnw_ref[...]).astype(BF16)
    m_ref[...] = jnp.full(m_ref.shape, -jnp.inf, F32)
    l_ref[...] = jnp.zeros_like(l_ref)
    acc_ref[...] = jnp.zeros_like(acc_ref)
    n_full = (i * tq + 1) // tk
    key_minus_query = (lax.broadcasted_iota(jnp.int32, (tq, tk), 1)
                       - lax.broadcasted_iota(jnp.int32, (tq, tk), 0))
    causal = key_minus_query <= i * tq - n_full * tk

    def tile(j, masked):
        start = pl.multiple_of(j * tk, tk)
        kj = kt_ref[:, pl.ds(start, tk)]
        vj = v_ref[pl.ds(start, tk), :]
        heads = range(FOX_GROUP)
        ss = [jnp.dot(qn_ref[g], kj, preferred_element_type=F32)
              - fh_ref[g:g + 1, pl.ds(start, tk)] for g in heads]
        if masked:
            ss = [jnp.where(causal, s, -jnp.inf) for s in ss]
        sc = [[s[:, c * LANES:(c + 1) * LANES] for c in range(tk // LANES)] for s in ss]
        m_olds = [m_ref[g] for g in heads]
        m_news = [jnp.maximum(m_olds[g], jnp.max(functools.reduce(jnp.maximum, sc[g]),
                                                 axis=-1, keepdims=True)) for g in heads]
        alphas = [jnp.exp2(m_olds[g] - m_news[g]) for g in heads]
        pc = [[jnp.exp2(s - m_news[g]) for s in sc[g]] for g in heads]
        for g in heads:
            row_sum = jnp.sum(functools.reduce(jnp.add, pc[g]), axis=-1, keepdims=True)
            l_ref[g] = alphas[g] * l_ref[g] + row_sum
            m_ref[g] = m_news[g]
        pvs = [jnp.dot(jnp.concatenate(pc[g], axis=1).astype(BF16), vj,
                       preferred_element_type=F32) for g in heads]
        for g in heads:
            for c in range(hd // LANES):
                cols = slice(c * LANES, (c + 1) * LANES)
                acc_ref[g, :, cols] = alphas[g] * acc_ref[g, :, cols] + pvs[g][:, cols]

    def body(j, carry):
        tile(j, False)
        return carry

    lax.fori_loop(0, n_full, body, 0)
    tile(n_full, True)
    for g in range(FOX_GROUP):
        inv_l = 1.0 / l_ref[g]
        for c in range(hd // LANES):
            cols = slice(g * hd + c * LANES, g * hd + (c + 1) * LANES)
            o_ref[:, cols] = (acc_ref[g, :, c * LANES:(c + 1) * LANES] * inv_l
                              * jax.nn.sigmoid(gate_ref[:, cols])).astype(o_ref.dtype)


def fox_attention(qg, kt, v, fh, qnw, cast_srcs, cast_layer, tq=256, tk=1024):
    l = qg.shape[0]
    hd = FOX_HEAD_DIM
    gw = FOX_GROUP * hd
    tk = min(tk, l)
    tq = min(tq, tk)
    assert tk % tq == 0 and l % tk == 0
    nq = l // tq
    c_in, c_out, c_shape = _cast_specs(cast_srcs, cast_layer, FOX_KV_HEADS * nq,
                                       lambda h, i: h * nq + i)
    return pl.pallas_call(
        functools.partial(_fox_kernel, tk=tk),
        grid=(FOX_KV_HEADS, l // tq),
        in_specs=[
            pl.BlockSpec((tq, gw), lambda h, i: (i, h)),
            pl.BlockSpec((tq, gw), lambda h, i: (i, FOX_KV_HEADS + h)),
            pl.BlockSpec((hd, l), lambda h, i: (h, 0), pipeline_mode=pl.Buffered(1)),
            pl.BlockSpec((l, hd), lambda h, i: (0, h), pipeline_mode=pl.Buffered(1)),
            pl.BlockSpec((FOX_GROUP, l), lambda h, i: (h, 0)),
            pl.BlockSpec((1, hd), lambda h, i: (0, 0)),
        ] + c_in,
        out_specs=[pl.BlockSpec((tq, gw), lambda h, i: (i, h))] + c_out,
        out_shape=[jax.ShapeDtypeStruct((l, FOX_KV_HEADS * gw), BF16)] + c_shape,
        scratch_shapes=[pltpu.VMEM((FOX_GROUP, tq, hd), BF16),
                        pltpu.VMEM((FOX_GROUP, tq, LANES), F32),
                        pltpu.VMEM((FOX_GROUP, tq, LANES), F32),
                        pltpu.VMEM((FOX_GROUP, tq, hd), F32)],
        compiler_params=_cparams(("arbitrary", "arbitrary")),
        name="fox_attention",
    )(qg, qg, kt, v, fh, qnw, *cast_srcs)


def kernel(x, c, ada_w, ada_b, norm_mix, norm_ffn, ffn_w_in, ffn_w_out, gdn_w_in, gdn_conv,
           gdn_a_log, gdn_dt_bias, gdn_norm, gdn_w_out, kv_ada_w, kv_ada_b, kv_norm, kv_w, k_norm,
           forget_b, fox_w_in, q_norm, fox_w_out, out_ada_w, out_ada_b, out_norm):
    bsz, l, d = x.shape
    assert bsz == 1 and ada_w.shape[0] == 2 and gdn_w_in.shape[0] == 1 and fox_w_in.shape[0] == 1
    xs = x.reshape(l, d)
    c_col = c.reshape(d, 1)
    row = lambda t: t.reshape(1, -1)

    def mods(w3, b3, layer, n):
        m = adaln(c_col, w3, b3, layer)
        return [m[:, i * d:(i + 1) * d] for i in range(n)]

    ada_b3 = ada_b[:, None, :]
    conv_dim = gdn_conv.shape[2]
    gdn_main = conv_dim + GDN_V_HEADS * GDN_HEAD_DIM
    kvd = FOX_KV_HEADS * FOX_HEAD_DIM

    sh_m, sc_m, g_m, sh_f, sc_f, g_f = mods(ada_w, ada_b3, 0, 6)
    ffn_srcs = (ffn_w_in.reshape(-1, ffn_w_in.shape[2]), ffn_w_out.reshape(-1, d))
    proj, small, gdn_w_out_b = nm_proj_t(
        xs, row(norm_mix[0]), sh_m, sc_m, jnp.swapaxes(gdn_w_in, 1, 2), 0, gdn_main,
        gdn_w_in.shape[2] - gdn_main, cast_src=gdn_w_out.reshape(-1, d))
    pad32 = lambda t: jnp.pad(t, (GDN_V_HEADS, LANES - 2 * GDN_V_HEADS)).reshape(1, LANES)
    gt, gh = gdn_gates(small, pad32(gdn_a_log[0]), pad32(gdn_dt_bias[0]))
    o, w_in_b, w_out_b = gdn_core(proj, gdn_conv[0], gt, gh, row(gdn_norm[0]), ffn_srcs, 0)
    xs = proj_res(o, gdn_w_out_b[None], 0, xs, g_m)
    xs, fox_w_in_b = ffn(xs, row(norm_ffn[0]), sh_f, sc_f, g_f, w_in_b, w_out_b,
                         cast_src=fox_w_in.reshape(d, -1))

    sh_k, sc_k = mods(kv_ada_w[None], kv_ada_b[None, None, :], 0, 2)
    kvp, fl = nm_proj_t(xs, row(kv_norm), sh_k, sc_k, kv_w.T[None], 0, 2 * kvd,
                        kv_w.shape[1] - 2 * kvd)
    fb_row = jnp.pad(forget_b, (0, LANES - forget_b.shape[0])).reshape(1, LANES)
    k_sh, v_sh, fh = kv_prep(kvp, fl, row(k_norm), fb_row)

    sh_m, sc_m, g_m, sh_f, sc_f, g_f = mods(ada_w, ada_b3, 1, 6)
    qg, fox_w_out_b = nm_proj(xs, row(norm_mix[1]), sh_m, sc_m, fox_w_in_b[None], 0,
                              fox_w_out.reshape(-1, d))
    a, w_in_b, w_out_b = fox_attention(qg, k_sh, v_sh, fh, row(q_norm[0]), ffn_srcs, 1)
    xs = proj_res(a, fox_w_out_b[None], 0, xs, g_m)
    sh_o, sc_o = mods(out_ada_w[None], out_ada_b[None, None, :], 0, 2)
    xs = ffn(xs, row(norm_ffn[1]), sh_f, sc_f, g_f, w_in_b, w_out_b,
             final_mod=(row(out_norm), sh_o, sc_o))
    return xs.reshape(bsz, l, d)
```

```python
import functools
import math

import jax
import jax.numpy as jnp
from jax import lax
from jax.experimental import pallas as pl
from jax.experimental.pallas import tpu as pltpu

F32 = jnp.float32
BF16 = jnp.bfloat16
NORM_EPS = 1e-6
LOG2E = math.log2(math.e)

V7X_VMEM_BYTES = 64 * 1024 * 1024
VMEM_LIMIT_BYTES = 56 * 1024 * 1024
LANES = 128
SUBLANES = 8

GDN_CHUNK = 64
GDN_HEAD_DIM = 128
GDN_V_HEADS = 32
GDN_QK_HEADS = 16
CONV_K = 4
FOX_HEAD_DIM = 256
FOX_KV_HEADS = 2
FOX_GROUP = 8

NT_DIMS = (((1,), (1,)), ((), ()))
TN_DIMS = (((0,), (0,)), ((), ()))


def _cparams(sem):
    return pltpu.CompilerParams(dimension_semantics=sem, vmem_limit_bytes=VMEM_LIMIT_BYTES)


def _silu(x):
    h = 0.5 * x
    return h + h * jnp.tanh(h)


def _softplus(x):
    return jnp.maximum(x, 0.0) + jnp.log1p(jnp.exp(-jnp.abs(x)))


def _log_sigmoid(x):
    return jnp.minimum(x, 0.0) - jnp.log1p(jnp.exp(-jnp.abs(x)))


def _norm_mod(x, nw, sh, sc):
    ms = jnp.mean(x * x, axis=-1, keepdims=True)
    y = x * lax.rsqrt(ms + NORM_EPS) * nw
    return y * (1.0 + sc) + sh


NORM_ROWS = 256


def _norm_mod_rows(dst_ref, src_fn, nw, sh, sc):
    tm = dst_ref.shape[0]
    step = min(NORM_ROWS, tm)
    for r in range(0, tm, step):
        rows = slice(r, r + step)
        dst_ref[rows, :] = _norm_mod(src_fn(rows), nw, sh, sc).astype(dst_ref.dtype)


def _adaln_kernel(c_ref, w_ref, b_ref, o_ref):
    cond = _silu(c_ref[...])
    o_ref[...] = jnp.sum(w_ref[...] * cond, axis=0, keepdims=True) + b_ref[...]


def adaln(c_col, w3, b3, layer, tn=1024):
    _, d, n = w3.shape
    tn = min(tn, n)
    return pl.pallas_call(
        _adaln_kernel,
        grid=(n // tn,),
        in_specs=[
            pl.BlockSpec((d, 1), lambda j: (0, 0)),
            pl.BlockSpec((None, d, tn), lambda j: (layer, 0, j)),
            pl.BlockSpec((None, 1, tn), lambda j: (layer, 0, j)),
        ],
        out_specs=pl.BlockSpec((1, tn), lambda j: (0, j)),
        out_shape=jax.ShapeDtypeStruct((1, n), F32),
        compiler_params=_cparams(("arbitrary",)),
        name="adaln",
    )(c_col, w3, b3)


def _nm_proj_kernel(x_ref, nw_ref, sh_ref, sc_ref, w_ref, c_ref, o_ref, co_ref, h_ref):
    _cast_passengers((c_ref,), (co_ref,))

    @pl.when(pl.program_id(1) == 0)
    def _():
        _norm_mod_rows(h_ref, lambda rows: x_ref[rows, :], nw_ref[...], sh_ref[...], sc_ref[...])

    o_ref[...] = jnp.dot(h_ref[...], w_ref[...], preferred_element_type=F32)


def nm_proj(x, nw, sh, sc, w3, layer, cast_src, tm=1024, tn=1024):
    l, d = x.shape
    n = w3.shape[2]
    tm, tn = min(tm, l), min(tn, n)
    nj = n // tn
    row = lambda i, j: (0, 0)
    c_in, c_out, c_shape = _cast_specs([cast_src], 0, (l // tm) * nj, lambda i, j: i * nj + j, 1)
    return pl.pallas_call(
        _nm_proj_kernel,
        grid=(l // tm, nj),
        in_specs=[
            pl.BlockSpec((tm, d), lambda i, j: (i, 0)),
            pl.BlockSpec((1, d), row), pl.BlockSpec((1, d), row), pl.BlockSpec((1, d), row),
            pl.BlockSpec((None, d, tn), lambda i, j: (layer, 0, j)),
        ] + c_in,
        out_specs=[pl.BlockSpec((tm, tn), lambda i, j: (i, j))] + c_out,
        out_shape=[jax.ShapeDtypeStruct((l, n), F32)] + c_shape,
        scratch_shapes=[pltpu.VMEM((tm, d), BF16)],
        compiler_params=_cparams(("arbitrary", "arbitrary")),
        name="nm_proj",
    )(x, nw, sh, sc, w3, cast_src)


def _nm_proj_t_kernel(x_ref, nw_ref, sh_ref, sc_ref, wt_ref, wst_ref, *rest, cast):
    if cast:
        c_ref, o_ref, os_ref, co_ref, h_ref = rest
        _cast_passengers((c_ref,), (co_ref,))
    else:
        o_ref, os_ref, h_ref = rest
    tm = o_ref.shape[0]

    @pl.when(pl.program_id(1) == 0)
    def _():
        _norm_mod_rows(h_ref, lambda rows: x_ref[rows, :], nw_ref[...], sh_ref[...], sc_ref[...])
        ns = wst_ref.shape[0]
        os_ref[:, 0:ns] = lax.dot_general(h_ref[...], wst_ref[...].astype(BF16), NT_DIMS,
                                          preferred_element_type=F32)
        os_ref[:, ns:] = jnp.zeros((tm, LANES - ns), F32)

    o_ref[...] = lax.dot_general(h_ref[...], wt_ref[...].astype(BF16), NT_DIMS,
                                 preferred_element_type=F32)


def nm_proj_t(x, nw, sh, sc, wt3, layer, n, ns, cast_src=None, tm=1024, tn=1024):
    l, d = x.shape
    tm, tn = min(tm, l), min(tn, n)
    assert n % tn == 0 and n % ns == 0
    nj = n // tn
    cast = cast_src is not None
    row = lambda i, j: (0, 0)
    in_specs = [
        pl.BlockSpec((tm, d), lambda i, j: (i, 0)),
        pl.BlockSpec((1, d), row), pl.BlockSpec((1, d), row), pl.BlockSpec((1, d), row),
        pl.BlockSpec((None, tn, d), lambda i, j: (layer, j, 0)),
        pl.BlockSpec((None, ns, d), lambda i, j: (layer, n // ns, 0)),
    ]
    out_specs = [pl.BlockSpec((tm, tn), lambda i, j: (i, j)),
                 pl.BlockSpec((tm, LANES), lambda i, j: (i, 0))]
    out_shape = [jax.ShapeDtypeStruct((l, n), F32), jax.ShapeDtypeStruct((l, LANES), F32)]
    args = [x, nw, sh, sc, wt3, wt3]
    if cast:
        c_in, c_out, c_shape = _cast_specs([cast_src], 0, (l // tm) * nj,
                                           lambda i, j: i * nj + j, 1)
        in_specs, out_specs, out_shape = in_specs + c_in, out_specs + c_out, out_shape + c_shape
        args.append(cast_src)
    return pl.pallas_call(
        functools.partial(_nm_proj_t_kernel, cast=cast),
        grid=(l // tm, nj),
        in_specs=in_specs,
        out_specs=out_specs,
        out_shape=out_shape,
        scratch_shapes=[pltpu.VMEM((tm, d), BF16)],
        compiler_params=_cparams(("arbitrary", "arbitrary")),
        name="nm_proj_t",
    )(*args)


def _proj_res_kernel(a_ref, w_ref, x_ref, g_ref, o_ref):
    y = jnp.dot(a_ref[...], w_ref[...], preferred_element_type=F32)
    o_ref[...] = x_ref[...] + g_ref[...] * y


def proj_res(a, w3, layer, x, g, tm=1024, tn=1024):
    l, k = a.shape
    d = w3.shape[2]
    tm, tn = min(tm, l), min(tn, d)
    return pl.pallas_call(
        _proj_res_kernel,
        grid=(l // tm, d // tn),
        in_specs=[
            pl.BlockSpec((tm, k), lambda i, j: (i, 0)),
            pl.BlockSpec((None, k, tn), lambda i, j: (layer, 0, j)),
            pl.BlockSpec((tm, tn), lambda i, j: (i, j)),
            pl.BlockSpec((1, tn), lambda i, j: (0, j)),
        ],
        out_specs=pl.BlockSpec((tm, tn), lambda i, j: (i, j)),
        out_shape=jax.ShapeDtypeStruct((l, d), F32),
        compiler_params=_cparams(("arbitrary", "arbitrary")),
        name="proj_res",
    )(a, w3, x, g)


def _ffn_kernel(x_ref, nw_ref, sh_ref, sc_ref, g_ref, wg_ref, wu_ref, wo_ref, *rest, final, cast):
    rest = list(rest)
    fnw_ref, fsh_ref, fsc_ref = (rest.pop(0), rest.pop(0), rest.pop(0)) if final else (None,) * 3
    c_ref = rest.pop(0) if cast else None
    o_ref = rest.pop(0)
    co_ref = rest.pop(0) if cast else None
    h_ref, acc_ref = rest
    if cast:
        _cast_passengers((c_ref,), (co_ref,))
    j = pl.program_id(1)

    @pl.when(j == 0)
    def _():
        _norm_mod_rows(h_ref, lambda rows: x_ref[rows, :], nw_ref[...], sh_ref[...], sc_ref[...])
        acc_ref[...] = jnp.zeros_like(acc_ref)

    h = h_ref[...]
    gate = jnp.dot(h, wg_ref[...], preferred_element_type=F32)
    up = jnp.dot(h, wu_ref[...], preferred_element_type=F32)
    act = (_silu(gate) * up).astype(BF16)
    acc_ref[...] += jnp.dot(act, wo_ref[...], preferred_element_type=F32)

    @pl.when(j == pl.num_programs(1) - 1)
    def _():
        res = lambda rows: x_ref[rows, :] + g_ref[...] * acc_ref[rows, :]
        if final:
            _norm_mod_rows(o_ref, res, fnw_ref[...], fsh_ref[...], fsc_ref[...])
        else:
            o_ref[...] = res(slice(None))


def ffn(x, nw, sh, sc, g, w_in, w_out, final_mod=None, cast_src=None, tm=512, th=512):
    l, d = x.shape
    hdim = w_out.shape[0]
    tm, th = min(tm, l), min(th, hdim)
    nh = hdim // th
    final, cast = final_mod is not None, cast_src is not None
    row = lambda i, j: (0, 0)
    vec = pl.BlockSpec((1, d), row)
    in_specs = [
        pl.BlockSpec((tm, d), lambda i, j: (i, 0)),
        vec, vec, vec, vec,
        pl.BlockSpec((d, th), lambda i, j: (0, j)),
        pl.BlockSpec((d, th), lambda i, j: (0, j + nh)),
        pl.BlockSpec((th, d), lambda i, j: (j, 0)),
    ]
    out_specs = [pl.BlockSpec((tm, d), lambda i, j: (i, 0))]
    out_shape = [jax.ShapeDtypeStruct((l, d), F32)]
    args = [x, nw, sh, sc, g, w_in, w_in, w_out]
    if final:
        in_specs += [vec, vec, vec]
        args += list(final_mod)
    if cast:
        c_in, c_out, c_shape = _cast_specs([cast_src], 0, (l // tm) * nh,
                                           lambda i, j: i * nh + j, 1)
        in_specs, out_specs, out_shape = in_specs + c_in, out_specs + c_out, out_shape + c_shape
        args.append(cast_src)
    out = pl.pallas_call(
        functools.partial(_ffn_kernel, final=final, cast=cast),
        grid=(l // tm, nh),
        in_specs=in_specs,
        out_specs=out_specs,
        out_shape=out_shape,
        scratch_shapes=[pltpu.VMEM((tm, d), BF16), pltpu.VMEM((tm, d), F32)],
        compiler_params=_cparams(("arbitrary", "arbitrary")),
        name="ffn",
    )(*args)
    return out if cast else out[0]


def _gdn_gates_kernel(s_ref, alog_ref, dt_ref, gt_ref, gh_ref):
    x = s_ref[...]
    tm = x.shape[0]
    lane = lax.broadcasted_iota(jnp.int32, x.shape, 1)
    beta = jax.nn.sigmoid(x)
    g = -jnp.exp(alog_ref[...]) * _softplus(x + dt_ref[...])
    g = jnp.where((lane >= GDN_V_HEADS) & (lane < 2 * GDN_V_HEADS), g, 0.0)
    r = lax.broadcasted_iota(jnp.int32, (tm, tm), 0)
    c = lax.broadcasted_iota(jnp.int32, (tm, tm), 1)
    same_chunk = (r // GDN_CHUNK) == (c // GDN_CHUNK)
    tril = jnp.where((r >= c) & same_chunk, 1.0, 0.0).astype(F32)
    gcum = jnp.dot(tril, g, preferred_element_type=F32, precision=lax.Precision.HIGHEST)
    gt = jnp.where(lane < GDN_V_HEADS, beta, gcum)
    gt_ref[...] = gt
    gh_ref[...] = gt.T


def gdn_gates(small, alog_row, dt_row, tm=512):
    l = small.shape[0]
    tm = min(tm, l)
    row = lambda i: (0, 0)
    return pl.pallas_call(
        _gdn_gates_kernel,
        grid=(l // tm,),
        in_specs=[pl.BlockSpec((tm, LANES), lambda i: (i, 0)),
                  pl.BlockSpec((1, LANES), row), pl.BlockSpec((1, LANES), row)],
        out_specs=[pl.BlockSpec((tm, LANES), lambda i: (i, 0)),
                   pl.BlockSpec((LANES, tm), lambda i: (0, i))],
        out_shape=[jax.ShapeDtypeStruct((l, LANES), F32), jax.ShapeDtypeStruct((LANES, l), F32)],
        compiler_params=_cparams(("arbitrary",)),
        name="gdn_gates",
    )(small, alog_row, dt_row)


GDN_PAIRS = 8


def _conv_silu(x_ref, xe_ref, w_ref):
    tb = x_ref.shape[0]
    pad = SUBLANES
    xe_ref[pad:pad + tb, :] = x_ref[...]
    w = w_ref[...]
    acc = w[CONV_K - 1:CONV_K, :] * xe_ref[pad:pad + tb, :]
    for i in range(CONV_K - 1):
        off = pad - (CONV_K - 1) + i
        acc = acc + w[i:i + 1, :] * xe_ref[off:off + tb, :]
    xe_ref[0:pad, :] = xe_ref[tb:tb + pad, :]
    return _silu(acc)


def _cast_passengers(in_refs, out_refs):
    for src, dst in zip(in_refs, out_refs):
        dst[...] = src[...].astype(dst.dtype)


def _cast_specs(arrays, layer, n_steps, step_of, n_layers=2):
    in_specs, out_specs, out_shape = [], [], []
    bf16_rows = 16
    for a in arrays:
        rows = a.shape[0] // n_layers
        assert a.shape[0] % n_layers == 0 and rows % bf16_rows == 0
        n_blk = max(n for n in range(1, n_steps + 1) if (rows // bf16_rows) % n == 0)
        blk = rows // n_blk
        blk_of = lambda *g, _n=n_blk: jnp.minimum(step_of(*g), _n - 1)
        in_specs.append(pl.BlockSpec((blk, a.shape[1]),
                                     lambda *g, _n=n_blk, _b=blk_of: (layer * _n + _b(*g), 0)))
        out_specs.append(pl.BlockSpec((blk, a.shape[1]), lambda *g, _b=blk_of: (_b(*g), 0)))
        out_shape.append(jax.ShapeDtypeStruct((rows, a.shape[1]), BF16))
    return in_specs, out_specs, out_shape


def _gdn_kernel(q_ref, k_ref, v_ref, z_ref, wq_ref, wk_ref, wv_ref, gt_ref, gh_ref, nw_ref,
                c0_ref, c1_ref, o_ref, co0_ref, co1_ref, xq_ref, xk_ref, xv_ref, s_ref, *, pairs):
    _cast_passengers((c0_ref, c1_ref), (co0_ref, co1_ref))
    gi = pl.program_id(0)
    tb = q_ref.shape[0]
    hd = GDN_HEAD_DIM
    ck = GDN_CHUNK
    nck = tb // ck
    n_sq = int(math.log2(ck)) - 1

    @pl.when(pl.program_id(1) == 0)
    def _():
        for xe_ref in (xq_ref, xk_ref, xv_ref):
            xe_ref[0:SUBLANES, :] = jnp.zeros((SUBLANES, xe_ref.shape[1]), F32)
        s_ref[...] = jnp.zeros_like(s_ref)

    q_all = _conv_silu(q_ref, xq_ref, wq_ref)
    k_all = _conv_silu(k_ref, xk_ref, wk_ref)
    v_all = _conv_silu(v_ref, xv_ref, wv_ref)

    head0 = 2 * pairs * gi
    gt = pltpu.roll(gt_ref[...], lax.rem(LANES - head0, LANES), axis=1)
    r_idx = lax.broadcasted_iota(jnp.int32, (tb, tb), 0)
    c_idx = lax.broadcasted_iota(jnp.int32, (tb, tb), 1)
    same = (r_idx // ck) == (c_idx // ck)
    causal = same & (r_idx >= c_idx)
    strict = same & (r_idx > c_idx)
    same_b = jnp.where(same, 1.0, 0.0).astype(BF16)
    nw = nw_ref[...]

    def block_diag(packed):
        return jnp.concatenate([packed.astype(BF16)] * nck, axis=0) * same_b

    heads = range(2 * pairs)
    qs, ks, kks, qks = [], [], [], []
    for pi in range(pairs):
        q = q_all[:, pi * hd:(pi + 1) * hd]
        k = k_all[:, pi * hd:(pi + 1) * hd]
        q = q * (lax.rsqrt(jnp.sum(q * q, axis=-1, keepdims=True) + NORM_EPS) * hd ** -0.5)
        k = k * lax.rsqrt(jnp.sum(k * k, axis=-1, keepdims=True) + NORM_EPS)
        kb = k.astype(BF16)
        qs.append(q)
        ks.append(k)
        kks.append(lax.dot_general(kb, kb, NT_DIMS, preferred_element_type=F32))
        qks.append(lax.dot_general(q.astype(BF16), kb, NT_DIMS, preferred_element_type=F32))

    bcs, gccs, attns, ps = [], [], [], []
    for h in heads:
        bc = gt[:, h:h + 1]
        gcc = gt[:, GDN_V_HEADS + h:GDN_V_HEADS + h + 1]
        gcr = gh_ref[pl.ds(GDN_V_HEADS + head0 + h, 1), :]
        decay = jnp.exp(jnp.where(causal, gcc - gcr, -jnp.inf))
        a_full = jnp.where(strict, kks[h // 2] * decay * bc, 0.0)
        attn = qks[h // 2] * decay
        attns.append([attn[c * ck:(c + 1) * ck, c * ck:(c + 1) * ck].astype(BF16)
                      for c in range(nck)])
        p = -a_full[0:ck]
        for c in range(1, nck):
            p = p - a_full[c * ck:(c + 1) * ck]
        bcs.append(bc)
        gccs.append(gcc)
        ps.append(p)

    rs = list(ps)
    p_bds = [block_diag(p) for p in ps]
    for _ in range(n_sq):
        ps = [jnp.dot(ps[h].astype(BF16), p_bds[h], preferred_element_type=F32) for h in heads]
        p_bds = [block_diag(p) for p in ps]
        rs = [rs[h] + ps[h] + jnp.dot(rs[h].astype(BF16), p_bds[h], preferred_element_type=F32)
              for h in heads]

    egs = [jnp.exp(gcc) for gcc in gccs]
    us, wqs = [], []
    for h in heads:
        k = ks[h // 2]
        rhs = bcs[h] * jnp.concatenate([v_all[:, h * hd:(h + 1) * hd], k * egs[h]], axis=1)
        uw = rhs + jnp.dot(block_diag(rs[h]), rhs.astype(BF16), preferred_element_type=F32)
        us.append(uw[:, :hd])
        q_dec = qs[h // 2] * egs[h]
        wqs.append([jnp.concatenate([uw[c * ck:(c + 1) * ck, hd:], q_dec[c * ck:(c + 1) * ck]],
                                    axis=0).astype(BF16) for c in range(nck)])

    states = [s_ref[h] for h in heads]
    outs = [[] for _ in heads]
    for ci in range(nck):
        rows = slice(ci * ck, (ci + 1) * ck)
        wq_s = [jnp.dot(wqs[h][ci], states[h].astype(BF16), preferred_element_type=F32)
                for h in heads]
        v_news = [(us[h][rows] - wq_s[h][:ck]).astype(BF16) for h in heads]
        for h in heads:
            outs[h].append(wq_s[h][ck:] + jnp.dot(attns[h][ci], v_news[h],
                                                  preferred_element_type=F32))
        for h in heads:
            g_last = gccs[h][(ci + 1) * ck - 1:(ci + 1) * ck, :]
            k_dec = (ks[h // 2][rows] * jnp.exp(g_last - gccs[h][rows])).astype(BF16)
            states[h] = states[h] * jnp.exp(g_last) + lax.dot_general(
                k_dec, v_news[h], TN_DIMS, preferred_element_type=F32)

    for h in heads:
        s_ref[h] = states[h]
        o = jnp.concatenate(outs[h], axis=0)
        on = o * lax.rsqrt(jnp.mean(o * o, axis=-1, keepdims=True) + NORM_EPS) * nw
        cols = slice(h * hd, (h + 1) * hd)
        o_ref[:, cols] = (on * _silu(z_ref[:, cols])).astype(o_ref.dtype)


def gdn_core(proj, conv_w, gt, gh, norm_w, cast_srcs, cast_layer, tb=256, pairs=GDN_PAIRS):
    l = proj.shape[0]
    hd = GDN_HEAD_DIM
    tb = min(tb, l)
    ng = GDN_QK_HEADS // pairs
    nb = l // tb
    qw, vw = pairs * hd, 2 * pairs * hd
    c_in, c_out, c_shape = _cast_specs(cast_srcs, cast_layer, ng * nb, lambda g, b: g * nb + b)
    return pl.pallas_call(
        functools.partial(_gdn_kernel, pairs=pairs),
        grid=(ng, nb),
        in_specs=[
            pl.BlockSpec((tb, qw), lambda g, b: (b, g)),
            pl.BlockSpec((tb, qw), lambda g, b: (b, ng + g)),
            pl.BlockSpec((tb, vw), lambda g, b: (b, ng + g)),
            pl.BlockSpec((tb, vw), lambda g, b: (b, 2 * ng + g)),
            pl.BlockSpec((CONV_K, qw), lambda g, b: (0, g)),
            pl.BlockSpec((CONV_K, qw), lambda g, b: (0, ng + g)),
            pl.BlockSpec((CONV_K, vw), lambda g, b: (0, ng + g)),
            pl.BlockSpec((tb, LANES), lambda g, b: (b, 0)),
            pl.BlockSpec((LANES, tb), lambda g, b: (0, b)),
            pl.BlockSpec((1, hd), lambda g, b: (0, 0)),
        ] + c_in,
        out_specs=[pl.BlockSpec((tb, vw), lambda g, b: (b, g))] + c_out,
        out_shape=[jax.ShapeDtypeStruct((l, GDN_V_HEADS * hd), BF16)] + c_shape,
        scratch_shapes=[pltpu.VMEM((tb + SUBLANES, qw), F32),
                        pltpu.VMEM((tb + SUBLANES, qw), F32),
                        pltpu.VMEM((tb + SUBLANES, vw), F32),
                        pltpu.VMEM((2 * pairs, hd, hd), F32)],
        compiler_params=_cparams(("arbitrary", "arbitrary")),
        name="gdn_core",
    )(proj, proj, proj, proj, conv_w, conv_w, conv_w, gt, gh, norm_w, *cast_srcs)


def _kv_prep_kernel(kv_ref, fl_ref, knw_ref, fb_ref, kt_ref, v_ref, fh_ref, carry_ref):
    hd = FOX_HEAD_DIM
    nkv = FOX_KV_HEADS
    tm = kv_ref.shape[0]

    @pl.when(pl.program_id(0) == 0)
    def _():
        carry_ref[...] = jnp.zeros_like(carry_ref)

    for h in range(nkv):
        kh = kv_ref[:, h * hd:(h + 1) * hd]
        ms = jnp.mean(kh * kh, axis=-1, keepdims=True)
        kn = kh * lax.rsqrt(ms + NORM_EPS) * knw_ref[...]
        kt_ref[h * hd:(h + 1) * hd, :] = kn.T.astype(BF16)
    v_ref[...] = kv_ref[:, nkv * hd:2 * nkv * hd].astype(BF16)

    log_f = _log_sigmoid(fl_ref[...] + fb_ref[...])
    r = lax.broadcasted_iota(jnp.int32, (tm, tm), 0)
    c = lax.broadcasted_iota(jnp.int32, (tm, tm), 1)
    tril = jnp.where(r >= c, 1.0, 0.0).astype(F32)
    cs = jnp.dot(tril, log_f, preferred_element_type=F32,
                 precision=lax.Precision.HIGHEST) + carry_ref[...]
    carry_ref[...] = cs[tm - 1:tm, :]
    fh_ref[...] = (cs * LOG2E).T[0:fh_ref.shape[0], :]


def kv_prep(kv, fl, knw, fb_row, tm=512):
    l = kv.shape[0]
    tm = min(tm, l)
    kvd = FOX_KV_HEADS * FOX_HEAD_DIM
    nh = FOX_KV_HEADS * FOX_GROUP
    row = lambda i: (0, 0)
    return pl.pallas_call(
        _kv_prep_kernel,
        grid=(l // tm,),
        in_specs=[pl.BlockSpec((tm, 2 * kvd), lambda i: (i, 0)),
                  pl.BlockSpec((tm, LANES), lambda i: (i, 0)),
                  pl.BlockSpec((1, FOX_HEAD_DIM), row), pl.BlockSpec((1, LANES), row)],
        out_specs=[pl.BlockSpec((kvd, tm), lambda i: (0, i)),
                   pl.BlockSpec((tm, kvd), lambda i: (i, 0)),
                   pl.BlockSpec((nh, tm), lambda i: (0, i))],
        out_shape=[jax.ShapeDtypeStruct((kvd, l), BF16), jax.ShapeDtypeStruct((l, kvd), BF16),
                   jax.ShapeDtypeStruct((nh, l), F32)],
        scratch_shapes=[pltpu.VMEM((1, LANES), F32)],
        compiler_params=_cparams(("arbitrary",)),
        name="kv_prep",
    )(kv, fl, knw, fb_row)


def _fox_kernel(q_ref, gate_ref, kt_ref, v_ref, fh_ref, qnw_ref, c0_ref, c1_ref, o_ref, co0_ref,
                co1_ref, qn_ref, m_ref, l_ref, acc_ref, *, tk):
    _cast_passengers((c0_ref, c1_ref), (co0_ref, co1_ref))
    i = pl.program_id(1)
    tq = q_ref.shape[0]
    hd = FOX_HEAD_DIM
    scale = hd ** -0.5 * LOG2E
    for g in range(FOX_GROUP):
        qh = q_ref[:, g * hd:(g + 1) * hd]
        ms = jnp.mean(qh * qh, axis=-1, keepdims=True)
        qn_ref[g] = (qh * (lax.rsqrt(ms + NORM_EPS) * scale) * qnw_ref[...]).astype(BF16)
    m_ref[...] = jnp.full(m_ref.shape, -jnp.inf, F32)
    l_ref[...] = jnp.zeros_like(l_ref)
    acc_ref[...] = jnp.zeros_like(acc_ref)
    n_full = (i * tq + 1) // tk
    key_minus_query = (lax.broadcasted_iota(jnp.int32, (tq, tk), 1)
                       - lax.broadcasted_iota(jnp.int32, (tq, tk), 0))
    causal = key_minus_query <= i * tq - n_full * tk

    def tile(j, masked):
        start = pl.multiple_of(j * tk, tk)
        kj = kt_ref[:, pl.ds(start, tk)]
        vj = v_ref[pl.ds(start, tk), :]
        heads = range(FOX_GROUP)
        ss = [jnp.dot(qn_ref[g], kj, preferred_element_type=F32)
              - fh_ref[g:g + 1, pl.ds(start, tk)] for g in heads]
        if masked:
            ss = [jnp.where(causal, s, -jnp.inf) for s in ss]
        sc = [[s[:, c * LANES:(c + 1) * LANES] for c in range(tk // LANES)] for s in ss]
        m_olds = [m_ref[g] for g in heads]
        m_news = [jnp.maximum(m_olds[g], jnp.max(functools.reduce(jnp.maximum, sc[g]),
                                                 axis=-1, keepdims=True)) for g in heads]
        alphas = [jnp.exp2(m_olds[g] - m_news[g]) for g in heads]
        pc = [[jnp.exp2(s - m_news[g]) for s in sc[g]] for g in heads]
        for g in heads:
            row_sum = jnp.sum(functools.reduce(jnp.add, pc[g]), axis=-1, keepdims=True)
            l_ref[g] = alphas[g] * l_ref[g] + row_sum
            m_ref[g] = m_news[g]
        pvs = [jnp.dot(jnp.concatenate(pc[g], axis=1).astype(BF16), vj,
                       preferred_element_type=F32) for g in heads]
        for g in heads:
            for c in range(hd // LANES):
                cols = slice(c * LANES, (c + 1) * LANES)
                acc_ref[g, :, cols] = alphas[g] * acc_ref[g, :, cols] + pvs[g][:, cols]

    def body(j, carry):
        tile(j, False)
        return carry

    lax.fori_loop(0, n_full, body, 0)
    tile(n_full, True)
    for g in range(FOX_GROUP):
        inv_l = 1.0 / l_ref[g]
        for c in range(hd // LANES):
            cols = slice(g * hd + c * LANES, g * hd + (c + 1) * LANES)
            o_ref[:, cols] = (acc_ref[g, :, c * LANES:(c + 1) * LANES] * inv_l
                              * jax.nn.sigmoid(gate_ref[:, cols])).astype(o_ref.dtype)


def fox_attention(qg, kt, v, fh, qnw, cast_srcs, cast_layer, tq=256, tk=1024):
    l = qg.shape[0]
    hd = FOX_HEAD_DIM
    gw = FOX_GROUP * hd
    tk = min(tk, l)
    tq = min(tq, tk)
    assert tk % tq == 0 and l % tk == 0
    nq = l // tq
    c_in, c_out, c_shape = _cast_specs(cast_srcs, cast_layer, FOX_KV_HEADS * nq,
                                       lambda h, i: h * nq + i)
    return pl.pallas_call(
        functools.partial(_fox_kernel, tk=tk),
        grid=(FOX_KV_HEADS, l // tq),
        in_specs=[
            pl.BlockSpec((tq, gw), lambda h, i: (i, h)),
            pl.BlockSpec((tq, gw), lambda h, i: (i, FOX_KV_HEADS + h)),
            pl.BlockSpec((hd, l), lambda h, i: (h, 0), pipeline_mode=pl.Buffered(1)),
            pl.BlockSpec((l, hd), lambda h, i: (0, h), pipeline_mode=pl.Buffered(1)),
            pl.BlockSpec((FOX_GROUP, l), lambda h, i: (h, 0)),
            pl.BlockSpec((1, hd), lambda h, i: (0, 0)),
        ] + c_in,
        out_specs=[pl.BlockSpec((tq, gw), lambda h, i: (i, h))] + c_out,
        out_shape=[jax.ShapeDtypeStruct((l, FOX_KV_HEADS * gw), BF16)] + c_shape,
        scratch_shapes=[pltpu.VMEM((FOX_GROUP, tq, hd), BF16),
                        pltpu.VMEM((FOX_GROUP, tq, LANES), F32),
                        pltpu.VMEM((FOX_GROUP, tq, LANES), F32),
                        pltpu.VMEM((FOX_GROUP, tq, hd), F32)],
        compiler_params=_cparams(("arbitrary", "arbitrary")),
        name="fox_attention",
    )(qg, qg, kt, v, fh, qnw, *cast_srcs)


def kernel(x, c, ada_w, ada_b, norm_mix, norm_ffn, ffn_w_in, ffn_w_out, gdn_w_in, gdn_conv,
           gdn_a_log, gdn_dt_bias, gdn_norm, gdn_w_out, kv_ada_w, kv_ada_b, kv_norm, kv_w, k_norm,
           forget_b, fox_w_in, q_norm, fox_w_out, out_ada_w, out_ada_b, out_norm):
    bsz, l, d = x.shape
    assert bsz == 1 and ada_w.shape[0] == 2 and gdn_w_in.shape[0] == 1 and fox_w_in.shape[0] == 1
    xs = x.reshape(l, d)
    c_col = c.reshape(d, 1)
    row = lambda t: t.reshape(1, -1)

    def mods(w3, b3, layer, n):
        m = adaln(c_col, w3, b3, layer)
        return [m[:, i * d:(i + 1) * d] for i in range(n)]

    ada_b3 = ada_b[:, None, :]
    conv_dim = gdn_conv.shape[2]
    gdn_main = conv_dim + GDN_V_HEADS * GDN_HEAD_DIM
    kvd = FOX_KV_HEADS * FOX_HEAD_DIM

    sh_m, sc_m, g_m, sh_f, sc_f, g_f = mods(ada_w, ada_b3, 0, 6)
    ffn_srcs = (ffn_w_in.reshape(-1, ffn_w_in.shape[2]), ffn_w_out.reshape(-1, d))
    proj, small, gdn_w_out_b = nm_proj_t(
        xs, row(norm_mix[0]), sh_m, sc_m, jnp.swapaxes(gdn_w_in, 1, 2), 0, gdn_main,
        gdn_w_in.shape[2] - gdn_main, cast_src=gdn_w_out.reshape(-1, d))
    pad32 = lambda t: jnp.pad(t, (GDN_V_HEADS, LANES - 2 * GDN_V_HEADS)).reshape(1, LANES)
    gt, gh = gdn_gates(small, pad32(gdn_a_log[0]), pad32(gdn_dt_bias[0]))
    o, w_in_b, w_out_b = gdn_core(proj, gdn_conv[0], gt, gh, row(gdn_norm[0]), ffn_srcs, 0)
    xs = proj_res(o, gdn_w_out_b[None], 0, xs, g_m)
    xs, fox_w_in_b = ffn(xs, row(norm_ffn[0]), sh_f, sc_f, g_f, w_in_b, w_out_b,
                         cast_src=fox_w_in.reshape(d, -1))

    sh_k, sc_k = mods(kv_ada_w[None], kv_ada_b[None, None, :], 0, 2)
    kvp, fl = nm_proj_t(xs, row(kv_norm), sh_k, sc_k, kv_w.T[None], 0, 2 * kvd,
                        kv_w.shape[1] - 2 * kvd)
    fb_row = jnp.pad(forget_b, (0, LANES - forget_b.shape[0])).reshape(1, LANES)
    k_sh, v_sh, fh = kv_prep(kvp, fl, row(k_norm), fb_row)

    sh_m, sc_m, g_m, sh_f, sc_f, g_f = mods(ada_w, ada_b3, 1, 6)
    qg, fox_w_out_b = nm_proj(xs, row(norm_mix[1]), sh_m, sc_m, fox_w_in_b[None], 0,
                              fox_w_out.reshape(-1, d))
    a, w_in_b, w_out_b = fox_attention(qg, k_sh, v_sh, fh, row(q_norm[0]), ffn_srcs, 1)
    xs = proj_res(a, fox_w_out_b[None], 0, xs, g_m)
    sh_o, sc_o = mods(out_ada_w[None], out_ada_b[None, None, :], 0, 2)
    xs = ffn(xs, row(norm_ffn[1]), sh_f, sc_f, g_f, w_in_b, w_out_b,
             final_mod=(row(out_norm), sh_o, sc_o))
    return xs.reshape(bsz, l, d)
```

```python
import functools
import math

import jax
import jax.numpy as jnp
from jax import lax
from jax.experimental import pallas as pl
from jax.experimental.pallas import tpu as pltpu

F32 = jnp.float32
BF16 = jnp.bfloat16
NORM_EPS = 1e-6
LOG2E = math.log2(math.e)

V7X_VMEM_BYTES = 64 * 1024 * 1024
VMEM_LIMIT_BYTES = V7X_VMEM_BYTES * 7 // 8
LANES = 128
SUBLANES = 8

GDN_CHUNK = 64
GDN_HEAD_DIM = 128
GDN_V_HEADS = 32
GDN_QK_HEADS = 16
CONV_K = 4
FOX_HEAD_DIM = 256
FOX_KV_HEADS = 2
FOX_GROUP = 8

NT_DIMS = (((1,), (1,)), ((), ()))
TN_DIMS = (((0,), (0,)), ((), ()))


def _cparams(sem):
    return pltpu.CompilerParams(dimension_semantics=sem, vmem_limit_bytes=VMEM_LIMIT_BYTES)


def _silu(x):
    h = 0.5 * x
    return h + h * jnp.tanh(h)


def _softplus(x):
    return jnp.maximum(x, 0.0) + jnp.log1p(jnp.exp(-jnp.abs(x)))


def _log_sigmoid(x):
    return jnp.minimum(x, 0.0) - jnp.log1p(jnp.exp(-jnp.abs(x)))


def _norm_mod(x, nw, sh, sc):
    ms = jnp.mean(x * x, axis=-1, keepdims=True)
    y = x * lax.rsqrt(ms + NORM_EPS) * nw
    return y * (1.0 + sc) + sh


NORM_ROWS = 256


def _norm_mod_rows(dst_ref, src_fn, nw, sh, sc):
    tm = dst_ref.shape[0]
    step = min(NORM_ROWS, tm)
    for r in range(0, tm, step):
        rows = slice(r, r + step)
        dst_ref[rows, :] = _norm_mod(src_fn(rows), nw, sh, sc).astype(dst_ref.dtype)


def _adaln_kernel(c_ref, w_ref, b_ref, o_ref):
    cond = _silu(c_ref[...])
    o_ref[...] = jnp.sum(w_ref[...] * cond, axis=0, keepdims=True) + b_ref[...]


def adaln(c_col, w3, b3, layer, tn=1024):
    _, d, n = w3.shape
    tn = min(tn, n)
    return pl.pallas_call(
        _adaln_kernel,
        grid=(n // tn,),
        in_specs=[
            pl.BlockSpec((d, 1), lambda j: (0, 0)),
            pl.BlockSpec((None, d, tn), lambda j: (layer, 0, j)),
            pl.BlockSpec((None, 1, tn), lambda j: (layer, 0, j)),
        ],
        out_specs=pl.BlockSpec((1, tn), lambda j: (0, j)),
        out_shape=jax.ShapeDtypeStruct((1, n), F32),
        compiler_params=_cparams(("arbitrary",)),
        name="adaln",
    )(c_col, w3, b3)


def _nm_proj_kernel(x_ref, nw_ref, sh_ref, sc_ref, w_ref, c_ref, o_ref, co_ref, h_ref):
    _cast_passengers((c_ref,), (co_ref,))

    @pl.when(pl.program_id(1) == 0)
    def _():
        _norm_mod_rows(h_ref, lambda rows: x_ref[rows, :], nw_ref[...], sh_ref[...], sc_ref[...])

    o_ref[...] = jnp.dot(h_ref[...], w_ref[...], preferred_element_type=F32)


def nm_proj(x, nw, sh, sc, w3, layer, cast_src, tm=1024, tn=1024):
    l, d = x.shape
    n = w3.shape[2]
    tm, tn = min(tm, l), min(tn, n)
    nj = n // tn
    row = lambda i, j: (0, 0)
    c_in, c_out, c_shape = _cast_specs([cast_src], 0, (l // tm) * nj, lambda i, j: i * nj + j, 1)
    return pl.pallas_call(
        _nm_proj_kernel,
        grid=(l // tm, nj),
        in_specs=[
            pl.BlockSpec((tm, d), lambda i, j: (i, 0)),
            pl.BlockSpec((1, d), row), pl.BlockSpec((1, d), row), pl.BlockSpec((1, d), row),
            pl.BlockSpec((None, d, tn), lambda i, j: (layer, 0, j)),
        ] + c_in,
        out_specs=[pl.BlockSpec((tm, tn), lambda i, j: (i, j))] + c_out,
        out_shape=[jax.ShapeDtypeStruct((l, n), F32)] + c_shape,
        scratch_shapes=[pltpu.VMEM((tm, d), BF16)],
        compiler_params=_cparams(("arbitrary", "arbitrary")),
        name="nm_proj",
    )(x, nw, sh, sc, w3, cast_src)


def _nm_proj_t_kernel(x_ref, nw_ref, sh_ref, sc_ref, wt_ref, wst_ref, *rest, cast):
    if cast:
        c_ref, o_ref, os_ref, co_ref, h_ref = rest
        _cast_passengers((c_ref,), (co_ref,))
    else:
        o_ref, os_ref, h_ref = rest
    tm = o_ref.shape[0]

    @pl.when(pl.program_id(1) == 0)
    def _():
        _norm_mod_rows(h_ref, lambda rows: x_ref[rows, :], nw_ref[...], sh_ref[...], sc_ref[...])
        ns = wst_ref.shape[0]
        os_ref[:, 0:ns] = lax.dot_general(h_ref[...], wst_ref[...].astype(BF16), NT_DIMS,
                                          preferred_element_type=F32)
        os_ref[:, ns:] = jnp.zeros((tm, LANES - ns), F32)

    o_ref[...] = lax.dot_general(h_ref[...], wt_ref[...].astype(BF16), NT_DIMS,
                                 preferred_element_type=F32)


def nm_proj_t(x, nw, sh, sc, wt3, layer, n, ns, cast_src=None, tm=1024, tn=1024):
    l, d = x.shape
    tm, tn = min(tm, l), min(tn, n)
    assert n % tn == 0 and n % ns == 0
    nj = n // tn
    cast = cast_src is not None
    row = lambda i, j: (0, 0)
    in_specs = [
        pl.BlockSpec((tm, d), lambda i, j: (i, 0)),
        pl.BlockSpec((1, d), row), pl.BlockSpec((1, d), row), pl.BlockSpec((1, d), row),
        pl.BlockSpec((None, tn, d), lambda i, j: (layer, j, 0)),
        pl.BlockSpec((None, ns, d), lambda i, j: (layer, n // ns, 0)),
    ]
    out_specs = [pl.BlockSpec((tm, tn), lambda i, j: (i, j)),
                 pl.BlockSpec((tm, LANES), lambda i, j: (i, 0))]
    out_shape = [jax.ShapeDtypeStruct((l, n), F32), jax.ShapeDtypeStruct((l, LANES), F32)]
    args = [x, nw, sh, sc, wt3, wt3]
    if cast:
        c_in, c_out, c_shape = _cast_specs([cast_src], 0, (l // tm) * nj,
                                           lambda i, j: i * nj + j, 1)
        in_specs, out_specs, out_shape = in_specs + c_in, out_specs + c_out, out_shape + c_shape
        args.append(cast_src)
    return pl.pallas_call(
        functools.partial(_nm_proj_t_kernel, cast=cast),
        grid=(l // tm, nj),
        in_specs=in_specs,
        out_specs=out_specs,
        out_shape=out_shape,
        scratch_shapes=[pltpu.VMEM((tm, d), BF16)],
        compiler_params=_cparams(("arbitrary", "arbitrary")),
        name="nm_proj_t",
    )(*args)


def _proj_res_kernel(a_ref, w_ref, x_ref, g_ref, o_ref):
    y = jnp.dot(a_ref[...], w_ref[...], preferred_element_type=F32)
    o_ref[...] = x_ref[...] + g_ref[...] * y


def proj_res(a, w3, layer, x, g, tm=1024, tn=1024):
    l, k = a.shape
    d = w3.shape[2]
    tm, tn = min(tm, l), min(tn, d)
    return pl.pallas_call(
        _proj_res_kernel,
        grid=(l // tm, d // tn),
        in_specs=[
            pl.BlockSpec((tm, k), lambda i, j: (i, 0)),
            pl.BlockSpec((None, k, tn), lambda i, j: (layer, 0, j)),
            pl.BlockSpec((tm, tn), lambda i, j: (i, j)),
            pl.BlockSpec((1, tn), lambda i, j: (0, j)),
        ],
        out_specs=pl.BlockSpec((tm, tn), lambda i, j: (i, j)),
        out_shape=jax.ShapeDtypeStruct((l, d), F32),
        compiler_params=_cparams(("arbitrary", "arbitrary")),
        name="proj_res",
    )(a, w3, x, g)


def _ffn_kernel(x_ref, nw_ref, sh_ref, sc_ref, g_ref, wg_ref, wu_ref, wo_ref, *rest, final, cast):
    rest = list(rest)
    fnw_ref, fsh_ref, fsc_ref = (rest.pop(0), rest.pop(0), rest.pop(0)) if final else (None,) * 3
    c_ref = rest.pop(0) if cast else None
    o_ref = rest.pop(0)
    co_ref = rest.pop(0) if cast else None
    h_ref, acc_ref = rest
    if cast:
        _cast_passengers((c_ref,), (co_ref,))
    j = pl.program_id(1)

    @pl.when(j == 0)
    def _():
        _norm_mod_rows(h_ref, lambda rows: x_ref[rows, :], nw_ref[...], sh_ref[...], sc_ref[...])
        acc_ref[...] = jnp.zeros_like(acc_ref)

    h = h_ref[...]
    gate = jnp.dot(h, wg_ref[...], preferred_element_type=F32)
    up = jnp.dot(h, wu_ref[...], preferred_element_type=F32)
    act = (_silu(gate) * up).astype(BF16)
    acc_ref[...] += jnp.dot(act, wo_ref[...], preferred_element_type=F32)

    @pl.when(j == pl.num_programs(1) - 1)
    def _():
        res = lambda rows: x_ref[rows, :] + g_ref[...] * acc_ref[rows, :]
        if final:
            _norm_mod_rows(o_ref, res, fnw_ref[...], fsh_ref[...], fsc_ref[...])
        else:
            o_ref[...] = res(slice(None))


def ffn(x, nw, sh, sc, g, w_in, w_out, final_mod=None, cast_src=None, tm=512, th=512):
    l, d = x.shape
    hdim = w_out.shape[0]
    tm, th = min(tm, l), min(th, hdim)
    nh = hdim // th
    final, cast = final_mod is not None, cast_src is not None
    row = lambda i, j: (0, 0)
    vec = pl.BlockSpec((1, d), row)
    in_specs = [
        pl.BlockSpec((tm, d), lambda i, j: (i, 0)),
        vec, vec, vec, vec,
        pl.BlockSpec((d, th), lambda i, j: (0, j)),
        pl.BlockSpec((d, th), lambda i, j: (0, j + nh)),
        pl.BlockSpec((th, d), lambda i, j: (j, 0)),
    ]
    out_specs = [pl.BlockSpec((tm, d), lambda i, j: (i, 0))]
    out_shape = [jax.ShapeDtypeStruct((l, d), F32)]
    args = [x, nw, sh, sc, g, w_in, w_in, w_out]
    if final:
        in_specs += [vec, vec, vec]
        args += list(final_mod)
    if cast:
        c_in, c_out, c_shape = _cast_specs([cast_src], 0, (l // tm) * nh,
                                           lambda i, j: i * nh + j, 1)
        in_specs, out_specs, out_shape = in_specs + c_in, out_specs + c_out, out_shape + c_shape
        args.append(cast_src)
    out = pl.pallas_call(
        functools.partial(_ffn_kernel, final=final, cast=cast),
        grid=(l // tm, nh),
        in_specs=in_specs,
        out_specs=out_specs,
        out_shape=out_shape,
        scratch_shapes=[pltpu.VMEM((tm, d), BF16), pltpu.VMEM((tm, d), F32)],
        compiler_params=_cparams(("arbitrary", "arbitrary")),
        name="ffn",
    )(*args)
    return out if cast else out[0]


def _gdn_gates_kernel(s_ref, alog_ref, dt_ref, gt_ref, gh_ref):
    x = s_ref[...]
    tm = x.shape[0]
    lane = lax.broadcasted_iota(jnp.int32, x.shape, 1)
    beta = jax.nn.sigmoid(x)
    g = -jnp.exp(alog_ref[...]) * _softplus(x + dt_ref[...])
    g = jnp.where((lane >= GDN_V_HEADS) & (lane < 2 * GDN_V_HEADS), g, 0.0)
    r = lax.broadcasted_iota(jnp.int32, (tm, tm), 0)
    c = lax.broadcasted_iota(jnp.int32, (tm, tm), 1)
    same_chunk = (r // GDN_CHUNK) == (c // GDN_CHUNK)
    tril = jnp.where((r >= c) & same_chunk, 1.0, 0.0).astype(F32)
    gcum = jnp.dot(tril, g, preferred_element_type=F32, precision=lax.Precision.HIGHEST)
    gt = jnp.where(lane < GDN_V_HEADS, beta, gcum)
    gt_ref[...] = gt
    gh_ref[...] = gt.T


def gdn_gates(small, alog_row, dt_row, tm=512):
    l = small.shape[0]
    tm = min(tm, l)
    row = lambda i: (0, 0)
    return pl.pallas_call(
        _gdn_gates_kernel,
        grid=(l // tm,),
        in_specs=[pl.BlockSpec((tm, LANES), lambda i: (i, 0)),
                  pl.BlockSpec((1, LANES), row), pl.BlockSpec((1, LANES), row)],
        out_specs=[pl.BlockSpec((tm, LANES), lambda i: (i, 0)),
                   pl.BlockSpec((LANES, tm), lambda i: (0, i))],
        out_shape=[jax.ShapeDtypeStruct((l, LANES), F32), jax.ShapeDtypeStruct((LANES, l), F32)],
        compiler_params=_cparams(("arbitrary",)),
        name="gdn_gates",
    )(small, alog_row, dt_row)


GDN_PAIRS = 8


def _conv_silu(x_ref, xe_ref, w_ref):
    tb = x_ref.shape[0]
    pad = SUBLANES
    xe_ref[pad:pad + tb, :] = x_ref[...]
    w = w_ref[...]
    acc = w[CONV_K - 1:CONV_K, :] * xe_ref[pad:pad + tb, :]
    for i in range(CONV_K - 1):
        off = pad - (CONV_K - 1) + i
        acc = acc + w[i:i + 1, :] * xe_ref[off:off + tb, :]
    xe_ref[0:pad, :] = xe_ref[tb:tb + pad, :]
    return _silu(acc)


def _cast_passengers(in_refs, out_refs):
    for src, dst in zip(in_refs, out_refs):
        dst[...] = src[...].astype(dst.dtype)


def _cast_specs(arrays, layer, n_steps, step_of, n_layers=2):
    in_specs, out_specs, out_shape = [], [], []
    bf16_rows = 16
    for a in arrays:
        rows = a.shape[0] // n_layers
        assert a.shape[0] % n_layers == 0 and rows % bf16_rows == 0
        n_blk = max(n for n in range(1, n_steps + 1) if (rows // bf16_rows) % n == 0)
        blk = rows // n_blk
        blk_of = lambda *g, _n=n_blk: jnp.minimum(step_of(*g), _n - 1)
        in_specs.append(pl.BlockSpec((blk, a.shape[1]),
                                     lambda *g, _n=n_blk, _b=blk_of: (layer * _n + _b(*g), 0)))
        out_specs.append(pl.BlockSpec((blk, a.shape[1]), lambda *g, _b=blk_of: (_b(*g), 0)))
        out_shape.append(jax.ShapeDtypeStruct((rows, a.shape[1]), BF16))
    return in_specs, out_specs, out_shape


def _gdn_kernel(q_ref, k_ref, v_ref, z_ref, wq_ref, wk_ref, wv_ref, gt_ref, gh_ref, nw_ref,
                c0_ref, c1_ref, o_ref, co0_ref, co1_ref, xq_ref, xk_ref, xv_ref, s_ref, *, pairs):
    _cast_passengers((c0_ref, c1_ref), (co0_ref, co1_ref))
    gi = pl.program_id(0)
    tb = q_ref.shape[0]
    hd = GDN_HEAD_DIM
    ck = GDN_CHUNK
    nck = tb // ck
    n_sq = int(math.log2(ck)) - 1

    @pl.when(pl.program_id(1) == 0)
    def _():
        for xe_ref in (xq_ref, xk_ref, xv_ref):
            xe_ref[0:SUBLANES, :] = jnp.zeros((SUBLANES, xe_ref.shape[1]), F32)
        s_ref[...] = jnp.zeros_like(s_ref)

    q_all = _conv_silu(q_ref, xq_ref, wq_ref)
    k_all = _conv_silu(k_ref, xk_ref, wk_ref)
    v_all = _conv_silu(v_ref, xv_ref, wv_ref)

    head0 = 2 * pairs * gi
    gt = pltpu.roll(gt_ref[...], lax.rem(LANES - head0, LANES), axis=1)
    r_idx = lax.broadcasted_iota(jnp.int32, (tb, tb), 0)
    c_idx = lax.broadcasted_iota(jnp.int32, (tb, tb), 1)
    same = (r_idx // ck) == (c_idx // ck)
    causal = same & (r_idx >= c_idx)
    strict = same & (r_idx > c_idx)
    same_b = jnp.where(same, 1.0, 0.0).astype(BF16)
    nw = nw_ref[...]

    def block_diag(packed):
        return jnp.concatenate([packed.astype(BF16)] * nck, axis=0) * same_b

    heads = range(2 * pairs)
    qs, ks, kks, qks = [], [], [], []
    for pi in range(pairs):
        q = q_all[:, pi * hd:(pi + 1) * hd]
        k = k_all[:, pi * hd:(pi + 1) * hd]
        q = q * (lax.rsqrt(jnp.sum(q * q, axis=-1, keepdims=True) + NORM_EPS) * hd ** -0.5)
        k = k * lax.rsqrt(jnp.sum(k * k, axis=-1, keepdims=True) + NORM_EPS)
        kb = k.astype(BF16)
        qs.append(q)
        ks.append(k)
        kks.append(lax.dot_general(kb, kb, NT_DIMS, preferred_element_type=F32))
        qks.append(lax.dot_general(q.astype(BF16), kb, NT_DIMS, preferred_element_type=F32))

    bcs, gccs, attns, ps = [], [], [], []
    for h in heads:
        bc = gt[:, h:h + 1]
        gcc = gt[:, GDN_V_HEADS + h:GDN_V_HEADS + h + 1]
        gcr = gh_ref[pl.ds(GDN_V_HEADS + head0 + h, 1), :]
        decay = jnp.exp(jnp.where(causal, gcc - gcr, -jnp.inf))
        a_full = jnp.where(strict, kks[h // 2] * decay * bc, 0.0)
        attn = qks[h // 2] * decay
        attns.append([attn[c * ck:(c + 1) * ck, c * ck:(c + 1) * ck].astype(BF16)
                      for c in range(nck)])
        p = -a_full[0:ck]
        for c in range(1, nck):
            p = p - a_full[c * ck:(c + 1) * ck]
        bcs.append(bc)
        gccs.append(gcc)
        ps.append(p)

    rs = list(ps)
    p_bds = [block_diag(p) for p in ps]
    for _ in range(n_sq):
        ps = [jnp.dot(ps[h].astype(BF16), p_bds[h], preferred_element_type=F32) for h in heads]
        p_bds = [block_diag(p) for p in ps]
        rs = [rs[h] + ps[h] + jnp.dot(rs[h].astype(BF16), p_bds[h], preferred_element_type=F32)
              for h in heads]

    egs = [jnp.exp(gcc) for gcc in gccs]
    us, wqs = [], []
    for h in heads:
        k = ks[h // 2]
        rhs = bcs[h] * jnp.concatenate([v_all[:, h * hd:(h + 1) * hd], k * egs[h]], axis=1)
        uw = rhs + jnp.dot(block_diag(rs[h]), rhs.astype(BF16), preferred_element_type=F32)
        us.append(uw[:, :hd])
        q_dec = qs[h // 2] * egs[h]
        wqs.append([jnp.concatenate([uw[c * ck:(c + 1) * ck, hd:], q_dec[c * ck:(c + 1) * ck]],
                                    axis=0).astype(BF16) for c in range(nck)])

    states = [s_ref[h] for h in heads]
    outs = [[] for _ in heads]
    for ci in range(nck):
        rows = slice(ci * ck, (ci + 1) * ck)
        wq_s = [jnp.dot(wqs[h][ci], states[h].astype(BF16), preferred_element_type=F32)
                for h in heads]
        v_news = [(us[h][rows] - wq_s[h][:ck]).astype(BF16) for h in heads]
        for h in heads:
            outs[h].append(wq_s[h][ck:] + jnp.dot(attns[h][ci], v_news[h],
                                                  preferred_element_type=F32))
        for h in heads:
            g_last = gccs[h][(ci + 1) * ck - 1:(ci + 1) * ck, :]
            k_dec = (ks[h // 2][rows] * jnp.exp(g_last - gccs[h][rows])).astype(BF16)
            states[h] = states[h] * jnp.exp(g_last) + lax.dot_general(
                k_dec, v_news[h], TN_DIMS, preferred_element_type=F32)

    for h in heads:
        s_ref[h] = states[h]
        o = jnp.concatenate(outs[h], axis=0)
        on = o * lax.rsqrt(jnp.mean(o * o, axis=-1, keepdims=True) + NORM_EPS) * nw
        cols = slice(h * hd, (h + 1) * hd)
        o_ref[:, cols] = (on * _silu(z_ref[:, cols])).astype(o_ref.dtype)


def gdn_core(proj, conv_w, gt, gh, norm_w, cast_srcs, cast_layer, tb=256, pairs=GDN_PAIRS):
    l = proj.shape[0]
    hd = GDN_HEAD_DIM
    tb = min(tb, l)
    ng = GDN_QK_HEADS // pairs
    nb = l // tb
    qw, vw = pairs * hd, 2 * pairs * hd
    c_in, c_out, c_shape = _cast_specs(cast_srcs, cast_layer, ng * nb, lambda g, b: g * nb + b)
    return pl.pallas_call(
        functools.partial(_gdn_kernel, pairs=pairs),
        grid=(ng, nb),
        in_specs=[
            pl.BlockSpec((tb, qw), lambda g, b: (b, g)),
            pl.BlockSpec((tb, qw), lambda g, b: (b, ng + g)),
            pl.BlockSpec((tb, vw), lambda g, b: (b, ng + g)),
            pl.BlockSpec((tb, vw), lambda g, b: (b, 2 * ng + g)),
            pl.BlockSpec((CONV_K, qw), lambda g, b: (0, g)),
            pl.BlockSpec((CONV_K, qw), lambda g, b: (0, ng + g)),
            pl.BlockSpec((CONV_K, vw), lambda g, b: (0, ng + g)),
            pl.BlockSpec((tb, LANES), lambda g, b: (b, 0)),
            pl.BlockSpec((LANES, tb), lambda g, b: (0, b)),
            pl.BlockSpec((1, hd), lambda g, b: (0, 0)),
        ] + c_in,
        out_specs=[pl.BlockSpec((tb, vw), lambda g, b: (b, g))] + c_out,
        out_shape=[jax.ShapeDtypeStruct((l, GDN_V_HEADS * hd), BF16)] + c_shape,
        scratch_shapes=[pltpu.VMEM((tb + SUBLANES, qw), F32),
                        pltpu.VMEM((tb + SUBLANES, qw), F32),
                        pltpu.VMEM((tb + SUBLANES, vw), F32),
                        pltpu.VMEM((2 * pairs, hd, hd), F32)],
        compiler_params=_cparams(("arbitrary", "arbitrary")),
        name="gdn_core",
    )(proj, proj, proj, proj, conv_w, conv_w, conv_w, gt, gh, norm_w, *cast_srcs)


def _kv_prep_kernel(kv_ref, fl_ref, knw_ref, fb_ref, kt_ref, v_ref, fh_ref, carry_ref):
    hd = FOX_HEAD_DIM
    nkv = FOX_KV_HEADS
    tm = kv_ref.shape[0]

    @pl.when(pl.program_id(0) == 0)
    def _():
        carry_ref[...] = jnp.zeros_like(carry_ref)

    for h in range(nkv):
        kh = kv_ref[:, h * hd:(h + 1) * hd]
        ms = jnp.mean(kh * kh, axis=-1, keepdims=True)
        kn = kh * lax.rsqrt(ms + NORM_EPS) * knw_ref[...]
        kt_ref[h * hd:(h + 1) * hd, :] = kn.T.astype(BF16)
    v_ref[...] = kv_ref[:, nkv * hd:2 * nkv * hd].astype(BF16)

    log_f = _log_sigmoid(fl_ref[...] + fb_ref[...])
    r = lax.broadcasted_iota(jnp.int32, (tm, tm), 0)
    c = lax.broadcasted_iota(jnp.int32, (tm, tm), 1)
    tril = jnp.where(r >= c, 1.0, 0.0).astype(F32)
    cs = jnp.dot(tril, log_f, preferred_element_type=F32,
                 precision=lax.Precision.HIGHEST) + carry_ref[...]
    carry_ref[...] = cs[tm - 1:tm, :]
    fh_ref[...] = (cs * LOG2E).T[0:fh_ref.shape[0], :]


def kv_prep(kv, fl, knw, fb_row, tm=512):
    l = kv.shape[0]
    tm = min(tm, l)
    kvd = FOX_KV_HEADS * FOX_HEAD_DIM
    nh = FOX_KV_HEADS * FOX_GROUP
    row = lambda i: (0, 0)
    return pl.pallas_call(
        _kv_prep_kernel,
        grid=(l // tm,),
        in_specs=[pl.BlockSpec((tm, 2 * kvd), lambda i: (i, 0)),
                  pl.BlockSpec((tm, LANES), lambda i: (i, 0)),
                  pl.BlockSpec((1, FOX_HEAD_DIM), row), pl.BlockSpec((1, LANES), row)],
        out_specs=[pl.BlockSpec((kvd, tm), lambda i: (0, i)),
                   pl.BlockSpec((tm, kvd), lambda i: (i, 0)),
                   pl.BlockSpec((nh, tm), lambda i: (0, i))],
        out_shape=[jax.ShapeDtypeStruct((kvd, l), BF16), jax.ShapeDtypeStruct((l, kvd), BF16),
                   jax.ShapeDtypeStruct((nh, l), F32)],
        scratch_shapes=[pltpu.VMEM((1, LANES), F32)],
        compiler_params=_cparams(("arbitrary",)),
        name="kv_prep",
    )(kv, fl, knw, fb_row)


def _fox_kernel(q_ref, gate_ref, kt_ref, v_ref, fh_ref, qnw_ref, c0_ref, c1_ref, o_ref, co0_ref,
                co1_ref, qn_ref, m_ref, l_ref, acc_ref, *, tk):
    _cast_passengers((c0_ref, c1_ref), (co0_ref, co1_ref))
    i = pl.program_id(1)
    tq = q_ref.shape[0]
    hd = FOX_HEAD_DIM
    scale = hd ** -0.5 * LOG2E
    for g in range(FOX_GROUP):
        qh = q_ref[:, g * hd:(g + 1) * hd]
        ms = jnp.mean(qh * qh, axis=-1, keepdims=True)
        qn_ref[g] = (qh * (lax.rsqrt(ms + NORM_EPS) * scale) * qnw_ref[...]).astype(BF16)
    m_ref[...] = jnp.full(m_ref.shape, -jnp.inf, F32)
    l_ref[...] = jnp.zeros_like(l_ref)
    acc_ref[...] = jnp.zeros_like(acc_ref)
    tkh = tk // 2
    n_full = (i * tq + 1) // tk
    diag0 = n_full * tk
    extra = (i * tq + 1 - diag0) // tkh
    diag_start = diag0 + extra * tkh

    def tile(start, width, mask_offset=None):
        start = pl.multiple_of(start, width)
        kj = kt_ref[:, pl.ds(start, width)]
        vj = v_ref[pl.ds(start, width), :]
        heads = range(FOX_GROUP)
        ss = [jnp.dot(qn_ref[g], kj, preferred_element_type=F32)
              - fh_ref[g:g + 1, pl.ds(start, width)] for g in heads]
        if mask_offset is not None:
            causal = (lax.broadcasted_iota(jnp.int32, (tq, width), 1)
                      - lax.broadcasted_iota(jnp.int32, (tq, width), 0)) <= mask_offset
            ss = [jnp.where(causal, s, -jnp.inf) for s in ss]
        sc = [[s[:, c * LANES:(c + 1) * LANES] for c in range(width // LANES)] for s in ss]
        m_olds = [m_ref[g] for g in heads]
        m_news = [jnp.maximum(m_olds[g], jnp.max(functools.reduce(jnp.maximum, sc[g]),
                                                 axis=-1, keepdims=True)) for g in heads]
        alphas = [jnp.exp2(m_olds[g] - m_news[g]) for g in heads]
        pc = [[jnp.exp2(s - m_news[g]) for s in sc[g]] for g in heads]
        for g in heads:
            row_sum = jnp.sum(functools.reduce(jnp.add, pc[g]), axis=-1, keepdims=True)
            l_ref[g] = alphas[g] * l_ref[g] + row_sum
            m_ref[g] = m_news[g]
        pvs = [jnp.dot(jnp.concatenate(pc[g], axis=1).astype(BF16), vj,
                       preferred_element_type=F32) for g in heads]
        for g in heads:
            for c in range(hd // LANES):
                cols = slice(c * LANES, (c + 1) * LANES)
                acc_ref[g, :, cols] = alphas[g] * acc_ref[g, :, cols] + pvs[g][:, cols]

    def body(j, carry):
        tile(j * tk, tk)
        return carry

    lax.fori_loop(0, n_full, body, 0)

    @pl.when(extra > 0)
    def _():
        tile(diag0, tkh)

    tile(diag_start, tkh, mask_offset=i * tq - diag_start)
    for g in range(FOX_GROUP):
        inv_l = 1.0 / l_ref[g]
        for c in range(hd // LANES):
            cols = slice(g * hd + c * LANES, g * hd + (c + 1) * LANES)
            o_ref[:, cols] = (acc_ref[g, :, c * LANES:(c + 1) * LANES] * inv_l
                              * jax.nn.sigmoid(gate_ref[:, cols])).astype(o_ref.dtype)


def fox_attention(qg, kt, v, fh, qnw, cast_srcs, cast_layer, tq=256, tk=1024):
    l = qg.shape[0]
    hd = FOX_HEAD_DIM
    gw = FOX_GROUP * hd
    tk = min(tk, l)
    tq = min(tq, tk // 2)
    assert tk % (2 * tq) == 0 and l % tk == 0
    nq = l // tq
    c_in, c_out, c_shape = _cast_specs(cast_srcs, cast_layer, FOX_KV_HEADS * nq,
                                       lambda h, i: h * nq + i)
    return pl.pallas_call(
        functools.partial(_fox_kernel, tk=tk),
        grid=(FOX_KV_HEADS, l // tq),
        in_specs=[
            pl.BlockSpec((tq, gw), lambda h, i: (i, h)),
            pl.BlockSpec((tq, gw), lambda h, i: (i, FOX_KV_HEADS + h)),
            pl.BlockSpec((hd, l), lambda h, i: (h, 0), pipeline_mode=pl.Buffered(1)),
            pl.BlockSpec((l, hd), lambda h, i: (0, h), pipeline_mode=pl.Buffered(1)),
            pl.BlockSpec((FOX_GROUP, l), lambda h, i: (h, 0)),
            pl.BlockSpec((1, hd), lambda h, i: (0, 0)),
        ] + c_in,
        out_specs=[pl.BlockSpec((tq, gw), lambda h, i: (i, h))] + c_out,
        out_shape=[jax.ShapeDtypeStruct((l, FOX_KV_HEADS * gw), BF16)] + c_shape,
        scratch_shapes=[pltpu.VMEM((FOX_GROUP, tq, hd), BF16),
                        pltpu.VMEM((FOX_GROUP, tq, LANES), F32),
                        pltpu.VMEM((FOX_GROUP, tq, LANES), F32),
                        pltpu.VMEM((FOX_GROUP, tq, hd), F32)],
        compiler_params=_cparams(("arbitrary", "arbitrary")),
        name="fox_attention",
    )(qg, qg, kt, v, fh, qnw, *cast_srcs)


def kernel(x, c, ada_w, ada_b, norm_mix, norm_ffn, ffn_w_in, ffn_w_out, gdn_w_in, gdn_conv,
           gdn_a_log, gdn_dt_bias, gdn_norm, gdn_w_out, kv_ada_w, kv_ada_b, kv_norm, kv_w, k_norm,
           forget_b, fox_w_in, q_norm, fox_w_out, out_ada_w, out_ada_b, out_norm):
    bsz, l, d = x.shape
    assert bsz == 1 and ada_w.shape[0] == 2 and gdn_w_in.shape[0] == 1 and fox_w_in.shape[0] == 1
    xs = x.reshape(l, d)
    c_col = c.reshape(d, 1)
    row = lambda t: t.reshape(1, -1)

    def mods(w3, b3, layer, n):
        m = adaln(c_col, w3, b3, layer)
        return [m[:, i * d:(i + 1) * d] for i in range(n)]

    ada_b3 = ada_b[:, None, :]
    conv_dim = gdn_conv.shape[2]
    gdn_main = conv_dim + GDN_V_HEADS * GDN_HEAD_DIM
    kvd = FOX_KV_HEADS * FOX_HEAD_DIM

    sh_m, sc_m, g_m, sh_f, sc_f, g_f = mods(ada_w, ada_b3, 0, 6)
    ffn_srcs = (ffn_w_in.reshape(-1, ffn_w_in.shape[2]), ffn_w_out.reshape(-1, d))
    proj, small, gdn_w_out_b = nm_proj_t(
        xs, row(norm_mix[0]), sh_m, sc_m, jnp.swapaxes(gdn_w_in, 1, 2), 0, gdn_main,
        gdn_w_in.shape[2] - gdn_main, cast_src=gdn_w_out.reshape(-1, d))
    pad32 = lambda t: jnp.pad(t, (GDN_V_HEADS, LANES - 2 * GDN_V_HEADS)).reshape(1, LANES)
    gt, gh = gdn_gates(small, pad32(gdn_a_log[0]), pad32(gdn_dt_bias[0]))
    o, w_in_b, w_out_b = gdn_core(proj, gdn_conv[0], gt, gh, row(gdn_norm[0]), ffn_srcs, 0)
    xs = proj_res(o, gdn_w_out_b[None], 0, xs, g_m)
    xs, fox_w_in_b = ffn(xs, row(norm_ffn[0]), sh_f, sc_f, g_f, w_in_b, w_out_b,
                         cast_src=fox_w_in.reshape(d, -1))

    sh_k, sc_k = mods(kv_ada_w[None], kv_ada_b[None, None, :], 0, 2)
    kvp, fl = nm_proj_t(xs, row(kv_norm), sh_k, sc_k, kv_w.T[None], 0, 2 * kvd,
                        kv_w.shape[1] - 2 * kvd)
    fb_row = jnp.pad(forget_b, (0, LANES - forget_b.shape[0])).reshape(1, LANES)
    k_sh, v_sh, fh = kv_prep(kvp, fl, row(k_norm), fb_row)

    sh_m, sc_m, g_m, sh_f, sc_f, g_f = mods(ada_w, ada_b3, 1, 6)
    qg, fox_w_out_b = nm_proj(xs, row(norm_mix[1]), sh_m, sc_m, fox_w_in_b[None], 0,
                              fox_w_out.reshape(-1, d))
    a, w_in_b, w_out_b = fox_attention(qg, k_sh, v_sh, fh, row(q_norm[0]), ffn_srcs, 1)
    xs = proj_res(a, fox_w_out_b[None], 0, xs, g_m)
    sh_o, sc_o = mods(out_ada_w[None], out_ada_b[None, None, :], 0, 2)
    xs = ffn(xs, row(norm_ffn[1]), sh_f, sc_f, g_f, w_in_b, w_out_b,
             final_mod=(row(out_norm), sh_o, sc_o))
    return xs.reshape(bsz, l, d)
```

```python
import functools
import math

import jax
import jax.numpy as jnp
from jax import lax
from jax.experimental import pallas as pl
from jax.experimental.pallas import tpu as pltpu

F32 = jnp.float32
BF16 = jnp.bfloat16
NORM_EPS = 1e-6
LOG2E = math.log2(math.e)

V7X_VMEM_BYTES = 64 * 1024 * 1024
VMEM_LIMIT_BYTES = V7X_VMEM_BYTES * 7 // 8
LANES = 128
SUBLANES = 8

GDN_CHUNK = 64
GDN_HEAD_DIM = 128
GDN_V_HEADS = 32
GDN_QK_HEADS = 16
CONV_K = 4
FOX_HEAD_DIM = 256
FOX_KV_HEADS = 2
FOX_GROUP = 8

NT_DIMS = (((1,), (1,)), ((), ()))
TN_DIMS = (((0,), (0,)), ((), ()))


def _cparams(sem):
    return pltpu.CompilerParams(dimension_semantics=sem, vmem_limit_bytes=VMEM_LIMIT_BYTES)


def _silu(x):
    h = 0.5 * x
    return h + h * jnp.tanh(h)


def _softplus(x):
    return jnp.maximum(x, 0.0) + jnp.log1p(jnp.exp(-jnp.abs(x)))


def _log_sigmoid(x):
    return jnp.minimum(x, 0.0) - jnp.log1p(jnp.exp(-jnp.abs(x)))


def _norm_mod(x, nw, sh, sc):
    ms = jnp.mean(x * x, axis=-1, keepdims=True)
    return x * lax.rsqrt(ms + NORM_EPS) * (nw * (1.0 + sc)) + sh


NORM_ROWS = 256


def _norm_mod_rows(dst_ref, src_fn, nw, sh, sc):
    tm = dst_ref.shape[0]
    step = min(NORM_ROWS, tm)
    for r in range(0, tm, step):
        rows = slice(r, r + step)
        dst_ref[rows, :] = _norm_mod(src_fn(rows), nw, sh, sc).astype(dst_ref.dtype)


def _adaln_kernel(c_ref, w_ref, b_ref, o_ref):
    cond = _silu(c_ref[...])
    o_ref[...] = jnp.sum(w_ref[...] * cond, axis=0, keepdims=True) + b_ref[...]


def adaln(c_col, w3, b3, layer, tn=1024):
    _, d, n = w3.shape
    tn = min(tn, n)
    return pl.pallas_call(
        _adaln_kernel,
        grid=(n // tn,),
        in_specs=[
            pl.BlockSpec((d, 1), lambda j: (0, 0)),
            pl.BlockSpec((None, d, tn), lambda j: (layer, 0, j)),
            pl.BlockSpec((None, 1, tn), lambda j: (layer, 0, j)),
        ],
        out_specs=pl.BlockSpec((1, tn), lambda j: (0, j)),
        out_shape=jax.ShapeDtypeStruct((1, n), F32),
        compiler_params=_cparams(("arbitrary",)),
        name="adaln",
    )(c_col, w3, b3)


def _nm_proj_kernel(x_ref, nw_ref, sh_ref, sc_ref, w_ref, c_ref, o_ref, co_ref, h_ref):
    _cast_passengers((c_ref,), (co_ref,))

    @pl.when(pl.program_id(1) == 0)
    def _():
        _norm_mod_rows(h_ref, lambda rows: x_ref[rows, :], nw_ref[...], sh_ref[...], sc_ref[...])

    o_ref[...] = jnp.dot(h_ref[...], w_ref[...], preferred_element_type=F32)


def nm_proj(x, nw, sh, sc, w3, layer, cast_src, tm=1024, tn=1024):
    l, d = x.shape
    n = w3.shape[2]
    tm, tn = min(tm, l), min(tn, n)
    nj = n // tn
    row = lambda i, j: (0, 0)
    c_in, c_out, c_shape = _cast_specs([cast_src], 0, (l // tm) * nj, lambda i, j: i * nj + j, 1)
    return pl.pallas_call(
        _nm_proj_kernel,
        grid=(l // tm, nj),
        in_specs=[
            pl.BlockSpec((tm, d), lambda i, j: (i, 0)),
            pl.BlockSpec((1, d), row), pl.BlockSpec((1, d), row), pl.BlockSpec((1, d), row),
            pl.BlockSpec((None, d, tn), lambda i, j: (layer, 0, j)),
        ] + c_in,
        out_specs=[pl.BlockSpec((tm, tn), lambda i, j: (i, j))] + c_out,
        out_shape=[jax.ShapeDtypeStruct((l, n), F32)] + c_shape,
        scratch_shapes=[pltpu.VMEM((tm, d), BF16)],
        compiler_params=_cparams(("arbitrary", "arbitrary")),
        name="nm_proj",
    )(x, nw, sh, sc, w3, cast_src)


def _nm_proj_t_kernel(x_ref, nw_ref, sh_ref, sc_ref, wt_ref, wst_ref, *rest, cast):
    if cast:
        c_ref, o_ref, os_ref, co_ref, h_ref = rest
        _cast_passengers((c_ref,), (co_ref,))
    else:
        o_ref, os_ref, h_ref = rest
    tm = o_ref.shape[0]

    @pl.when(pl.program_id(1) == 0)
    def _():
        _norm_mod_rows(h_ref, lambda rows: x_ref[rows, :], nw_ref[...], sh_ref[...], sc_ref[...])
        ns = wst_ref.shape[0]
        os_ref[:, 0:ns] = lax.dot_general(h_ref[...], wst_ref[...].astype(BF16), NT_DIMS,
                                          preferred_element_type=F32)
        os_ref[:, ns:] = jnp.zeros((tm, LANES - ns), F32)

    o_ref[...] = lax.dot_general(h_ref[...], wt_ref[...].astype(BF16), NT_DIMS,
                                 preferred_element_type=F32)


def nm_proj_t(x, nw, sh, sc, wt3, layer, n, ns, cast_src=None, tm=1024, tn=1024):
    l, d = x.shape
    tm, tn = min(tm, l), min(tn, n)
    assert n % tn == 0 and n % ns == 0
    nj = n // tn
    cast = cast_src is not None
    row = lambda i, j: (0, 0)
    in_specs = [
        pl.BlockSpec((tm, d), lambda i, j: (i, 0)),
        pl.BlockSpec((1, d), row), pl.BlockSpec((1, d), row), pl.BlockSpec((1, d), row),
        pl.BlockSpec((None, tn, d), lambda i, j: (layer, j, 0)),
        pl.BlockSpec((None, ns, d), lambda i, j: (layer, n // ns, 0)),
    ]
    out_specs = [pl.BlockSpec((tm, tn), lambda i, j: (i, j)),
                 pl.BlockSpec((tm, LANES), lambda i, j: (i, 0))]
    out_shape = [jax.ShapeDtypeStruct((l, n), F32), jax.ShapeDtypeStruct((l, LANES), F32)]
    args = [x, nw, sh, sc, wt3, wt3]
    if cast:
        c_in, c_out, c_shape = _cast_specs([cast_src], 0, (l // tm) * nj,
                                           lambda i, j: i * nj + j, 1)
        in_specs, out_specs, out_shape = in_specs + c_in, out_specs + c_out, out_shape + c_shape
        args.append(cast_src)
    return pl.pallas_call(
        functools.partial(_nm_proj_t_kernel, cast=cast),
        grid=(l // tm, nj),
        in_specs=in_specs,
        out_specs=out_specs,
        out_shape=out_shape,
        scratch_shapes=[pltpu.VMEM((tm, d), BF16)],
        compiler_params=_cparams(("arbitrary", "arbitrary")),
        name="nm_proj_t",
    )(*args)


def _proj_res_kernel(a_ref, w_ref, x_ref, g_ref, o_ref):
    y = jnp.dot(a_ref[...], w_ref[...], preferred_element_type=F32)
    o_ref[...] = x_ref[...] + g_ref[...] * y


def proj_res(a, w3, layer, x, g, tm=1024, tn=1024):
    l, k = a.shape
    d = w3.shape[2]
    tm, tn = min(tm, l), min(tn, d)
    return pl.pallas_call(
        _proj_res_kernel,
        grid=(l // tm, d // tn),
        in_specs=[
            pl.BlockSpec((tm, k), lambda i, j: (i, 0)),
            pl.BlockSpec((None, k, tn), lambda i, j: (layer, 0, j)),
            pl.BlockSpec((tm, tn), lambda i, j: (i, j)),
            pl.BlockSpec((1, tn), lambda i, j: (0, j)),
        ],
        out_specs=pl.BlockSpec((tm, tn), lambda i, j: (i, j)),
        out_shape=jax.ShapeDtypeStruct((l, d), F32),
        compiler_params=_cparams(("arbitrary", "arbitrary")),
        name="proj_res",
    )(a, w3, x, g)


def _ffn_kernel(x_ref, nw_ref, sh_ref, sc_ref, g_ref, wg_ref, wu_ref, wo_ref, *rest, final, cast):
    rest = list(rest)
    fnw_ref, fsh_ref, fsc_ref = (rest.pop(0), rest.pop(0), rest.pop(0)) if final else (None,) * 3
    c_ref = rest.pop(0) if cast else None
    o_ref = rest.pop(0)
    co_ref = rest.pop(0) if cast else None
    h_ref, acc_ref = rest
    if cast:
        _cast_passengers((c_ref,), (co_ref,))
    j = pl.program_id(1)

    @pl.when(j == 0)
    def _():
        _norm_mod_rows(h_ref, lambda rows: x_ref[rows, :], nw_ref[...], sh_ref[...], sc_ref[...])
        acc_ref[...] = jnp.zeros_like(acc_ref)

    h = h_ref[...]
    gate = jnp.dot(h, wg_ref[...], preferred_element_type=F32)
    up = jnp.dot(h, wu_ref[...], preferred_element_type=F32)
    act = (_silu(gate) * up).astype(BF16)
    acc_ref[...] += jnp.dot(act, wo_ref[...], preferred_element_type=F32)

    @pl.when(j == pl.num_programs(1) - 1)
    def _():
        res = lambda rows: x_ref[rows, :] + g_ref[...] * acc_ref[rows, :]
        if final:
            _norm_mod_rows(o_ref, res, fnw_ref[...], fsh_ref[...], fsc_ref[...])
        else:
            o_ref[...] = res(slice(None))


def ffn(x, nw, sh, sc, g, w_in, w_out, final_mod=None, cast_src=None, tm=512, th=512):
    l, d = x.shape
    hdim = w_out.shape[0]
    tm, th = min(tm, l), min(th, hdim)
    nh = hdim // th
    final, cast = final_mod is not None, cast_src is not None
    row = lambda i, j: (0, 0)
    vec = pl.BlockSpec((1, d), row)
    in_specs = [
        pl.BlockSpec((tm, d), lambda i, j: (i, 0)),
        vec, vec, vec, vec,
        pl.BlockSpec((d, th), lambda i, j: (0, j)),
        pl.BlockSpec((d, th), lambda i, j: (0, j + nh)),
        pl.BlockSpec((th, d), lambda i, j: (j, 0)),
    ]
    out_specs = [pl.BlockSpec((tm, d), lambda i, j: (i, 0))]
    out_shape = [jax.ShapeDtypeStruct((l, d), F32)]
    args = [x, nw, sh, sc, g, w_in, w_in, w_out]
    if final:
        in_specs += [vec, vec, vec]
        args += list(final_mod)
    if cast:
        c_in, c_out, c_shape = _cast_specs([cast_src], 0, (l // tm) * nh,
                                           lambda i, j: i * nh + j, 1)
        in_specs, out_specs, out_shape = in_specs + c_in, out_specs + c_out, out_shape + c_shape
        args.append(cast_src)
    out = pl.pallas_call(
        functools.partial(_ffn_kernel, final=final, cast=cast),
        grid=(l // tm, nh),
        in_specs=in_specs,
        out_specs=out_specs,
        out_shape=out_shape,
        scratch_shapes=[pltpu.VMEM((tm, d), BF16), pltpu.VMEM((tm, d), F32)],
        compiler_params=_cparams(("arbitrary", "arbitrary")),
        name="ffn",
    )(*args)
    return out if cast else out[0]


def _gdn_gates_kernel(s_ref, alog_ref, dt_ref, gt_ref, gh_ref):
    x = s_ref[...]
    tm = x.shape[0]
    lane = lax.broadcasted_iota(jnp.int32, x.shape, 1)
    beta = jax.nn.sigmoid(x)
    g = -jnp.exp(alog_ref[...]) * _softplus(x + dt_ref[...])
    g = jnp.where((lane >= GDN_V_HEADS) & (lane < 2 * GDN_V_HEADS), g, 0.0)
    r = lax.broadcasted_iota(jnp.int32, (tm, tm), 0)
    c = lax.broadcasted_iota(jnp.int32, (tm, tm), 1)
    same_chunk = (r // GDN_CHUNK) == (c // GDN_CHUNK)
    tril = jnp.where((r >= c) & same_chunk, 1.0, 0.0).astype(F32)
    gcum = jnp.dot(tril, g, preferred_element_type=F32, precision=lax.Precision.HIGHEST)
    gt = jnp.where(lane < GDN_V_HEADS, beta, gcum)
    gt_ref[...] = gt
    gh_ref[...] = gt.T


def gdn_gates(small, alog_row, dt_row, tm=512):
    l = small.shape[0]
    tm = min(tm, l)
    row = lambda i: (0, 0)
    return pl.pallas_call(
        _gdn_gates_kernel,
        grid=(l // tm,),
        in_specs=[pl.BlockSpec((tm, LANES), lambda i: (i, 0)),
                  pl.BlockSpec((1, LANES), row), pl.BlockSpec((1, LANES), row)],
        out_specs=[pl.BlockSpec((tm, LANES), lambda i: (i, 0)),
                   pl.BlockSpec((LANES, tm), lambda i: (0, i))],
        out_shape=[jax.ShapeDtypeStruct((l, LANES), F32), jax.ShapeDtypeStruct((LANES, l), F32)],
        compiler_params=_cparams(("arbitrary",)),
        name="gdn_gates",
    )(small, alog_row, dt_row)


GDN_PAIRS = 8


def _conv_silu(x_ref, xe_ref, w_ref):
    tb = x_ref.shape[0]
    pad = SUBLANES
    xe_ref[pad:pad + tb, :] = x_ref[...]
    w = w_ref[...]
    acc = w[CONV_K - 1:CONV_K, :] * xe_ref[pad:pad + tb, :]
    for i in range(CONV_K - 1):
        off = pad - (CONV_K - 1) + i
        acc = acc + w[i:i + 1, :] * xe_ref[off:off + tb, :]
    xe_ref[0:pad, :] = xe_ref[tb:tb + pad, :]
    return _silu(acc)


def _cast_passengers(in_refs, out_refs):
    for src, dst in zip(in_refs, out_refs):
        dst[...] = src[...].astype(dst.dtype)


def _cast_specs(arrays, layer, n_steps, step_of, n_layers=2):
    in_specs, out_specs, out_shape = [], [], []
    bf16_rows = 16
    for a in arrays:
        rows = a.shape[0] // n_layers
        assert a.shape[0] % n_layers == 0 and rows % bf16_rows == 0
        n_blk = max(n for n in range(1, n_steps + 1) if (rows // bf16_rows) % n == 0)
        blk = rows // n_blk
        blk_of = lambda *g, _n=n_blk: jnp.minimum(step_of(*g), _n - 1)
        in_specs.append(pl.BlockSpec((blk, a.shape[1]),
                                     lambda *g, _n=n_blk, _b=blk_of: (layer * _n + _b(*g), 0)))
        out_specs.append(pl.BlockSpec((blk, a.shape[1]), lambda *g, _b=blk_of: (_b(*g), 0)))
        out_shape.append(jax.ShapeDtypeStruct((rows, a.shape[1]), BF16))
    return in_specs, out_specs, out_shape


def _gdn_kernel(q_ref, k_ref, v_ref, z_ref, wq_ref, wk_ref, wv_ref, gt_ref, gh_ref, nw_ref,
                c0_ref, c1_ref, o_ref, co0_ref, co1_ref, xq_ref, xk_ref, xv_ref, s_ref, *, pairs):
    _cast_passengers((c0_ref, c1_ref), (co0_ref, co1_ref))
    gi = pl.program_id(0)
    tb = q_ref.shape[0]
    hd = GDN_HEAD_DIM
    ck = GDN_CHUNK
    nck = tb // ck
    n_sq = int(math.log2(ck)) - 1

    @pl.when(pl.program_id(1) == 0)
    def _():
        for xe_ref in (xq_ref, xk_ref, xv_ref):
            xe_ref[0:SUBLANES, :] = jnp.zeros((SUBLANES, xe_ref.shape[1]), F32)
        s_ref[...] = jnp.zeros_like(s_ref)

    q_all = _conv_silu(q_ref, xq_ref, wq_ref)
    k_all = _conv_silu(k_ref, xk_ref, wk_ref)
    v_all = _conv_silu(v_ref, xv_ref, wv_ref)

    head0 = 2 * pairs * gi
    gt = pltpu.roll(gt_ref[...], lax.rem(LANES - head0, LANES), axis=1)
    r_idx = lax.broadcasted_iota(jnp.int32, (tb, tb), 0)
    c_idx = lax.broadcasted_iota(jnp.int32, (tb, tb), 1)
    same_b = jnp.where((r_idx // ck) == (c_idx // ck), 1.0, 0.0).astype(BF16)
    lt = LANES
    r_l = lax.broadcasted_iota(jnp.int32, (lt, lt), 0)
    c_l = lax.broadcasted_iota(jnp.int32, (lt, lt), 1)
    same_l = (r_l // ck) == (c_l // ck)
    causal = same_l & (r_l >= c_l)
    strict = same_l & (r_l > c_l)
    tiles = [slice(t * lt, (t + 1) * lt) for t in range(tb // lt)]
    nw = nw_ref[...]

    def block_diag(packed):
        return jnp.concatenate([packed.astype(BF16)] * nck, axis=0) * same_b

    heads = range(2 * pairs)
    qs, ks, kks, qks = [], [], [], []
    for pi in range(pairs):
        q = q_all[:, pi * hd:(pi + 1) * hd]
        k = k_all[:, pi * hd:(pi + 1) * hd]
        q = q * (lax.rsqrt(jnp.sum(q * q, axis=-1, keepdims=True) + NORM_EPS) * hd ** -0.5)
        k = k * lax.rsqrt(jnp.sum(k * k, axis=-1, keepdims=True) + NORM_EPS)
        kb = k.astype(BF16)
        qb = q.astype(BF16)
        qs.append(q)
        ks.append(k)
        kks.append([lax.dot_general(kb[t], kb[t], NT_DIMS, preferred_element_type=F32)
                    for t in tiles])
        qks.append([lax.dot_general(qb[t], kb[t], NT_DIMS, preferred_element_type=F32)
                    for t in tiles])

    bcs, gccs, attns, ps = [], [], [], []
    for h in heads:
        bc = gt[:, h:h + 1]
        gcc = gt[:, GDN_V_HEADS + h:GDN_V_HEADS + h + 1]
        gcr = gh_ref[pl.ds(GDN_V_HEADS + head0 + h, 1), :]
        attn_chunks, p_parts = [], []
        for ti, t in enumerate(tiles):
            decay = jnp.exp(jnp.where(causal, gcc[t] - gcr[:, t], -jnp.inf))
            a_t = jnp.where(strict, kks[h // 2][ti] * decay * bc[t], 0.0)
            attn_t = qks[h // 2][ti] * decay
            part = -a_t[0:ck]
            for c in range(lt // ck):
                attn_chunks.append(attn_t[c * ck:(c + 1) * ck, c * ck:(c + 1) * ck].astype(BF16))
                if c:
                    part = part - a_t[c * ck:(c + 1) * ck]
            p_parts.append(part)
        attns.append(attn_chunks)
        bcs.append(bc)
        gccs.append(gcc)
        ps.append(jnp.concatenate(p_parts, axis=1))

    rs = list(ps)
    p_bds = [block_diag(p) for p in ps]
    for _ in range(n_sq):
        ps = [jnp.dot(ps[h].astype(BF16), p_bds[h], preferred_element_type=F32) for h in heads]
        p_bds = [block_diag(p) for p in ps]
        rs = [rs[h] + ps[h] + jnp.dot(rs[h].astype(BF16), p_bds[h], preferred_element_type=F32)
              for h in heads]

    egs = [jnp.exp(gcc) for gcc in gccs]
    us, wqs = [], []
    for h in heads:
        k = ks[h // 2]
        rhs = bcs[h] * jnp.concatenate([v_all[:, h * hd:(h + 1) * hd], k * egs[h]], axis=1)
        uw = rhs + jnp.dot(block_diag(rs[h]), rhs.astype(BF16), preferred_element_type=F32)
        us.append(uw[:, :hd])
        q_dec = qs[h // 2] * egs[h]
        wqs.append([jnp.concatenate([uw[c * ck:(c + 1) * ck, hd:], q_dec[c * ck:(c + 1) * ck]],
                                    axis=0).astype(BF16) for c in range(nck)])

    states = [s_ref[h] for h in heads]
    outs = [[] for _ in heads]
    for ci in range(nck):
        rows = slice(ci * ck, (ci + 1) * ck)
        wq_s = [jnp.dot(wqs[h][ci], states[h].astype(BF16), preferred_element_type=F32)
                for h in heads]
        v_news = [(us[h][rows] - wq_s[h][:ck]).astype(BF16) for h in heads]
        for h in heads:
            outs[h].append(wq_s[h][ck:] + jnp.dot(attns[h][ci], v_news[h],
                                                  preferred_element_type=F32))
        for h in heads:
            g_last = gccs[h][(ci + 1) * ck - 1:(ci + 1) * ck, :]
            k_dec = (ks[h // 2][rows] * jnp.exp(g_last - gccs[h][rows])).astype(BF16)
            states[h] = states[h] * jnp.exp(g_last) + lax.dot_general(
                k_dec, v_news[h], TN_DIMS, preferred_element_type=F32)

    for h in heads:
        s_ref[h] = states[h]
        o = jnp.concatenate(outs[h], axis=0)
        on = o * lax.rsqrt(jnp.mean(o * o, axis=-1, keepdims=True) + NORM_EPS) * nw
        cols = slice(h * hd, (h + 1) * hd)
        o_ref[:, cols] = (on * _silu(z_ref[:, cols])).astype(o_ref.dtype)


def gdn_core(proj, conv_w, gt, gh, norm_w, cast_srcs, cast_layer, tb=256, pairs=GDN_PAIRS):
    l = proj.shape[0]
    hd = GDN_HEAD_DIM
    tb = min(tb, l)
    ng = GDN_QK_HEADS // pairs
    nb = l // tb
    qw, vw = pairs * hd, 2 * pairs * hd
    c_in, c_out, c_shape = _cast_specs(cast_srcs, cast_layer, ng * nb, lambda g, b: g * nb + b)
    return pl.pallas_call(
        functools.partial(_gdn_kernel, pairs=pairs),
        grid=(ng, nb),
        in_specs=[
            pl.BlockSpec((tb, qw), lambda g, b: (b, g)),
            pl.BlockSpec((tb, qw), lambda g, b: (b, ng + g)),
            pl.BlockSpec((tb, vw), lambda g, b: (b, ng + g)),
            pl.BlockSpec((tb, vw), lambda g, b: (b, 2 * ng + g)),
            pl.BlockSpec((CONV_K, qw), lambda g, b: (0, g)),
            pl.BlockSpec((CONV_K, qw), lambda g, b: (0, ng + g)),
            pl.BlockSpec((CONV_K, vw), lambda g, b: (0, ng + g)),
            pl.BlockSpec((tb, LANES), lambda g, b: (b, 0)),
            pl.BlockSpec((LANES, tb), lambda g, b: (0, b)),
            pl.BlockSpec((1, hd), lambda g, b: (0, 0)),
        ] + c_in,
        out_specs=[pl.BlockSpec((tb, vw), lambda g, b: (b, g))] + c_out,
        out_shape=[jax.ShapeDtypeStruct((l, GDN_V_HEADS * hd), BF16)] + c_shape,
        scratch_shapes=[pltpu.VMEM((tb + SUBLANES, qw), F32),
                        pltpu.VMEM((tb + SUBLANES, qw), F32),
                        pltpu.VMEM((tb + SUBLANES, vw), F32),
                        pltpu.VMEM((2 * pairs, hd, hd), F32)],
        compiler_params=_cparams(("arbitrary", "arbitrary")),
        name="gdn_core",
    )(proj, proj, proj, proj, conv_w, conv_w, conv_w, gt, gh, norm_w, *cast_srcs)


def _kv_prep_kernel(kv_ref, fl_ref, knw_ref, fb_ref, kt_ref, v_ref, fh_ref, carry_ref):
    hd = FOX_HEAD_DIM
    nkv = FOX_KV_HEADS
    tm = kv_ref.shape[0]

    @pl.when(pl.program_id(0) == 0)
    def _():
        carry_ref[...] = jnp.zeros_like(carry_ref)

    for h in range(nkv):
        kh = kv_ref[:, h * hd:(h + 1) * hd]
        ms = jnp.mean(kh * kh, axis=-1, keepdims=True)
        kn = kh * lax.rsqrt(ms + NORM_EPS) * knw_ref[...]
        kt_ref[h * hd:(h + 1) * hd, :] = kn.T.astype(BF16)
    v_ref[...] = kv_ref[:, nkv * hd:2 * nkv * hd].astype(BF16)

    log_f = _log_sigmoid(fl_ref[...] + fb_ref[...])
    r = lax.broadcasted_iota(jnp.int32, (tm, tm), 0)
    c = lax.broadcasted_iota(jnp.int32, (tm, tm), 1)
    tril = jnp.where(r >= c, 1.0, 0.0).astype(F32)
    cs = jnp.dot(tril, log_f, preferred_element_type=F32,
                 precision=lax.Precision.HIGHEST) + carry_ref[...]
    carry_ref[...] = cs[tm - 1:tm, :]
    fh_ref[...] = (cs * LOG2E).T[0:fh_ref.shape[0], :]


def kv_prep(kv, fl, knw, fb_row, tm=512):
    l = kv.shape[0]
    tm = min(tm, l)
    kvd = FOX_KV_HEADS * FOX_HEAD_DIM
    nh = FOX_KV_HEADS * FOX_GROUP
    row = lambda i: (0, 0)
    return pl.pallas_call(
        _kv_prep_kernel,
        grid=(l // tm,),
        in_specs=[pl.BlockSpec((tm, 2 * kvd), lambda i: (i, 0)),
                  pl.BlockSpec((tm, LANES), lambda i: (i, 0)),
                  pl.BlockSpec((1, FOX_HEAD_DIM), row), pl.BlockSpec((1, LANES), row)],
        out_specs=[pl.BlockSpec((kvd, tm), lambda i: (0, i)),
                   pl.BlockSpec((tm, kvd), lambda i: (i, 0)),
                   pl.BlockSpec((nh, tm), lambda i: (0, i))],
        out_shape=[jax.ShapeDtypeStruct((kvd, l), BF16), jax.ShapeDtypeStruct((l, kvd), BF16),
                   jax.ShapeDtypeStruct((nh, l), F32)],
        scratch_shapes=[pltpu.VMEM((1, LANES), F32)],
        compiler_params=_cparams(("arbitrary",)),
        name="kv_prep",
    )(kv, fl, knw, fb_row)


def _fox_kernel(q_ref, gate_ref, kt_ref, v_ref, fh_ref, qnw_ref, c0_ref, c1_ref, o_ref, co0_ref,
                co1_ref, qn_ref, m_ref, l_ref, acc_ref, *, tk):
    _cast_passengers((c0_ref, c1_ref), (co0_ref, co1_ref))
    i = pl.program_id(1)
    tq = q_ref.shape[0]
    hd = FOX_HEAD_DIM
    scale = hd ** -0.5 * LOG2E
    for g in range(FOX_GROUP):
        qh = q_ref[:, g * hd:(g + 1) * hd]
        ms = jnp.mean(qh * qh, axis=-1, keepdims=True)
        qn_ref[g] = (qh * (lax.rsqrt(ms + NORM_EPS) * scale) * qnw_ref[...]).astype(BF16)
    m_ref[...] = jnp.full(m_ref.shape, -jnp.inf, F32)
    l_ref[...] = jnp.zeros_like(l_ref)
    acc_ref[...] = jnp.zeros_like(acc_ref)
    tkh = tk // 2
    n_full = (i * tq + 1) // tk
    diag0 = n_full * tk
    extra = (i * tq + 1 - diag0) // tkh
    diag_start = diag0 + extra * tkh

    def tile(start, width, mask_offset=None):
        start = pl.multiple_of(start, width)
        kj = kt_ref[:, pl.ds(start, width)]
        vj = v_ref[pl.ds(start, width), :]
        heads = range(FOX_GROUP)
        ss = [jnp.dot(qn_ref[g], kj, preferred_element_type=F32)
              - fh_ref[g:g + 1, pl.ds(start, width)] for g in heads]
        if mask_offset is not None:
            causal = (lax.broadcasted_iota(jnp.int32, (tq, width), 1)
                      - lax.broadcasted_iota(jnp.int32, (tq, width), 0)) <= mask_offset
            ss = [jnp.where(causal, s, -jnp.inf) for s in ss]
        sc = [[s[:, c * LANES:(c + 1) * LANES] for c in range(width // LANES)] for s in ss]
        m_olds = [m_ref[g] for g in heads]
        m_news = [jnp.maximum(m_olds[g], jnp.max(functools.reduce(jnp.maximum, sc[g]),
                                                 axis=-1, keepdims=True)) for g in heads]
        alphas = [jnp.exp2(m_olds[g] - m_news[g]) for g in heads]
        pc = [[jnp.exp2(s - m_news[g]) for s in sc[g]] for g in heads]
        for g in heads:
            row_sum = jnp.sum(functools.reduce(jnp.add, pc[g]), axis=-1, keepdims=True)
            l_ref[g] = alphas[g] * l_ref[g] + row_sum
            m_ref[g] = m_news[g]
        pvs = [jnp.dot(jnp.concatenate(pc[g], axis=1).astype(BF16), vj,
                       preferred_element_type=F32) for g in heads]
        for g in heads:
            for c in range(hd // LANES):
                cols = slice(c * LANES, (c + 1) * LANES)
                acc_ref[g, :, cols] = alphas[g] * acc_ref[g, :, cols] + pvs[g][:, cols]

    def body(j, carry):
        tile(j * tk, tk)
        return carry

    lax.fori_loop(0, n_full, body, 0)

    @pl.when(extra > 0)
    def _():
        tile(diag0, tkh)

    tile(diag_start, tkh, mask_offset=i * tq - diag_start)
    for g in range(FOX_GROUP):
        half_inv_l = 0.5 / l_ref[g]
        for c in range(hd // LANES):
            cols = slice(g * hd + c * LANES, g * hd + (c + 1) * LANES)
            o_ref[:, cols] = (acc_ref[g, :, c * LANES:(c + 1) * LANES] * half_inv_l
                              * (1.0 + jnp.tanh(0.5 * gate_ref[:, cols]))).astype(o_ref.dtype)


def fox_attention(qg, kt, v, fh, qnw, cast_srcs, cast_layer, tq=256, tk=1024):
    l = qg.shape[0]
    hd = FOX_HEAD_DIM
    gw = FOX_GROUP * hd
    tk = min(tk, l)
    tq = min(tq, tk // 2)
    assert tk % (2 * tq) == 0 and l % tk == 0
    nq = l // tq
    c_in, c_out, c_shape = _cast_specs(cast_srcs, cast_layer, FOX_KV_HEADS * nq,
                                       lambda h, i: h * nq + i)
    return pl.pallas_call(
        functools.partial(_fox_kernel, tk=tk),
        grid=(FOX_KV_HEADS, l // tq),
        in_specs=[
            pl.BlockSpec((tq, gw), lambda h, i: (i, h)),
            pl.BlockSpec((tq, gw), lambda h, i: (i, FOX_KV_HEADS + h)),
            pl.BlockSpec((hd, l), lambda h, i: (h, 0), pipeline_mode=pl.Buffered(1)),
            pl.BlockSpec((l, hd), lambda h, i: (0, h), pipeline_mode=pl.Buffered(1)),
            pl.BlockSpec((FOX_GROUP, l), lambda h, i: (h, 0)),
            pl.BlockSpec((1, hd), lambda h, i: (0, 0)),
        ] + c_in,
        out_specs=[pl.BlockSpec((tq, gw), lambda h, i: (i, h))] + c_out,
        out_shape=[jax.ShapeDtypeStruct((l, FOX_KV_HEADS * gw), BF16)] + c_shape,
        scratch_shapes=[pltpu.VMEM((FOX_GROUP, tq, hd), BF16),
                        pltpu.VMEM((FOX_GROUP, tq, LANES), F32),
                        pltpu.VMEM((FOX_GROUP, tq, LANES), F32),
                        pltpu.VMEM((FOX_GROUP, tq, hd), F32)],
        compiler_params=_cparams(("arbitrary", "arbitrary")),
        name="fox_attention",
    )(qg, qg, kt, v, fh, qnw, *cast_srcs)


def kernel(x, c, ada_w, ada_b, norm_mix, norm_ffn, ffn_w_in, ffn_w_out, gdn_w_in, gdn_conv,
           gdn_a_log, gdn_dt_bias, gdn_norm, gdn_w_out, kv_ada_w, kv_ada_b, kv_norm, kv_w, k_norm,
           forget_b, fox_w_in, q_norm, fox_w_out, out_ada_w, out_ada_b, out_norm):
    bsz, l, d = x.shape
    assert bsz == 1 and ada_w.shape[0] == 2 and gdn_w_in.shape[0] == 1 and fox_w_in.shape[0] == 1
    xs = x.reshape(l, d)
    c_col = c.reshape(d, 1)
    row = lambda t: t.reshape(1, -1)

    def mods(w3, b3, layer, n):
        m = adaln(c_col, w3, b3, layer)
        return [m[:, i * d:(i + 1) * d] for i in range(n)]

    ada_b3 = ada_b[:, None, :]
    conv_dim = gdn_conv.shape[2]
    gdn_main = conv_dim + GDN_V_HEADS * GDN_HEAD_DIM
    kvd = FOX_KV_HEADS * FOX_HEAD_DIM

    sh_m, sc_m, g_m, sh_f, sc_f, g_f = mods(ada_w, ada_b3, 0, 6)
    ffn_srcs = (ffn_w_in.reshape(-1, ffn_w_in.shape[2]), ffn_w_out.reshape(-1, d))
    proj, small, gdn_w_out_b = nm_proj_t(
        xs, row(norm_mix[0]), sh_m, sc_m, jnp.swapaxes(gdn_w_in, 1, 2), 0, gdn_main,
        gdn_w_in.shape[2] - gdn_main, cast_src=gdn_w_out.reshape(-1, d))
    pad32 = lambda t: jnp.pad(t, (GDN_V_HEADS, LANES - 2 * GDN_V_HEADS)).reshape(1, LANES)
    gt, gh = gdn_gates(small, pad32(gdn_a_log[0]), pad32(gdn_dt_bias[0]))
    o, w_in_b, w_out_b = gdn_core(proj, gdn_conv[0], gt, gh, row(gdn_norm[0]), ffn_srcs, 0)
    xs = proj_res(o, gdn_w_out_b[None], 0, xs, g_m)
    xs, fox_w_in_b = ffn(xs, row(norm_ffn[0]), sh_f, sc_f, g_f, w_in_b, w_out_b,
                         cast_src=fox_w_in.reshape(d, -1))

    sh_k, sc_k = mods(kv_ada_w[None], kv_ada_b[None, None, :], 0, 2)
    kvp, fl = nm_proj_t(xs, row(kv_norm), sh_k, sc_k, kv_w.T[None], 0, 2 * kvd,
                        kv_w.shape[1] - 2 * kvd)
    fb_row = jnp.pad(forget_b, (0, LANES - forget_b.shape[0])).reshape(1, LANES)
    k_sh, v_sh, fh = kv_prep(kvp, fl, row(k_norm), fb_row)

    sh_m, sc_m, g_m, sh_f, sc_f, g_f = mods(ada_w, ada_b3, 1, 6)
    qg, fox_w_out_b = nm_proj(xs, row(norm_mix[1]), sh_m, sc_m, fox_w_in_b[None], 0,
                              fox_w_out.reshape(-1, d))
    a, w_in_b, w_out_b = fox_attention(qg, k_sh, v_sh, fh, row(q_norm[0]), ffn_srcs, 1)
    xs = proj_res(a, fox_w_out_b[None], 0, xs, g_m)
    sh_o, sc_o = mods(out_ada_w[None], out_ada_b[None, None, :], 0, 2)
    xs = ffn(xs, row(norm_ffn[1]), sh_f, sc_f, g_f, w_in_b, w_out_b,
             final_mod=(row(out_norm), sh_o, sc_o))
    return xs.reshape(bsz, l, d)
```

```python
import functools
import math

import jax
import jax.numpy as jnp
from jax import lax
from jax.experimental import pallas as pl
from jax.experimental.pallas import tpu as pltpu

F32 = jnp.float32
BF16 = jnp.bfloat16
NORM_EPS = 1e-6
LOG2E = math.log2(math.e)

V7X_VMEM_BYTES = 64 * 1024 * 1024
VMEM_LIMIT_BYTES = V7X_VMEM_BYTES * 7 // 8
LANES = 128
SUBLANES = 8

GDN_CHUNK = 64
GDN_HEAD_DIM = 128
GDN_V_HEADS = 32
GDN_QK_HEADS = 16
CONV_K = 4
FOX_HEAD_DIM = 256
FOX_KV_HEADS = 2
FOX_GROUP = 8

NT_DIMS = (((1,), (1,)), ((), ()))
TN_DIMS = (((0,), (0,)), ((), ()))


def _cparams(sem):
    return pltpu.CompilerParams(dimension_semantics=sem, vmem_limit_bytes=VMEM_LIMIT_BYTES)


def _silu(x):
    h = 0.5 * x
    return h + h * jnp.tanh(h)


def _softplus(x):
    return jnp.maximum(x, 0.0) + jnp.log1p(jnp.exp(-jnp.abs(x)))


def _log_sigmoid(x):
    return jnp.minimum(x, 0.0) - jnp.log1p(jnp.exp(-jnp.abs(x)))


def _norm_mod(x, nw, sh, sc):
    ms = jnp.mean(x * x, axis=-1, keepdims=True)
    return x * lax.rsqrt(ms + NORM_EPS) * (nw * (1.0 + sc)) + sh


NORM_ROWS = 256


def _norm_mod_rows(dst_ref, src_fn, nw, sh, sc):
    tm = dst_ref.shape[0]
    step = min(NORM_ROWS, tm)
    for r in range(0, tm, step):
        rows = slice(r, r + step)
        dst_ref[rows, :] = _norm_mod(src_fn(rows), nw, sh, sc).astype(dst_ref.dtype)


def _adaln_kernel(c_ref, w_ref, b_ref, o_ref):
    cond = _silu(c_ref[...])
    o_ref[...] = jnp.sum(w_ref[...] * cond, axis=0, keepdims=True) + b_ref[...]


def adaln(c_col, w3, b3, tn=1024):
    nl, d, n = w3.shape
    tn = min(tn, n)
    return pl.pallas_call(
        _adaln_kernel,
        grid=(nl, n // tn),
        in_specs=[
            pl.BlockSpec((d, 1), lambda a, j: (0, 0)),
            pl.BlockSpec((None, d, tn), lambda a, j: (a, 0, j)),
            pl.BlockSpec((None, 1, tn), lambda a, j: (a, 0, j)),
        ],
        out_specs=pl.BlockSpec((None, 1, tn), lambda a, j: (a, 0, j)),
        out_shape=jax.ShapeDtypeStruct((nl, 1, n), F32),
        compiler_params=_cparams(("arbitrary", "arbitrary")),
        name="adaln",
    )(c_col, w3, b3)


def _nm_proj_kernel(x_ref, nw_ref, sh_ref, sc_ref, w_ref, c_ref, o_ref, co_ref, h_ref):
    _cast_passengers((c_ref,), (co_ref,))

    @pl.when(pl.program_id(1) == 0)
    def _():
        _norm_mod_rows(h_ref, lambda rows: x_ref[rows, :], nw_ref[...], sh_ref[...], sc_ref[...])

    o_ref[...] = jnp.dot(h_ref[...], w_ref[...], preferred_element_type=F32)


def nm_proj(x, nw, sh, sc, w3, layer, cast_src, tm=1024, tn=1024):
    l, d = x.shape
    n = w3.shape[2]
    tm, tn = min(tm, l), min(tn, n)
    nj = n // tn
    row = lambda i, j: (0, 0)
    c_in, c_out, c_shape = _cast_specs([cast_src], 0, (l // tm) * nj, lambda i, j: i * nj + j, 1)
    return pl.pallas_call(
        _nm_proj_kernel,
        grid=(l // tm, nj),
        in_specs=[
            pl.BlockSpec((tm, d), lambda i, j: (i, 0)),
            pl.BlockSpec((1, d), row), pl.BlockSpec((1, d), row), pl.BlockSpec((1, d), row),
            pl.BlockSpec((None, d, tn), lambda i, j: (layer, 0, j)),
        ] + c_in,
        out_specs=[pl.BlockSpec((tm, tn), lambda i, j: (i, j))] + c_out,
        out_shape=[jax.ShapeDtypeStruct((l, n), F32)] + c_shape,
        scratch_shapes=[pltpu.VMEM((tm, d), BF16)],
        compiler_params=_cparams(("arbitrary", "arbitrary")),
        name="nm_proj",
    )(x, nw, sh, sc, w3, cast_src)


def _nm_proj_t_kernel(x_ref, nw_ref, sh_ref, sc_ref, wt_ref, wst_ref, *rest, cast):
    if cast:
        c_ref, o_ref, os_ref, co_ref, h_ref = rest
        _cast_passengers((c_ref,), (co_ref,))
    else:
        o_ref, os_ref, h_ref = rest
    tm = o_ref.shape[0]

    @pl.when(pl.program_id(1) == 0)
    def _():
        _norm_mod_rows(h_ref, lambda rows: x_ref[rows, :], nw_ref[...], sh_ref[...], sc_ref[...])
        ns = wst_ref.shape[0]
        os_ref[:, 0:ns] = lax.dot_general(h_ref[...], wst_ref[...].astype(BF16), NT_DIMS,
                                          preferred_element_type=F32)
        os_ref[:, ns:] = jnp.zeros((tm, LANES - ns), F32)

    o_ref[...] = lax.dot_general(h_ref[...], wt_ref[...].astype(BF16), NT_DIMS,
                                 preferred_element_type=F32)


def nm_proj_t(x, nw, sh, sc, wt3, layer, n, ns, cast_src=None, tm=1024, tn=1024):
    l, d = x.shape
    tm, tn = min(tm, l), min(tn, n)
    assert n % tn == 0 and n % ns == 0
    nj = n // tn
    cast = cast_src is not None
    row = lambda i, j: (0, 0)
    in_specs = [
        pl.BlockSpec((tm, d), lambda i, j: (i, 0)),
        pl.BlockSpec((1, d), row), pl.BlockSpec((1, d), row), pl.BlockSpec((1, d), row),
        pl.BlockSpec((None, tn, d), lambda i, j: (layer, j, 0)),
        pl.BlockSpec((None, ns, d), lambda i, j: (layer, n // ns, 0)),
    ]
    out_specs = [pl.BlockSpec((tm, tn), lambda i, j: (i, j)),
                 pl.BlockSpec((tm, LANES), lambda i, j: (i, 0))]
    out_shape = [jax.ShapeDtypeStruct((l, n), F32), jax.ShapeDtypeStruct((l, LANES), F32)]
    args = [x, nw, sh, sc, wt3, wt3]
    if cast:
        c_in, c_out, c_shape = _cast_specs([cast_src], 0, (l // tm) * nj,
                                           lambda i, j: i * nj + j, 1)
        in_specs, out_specs, out_shape = in_specs + c_in, out_specs + c_out, out_shape + c_shape
        args.append(cast_src)
    return pl.pallas_call(
        functools.partial(_nm_proj_t_kernel, cast=cast),
        grid=(l // tm, nj),
        in_specs=in_specs,
        out_specs=out_specs,
        out_shape=out_shape,
        scratch_shapes=[pltpu.VMEM((tm, d), BF16)],
        compiler_params=_cparams(("arbitrary", "arbitrary")),
        name="nm_proj_t",
    )(*args)


def _proj_res_kernel(a_ref, w_ref, x_ref, g_ref, o_ref):
    y = jnp.dot(a_ref[...], w_ref[...], preferred_element_type=F32)
    o_ref[...] = x_ref[...] + g_ref[...] * y


def proj_res(a, w3, layer, x, g, tm=1024, tn=1024):
    l, k = a.shape
    d = w3.shape[2]
    tm, tn = min(tm, l), min(tn, d)
    return pl.pallas_call(
        _proj_res_kernel,
        grid=(l // tm, d // tn),
        in_specs=[
            pl.BlockSpec((tm, k), lambda i, j: (i, 0)),
            pl.BlockSpec((None, k, tn), lambda i, j: (layer, 0, j)),
            pl.BlockSpec((tm, tn), lambda i, j: (i, j)),
            pl.BlockSpec((1, tn), lambda i, j: (0, j)),
        ],
        out_specs=pl.BlockSpec((tm, tn), lambda i, j: (i, j)),
        out_shape=jax.ShapeDtypeStruct((l, d), F32),
        compiler_params=_cparams(("arbitrary", "arbitrary")),
        name="proj_res",
    )(a, w3, x, g)


def _ffn_kernel(x_ref, nw_ref, sh_ref, sc_ref, g_ref, wg_ref, wu_ref, wo_ref, *rest, final, cast):
    rest = list(rest)
    fnw_ref, fsh_ref, fsc_ref = (rest.pop(0), rest.pop(0), rest.pop(0)) if final else (None,) * 3
    c_ref = rest.pop(0) if cast else None
    o_ref = rest.pop(0)
    co_ref = rest.pop(0) if cast else None
    h_ref, acc_ref = rest
    if cast:
        _cast_passengers((c_ref,), (co_ref,))
    j = pl.program_id(1)

    @pl.when(j == 0)
    def _():
        _norm_mod_rows(h_ref, lambda rows: x_ref[rows, :], nw_ref[...], sh_ref[...], sc_ref[...])
        acc_ref[...] = jnp.zeros_like(acc_ref)

    h = h_ref[...]
    gate = jnp.dot(h, wg_ref[...], preferred_element_type=F32)
    up = jnp.dot(h, wu_ref[...], preferred_element_type=F32)
    act = (_silu(gate) * up).astype(BF16)
    acc_ref[...] += jnp.dot(act, wo_ref[...], preferred_element_type=F32)

    @pl.when(j == pl.num_programs(1) - 1)
    def _():
        res = lambda rows: x_ref[rows, :] + g_ref[...] * acc_ref[rows, :]
        if final:
            _norm_mod_rows(o_ref, res, fnw_ref[...], fsh_ref[...], fsc_ref[...])
        else:
            o_ref[...] = res(slice(None))


def ffn(x, nw, sh, sc, g, w_in, w_out, final_mod=None, cast_src=None, tm=512, th=512):
    l, d = x.shape
    hdim = w_out.shape[0]
    tm, th = min(tm, l), min(th, hdim)
    nh = hdim // th
    final, cast = final_mod is not None, cast_src is not None
    row = lambda i, j: (0, 0)
    vec = pl.BlockSpec((1, d), row)
    in_specs = [
        pl.BlockSpec((tm, d), lambda i, j: (i, 0)),
        vec, vec, vec, vec,
        pl.BlockSpec((d, th), lambda i, j: (0, j)),
        pl.BlockSpec((d, th), lambda i, j: (0, j + nh)),
        pl.BlockSpec((th, d), lambda i, j: (j, 0)),
    ]
    out_specs = [pl.BlockSpec((tm, d), lambda i, j: (i, 0))]
    out_shape = [jax.ShapeDtypeStruct((l, d), F32)]
    args = [x, nw, sh, sc, g, w_in, w_in, w_out]
    if final:
        in_specs += [vec, vec, vec]
        args += list(final_mod)
    if cast:
        c_in, c_out, c_shape = _cast_specs([cast_src], 0, (l // tm) * nh,
                                           lambda i, j: i * nh + j, 1)
        in_specs, out_specs, out_shape = in_specs + c_in, out_specs + c_out, out_shape + c_shape
        args.append(cast_src)
    out = pl.pallas_call(
        functools.partial(_ffn_kernel, final=final, cast=cast),
        grid=(l // tm, nh),
        in_specs=in_specs,
        out_specs=out_specs,
        out_shape=out_shape,
        scratch_shapes=[pltpu.VMEM((tm, d), BF16), pltpu.VMEM((tm, d), F32)],
        compiler_params=_cparams(("arbitrary", "arbitrary")),
        name="ffn",
    )(*args)
    return out if cast else out[0]


def _gdn_gates_kernel(s_ref, alog_ref, dt_ref, gt_ref, gh_ref):
    x = s_ref[...]
    tm = x.shape[0]
    lane = lax.broadcasted_iota(jnp.int32, x.shape, 1)
    beta = jax.nn.sigmoid(x)
    g = -jnp.exp(alog_ref[...]) * _softplus(x + dt_ref[...])
    g = jnp.where((lane >= GDN_V_HEADS) & (lane < 2 * GDN_V_HEADS), g, 0.0)
    r = lax.broadcasted_iota(jnp.int32, (tm, tm), 0)
    c = lax.broadcasted_iota(jnp.int32, (tm, tm), 1)
    same_chunk = (r // GDN_CHUNK) == (c // GDN_CHUNK)
    tril = jnp.where((r >= c) & same_chunk, 1.0, 0.0).astype(F32)
    gcum = jnp.dot(tril, g, preferred_element_type=F32, precision=lax.Precision.HIGHEST)
    gt = jnp.where(lane < GDN_V_HEADS, beta, gcum)
    gt_ref[...] = gt
    gh_ref[...] = gt.T


def gdn_gates(small, alog_row, dt_row, tm=512):
    l = small.shape[0]
    tm = min(tm, l)
    row = lambda i: (0, 0)
    return pl.pallas_call(
        _gdn_gates_kernel,
        grid=(l // tm,),
        in_specs=[pl.BlockSpec((tm, LANES), lambda i: (i, 0)),
                  pl.BlockSpec((1, LANES), row), pl.BlockSpec((1, LANES), row)],
        out_specs=[pl.BlockSpec((tm, LANES), lambda i: (i, 0)),
                   pl.BlockSpec((LANES, tm), lambda i: (0, i))],
        out_shape=[jax.ShapeDtypeStruct((l, LANES), F32), jax.ShapeDtypeStruct((LANES, l), F32)],
        compiler_params=_cparams(("arbitrary",)),
        name="gdn_gates",
    )(small, alog_row, dt_row)


GDN_PAIRS = 8


def _conv_silu(x_ref, xe_ref, w_ref):
    tb = x_ref.shape[0]
    pad = SUBLANES
    xe_ref[pad:pad + tb, :] = x_ref[...]
    w = w_ref[...]
    acc = w[CONV_K - 1:CONV_K, :] * xe_ref[pad:pad + tb, :]
    for i in range(CONV_K - 1):
        off = pad - (CONV_K - 1) + i
        acc = acc + w[i:i + 1, :] * xe_ref[off:off + tb, :]
    xe_ref[0:pad, :] = xe_ref[tb:tb + pad, :]
    return _silu(acc)


def _cast_passengers(in_refs, out_refs):
    for src, dst in zip(in_refs, out_refs):
        dst[...] = src[...].astype(dst.dtype)


def _cast_specs(arrays, layer, n_steps, step_of, n_layers=2):
    in_specs, out_specs, out_shape = [], [], []
    bf16_rows = 16
    for a in arrays:
        rows = a.shape[0] // n_layers
        assert a.shape[0] % n_layers == 0 and rows % bf16_rows == 0
        n_blk = max(n for n in range(1, n_steps + 1) if (rows // bf16_rows) % n == 0)
        blk = rows // n_blk
        blk_of = lambda *g, _n=n_blk: jnp.minimum(step_of(*g), _n - 1)
        in_specs.append(pl.BlockSpec((blk, a.shape[1]),
                                     lambda *g, _n=n_blk, _b=blk_of: (layer * _n + _b(*g), 0)))
        out_specs.append(pl.BlockSpec((blk, a.shape[1]), lambda *g, _b=blk_of: (_b(*g), 0)))
        out_shape.append(jax.ShapeDtypeStruct((rows, a.shape[1]), BF16))
    return in_specs, out_specs, out_shape


def _gdn_kernel(q_ref, k_ref, v_ref, z_ref, wq_ref, wk_ref, wv_ref, gt_ref, gh_ref, nw_ref,
                c0_ref, c1_ref, o_ref, co0_ref, co1_ref, xq_ref, xk_ref, xv_ref, s_ref, *, pairs):
    _cast_passengers((c0_ref, c1_ref), (co0_ref, co1_ref))
    gi = pl.program_id(0)
    tb = q_ref.shape[0]
    hd = GDN_HEAD_DIM
    ck = GDN_CHUNK
    nck = tb // ck
    n_sq = int(math.log2(ck)) - 1

    @pl.when(pl.program_id(1) == 0)
    def _():
        for xe_ref in (xq_ref, xk_ref, xv_ref):
            xe_ref[0:SUBLANES, :] = jnp.zeros((SUBLANES, xe_ref.shape[1]), F32)
        s_ref[...] = jnp.zeros_like(s_ref)

    q_all = _conv_silu(q_ref, xq_ref, wq_ref)
    k_all = _conv_silu(k_ref, xk_ref, wk_ref)
    v_all = _conv_silu(v_ref, xv_ref, wv_ref)

    head0 = 2 * pairs * gi
    gt = pltpu.roll(gt_ref[...], lax.rem(LANES - head0, LANES), axis=1)
    lt = LANES
    r_l = lax.broadcasted_iota(jnp.int32, (lt, lt), 0)
    c_l = lax.broadcasted_iota(jnp.int32, (lt, lt), 1)
    same_l = (r_l // ck) == (c_l // ck)
    same_lb = jnp.where(same_l, 1.0, 0.0).astype(BF16)
    causal = same_l & (r_l >= c_l)
    strict = same_l & (r_l > c_l)
    tiles = [slice(t * lt, (t + 1) * lt) for t in range(tb // lt)]
    nw = nw_ref[...]

    def block_diag(packed):
        pb = packed.astype(BF16)
        zero = jnp.zeros((lt, lt), BF16)
        rows = []
        for ti, t in enumerate(tiles):
            blk = jnp.concatenate([pb[:, t]] * (lt // ck), axis=0) * same_lb
            rows.append(jnp.concatenate([blk if u == ti else zero for u in range(len(tiles))],
                                        axis=1))
        return jnp.concatenate(rows, axis=0)

    heads = range(2 * pairs)
    qs, ks, kks, qks = [], [], [], []
    for pi in range(pairs):
        q = q_all[:, pi * hd:(pi + 1) * hd]
        k = k_all[:, pi * hd:(pi + 1) * hd]
        q = q * (lax.rsqrt(jnp.sum(q * q, axis=-1, keepdims=True) + NORM_EPS) * hd ** -0.5)
        k = k * lax.rsqrt(jnp.sum(k * k, axis=-1, keepdims=True) + NORM_EPS)
        kb = k.astype(BF16)
        qb = q.astype(BF16)
        qs.append(q)
        ks.append(k)
        kks.append([lax.dot_general(kb[t], kb[t], NT_DIMS, preferred_element_type=F32)
                    for t in tiles])
        qks.append([lax.dot_general(qb[t], kb[t], NT_DIMS, preferred_element_type=F32)
                    for t in tiles])

    bcs, gccs, attns, ps = [], [], [], []
    for h in heads:
        bc = gt[:, h:h + 1]
        gcc = gt[:, GDN_V_HEADS + h:GDN_V_HEADS + h + 1]
        gcr = gh_ref[pl.ds(GDN_V_HEADS + head0 + h, 1), :]
        attn_chunks, p_parts = [], []
        for ti, t in enumerate(tiles):
            decay = jnp.exp(jnp.where(causal, gcc[t] - gcr[:, t], -jnp.inf))
            a_t = jnp.where(strict, kks[h // 2][ti] * decay * bc[t], 0.0)
            attn_t = qks[h // 2][ti] * decay
            part = -a_t[0:ck]
            for c in range(lt // ck):
                attn_chunks.append(attn_t[c * ck:(c + 1) * ck, c * ck:(c + 1) * ck].astype(BF16))
                if c:
                    part = part - a_t[c * ck:(c + 1) * ck]
            p_parts.append(part)
        attns.append(attn_chunks)
        bcs.append(bc)
        gccs.append(gcc)
        ps.append(jnp.concatenate(p_parts, axis=1))

    rs = list(ps)
    p_bds = [block_diag(p) for p in ps]
    for _ in range(n_sq):
        ps = [jnp.dot(ps[h].astype(BF16), p_bds[h], preferred_element_type=F32) for h in heads]
        p_bds = [block_diag(p) for p in ps]
        rs = [rs[h] + ps[h] + jnp.dot(rs[h].astype(BF16), p_bds[h], preferred_element_type=F32)
              for h in heads]

    egs = [jnp.exp(gcc) for gcc in gccs]
    us, wqs = [], []
    for h in heads:
        k = ks[h // 2]
        rhs = bcs[h] * jnp.concatenate([v_all[:, h * hd:(h + 1) * hd], k * egs[h]], axis=1)
        uw = rhs + jnp.dot(block_diag(rs[h]), rhs.astype(BF16), preferred_element_type=F32)
        us.append(uw[:, :hd])
        q_dec = qs[h // 2] * egs[h]
        wqs.append([jnp.concatenate([uw[c * ck:(c + 1) * ck, hd:], q_dec[c * ck:(c + 1) * ck]],
                                    axis=0).astype(BF16) for c in range(nck)])

    states = [s_ref[h] for h in heads]
    outs = [[] for _ in heads]
    for ci in range(nck):
        rows = slice(ci * ck, (ci + 1) * ck)
        wq_s = [jnp.dot(wqs[h][ci], states[h].astype(BF16), preferred_element_type=F32)
                for h in heads]
        v_news = [(us[h][rows] - wq_s[h][:ck]).astype(BF16) for h in heads]
        for h in heads:
            outs[h].append(wq_s[h][ck:] + jnp.dot(attns[h][ci], v_news[h],
                                                  preferred_element_type=F32))
        for h in heads:
            g_last = gccs[h][(ci + 1) * ck - 1:(ci + 1) * ck, :]
            k_dec = (ks[h // 2][rows] * jnp.exp(g_last - gccs[h][rows])).astype(BF16)
            states[h] = states[h] * jnp.exp(g_last) + lax.dot_general(
                k_dec, v_news[h], TN_DIMS, preferred_element_type=F32)

    for h in heads:
        s_ref[h] = states[h]
        o = jnp.concatenate(outs[h], axis=0)
        on = o * lax.rsqrt(jnp.mean(o * o, axis=-1, keepdims=True) + NORM_EPS) * nw
        cols = slice(h * hd, (h + 1) * hd)
        o_ref[:, cols] = (on * _silu(z_ref[:, cols])).astype(o_ref.dtype)


def gdn_core(proj, conv_w, gt, gh, norm_w, cast_srcs, cast_layer, tb=256, pairs=GDN_PAIRS):
    l = proj.shape[0]
    hd = GDN_HEAD_DIM
    tb = min(tb, l)
    ng = GDN_QK_HEADS // pairs
    nb = l // tb
    qw, vw = pairs * hd, 2 * pairs * hd
    c_in, c_out, c_shape = _cast_specs(cast_srcs, cast_layer, ng * nb, lambda g, b: g * nb + b)
    return pl.pallas_call(
        functools.partial(_gdn_kernel, pairs=pairs),
        grid=(ng, nb),
        in_specs=[
            pl.BlockSpec((tb, qw), lambda g, b: (b, g)),
            pl.BlockSpec((tb, qw), lambda g, b: (b, ng + g)),
            pl.BlockSpec((tb, vw), lambda g, b: (b, ng + g)),
            pl.BlockSpec((tb, vw), lambda g, b: (b, 2 * ng + g)),
            pl.BlockSpec((CONV_K, qw), lambda g, b: (0, g)),
            pl.BlockSpec((CONV_K, qw), lambda g, b: (0, ng + g)),
            pl.BlockSpec((CONV_K, vw), lambda g, b: (0, ng + g)),
            pl.BlockSpec((tb, LANES), lambda g, b: (b, 0)),
            pl.BlockSpec((LANES, tb), lambda g, b: (0, b)),
            pl.BlockSpec((1, hd), lambda g, b: (0, 0)),
        ] + c_in,
        out_specs=[pl.BlockSpec((tb, vw), lambda g, b: (b, g))] + c_out,
        out_shape=[jax.ShapeDtypeStruct((l, GDN_V_HEADS * hd), BF16)] + c_shape,
        scratch_shapes=[pltpu.VMEM((tb + SUBLANES, qw), F32),
                        pltpu.VMEM((tb + SUBLANES, qw), F32),
                        pltpu.VMEM((tb + SUBLANES, vw), F32),
                        pltpu.VMEM((2 * pairs, hd, hd), F32)],
        compiler_params=_cparams(("arbitrary", "arbitrary")),
        name="gdn_core",
    )(proj, proj, proj, proj, conv_w, conv_w, conv_w, gt, gh, norm_w, *cast_srcs)


def _kv_prep_kernel(kv_ref, fl_ref, knw_ref, fb_ref, kt_ref, v_ref, fh_ref, carry_ref):
    hd = FOX_HEAD_DIM
    nkv = FOX_KV_HEADS
    tm = kv_ref.shape[0]

    @pl.when(pl.program_id(0) == 0)
    def _():
        carry_ref[...] = jnp.zeros_like(carry_ref)

    for h in range(nkv):
        kh = kv_ref[:, h * hd:(h + 1) * hd]
        ms = jnp.mean(kh * kh, axis=-1, keepdims=True)
        kn = kh * lax.rsqrt(ms + NORM_EPS) * knw_ref[...]
        kt_ref[h * hd:(h + 1) * hd, :] = kn.T.astype(BF16)
    v_ref[...] = kv_ref[:, nkv * hd:2 * nkv * hd].astype(BF16)

    log_f = _log_sigmoid(fl_ref[...] + fb_ref[...])
    r = lax.broadcasted_iota(jnp.int32, (tm, tm), 0)
    c = lax.broadcasted_iota(jnp.int32, (tm, tm), 1)
    tril = jnp.where(r >= c, 1.0, 0.0).astype(F32)
    cs = jnp.dot(tril, log_f, preferred_element_type=F32,
                 precision=lax.Precision.HIGHEST) + carry_ref[...]
    carry_ref[...] = cs[tm - 1:tm, :]
    fh_ref[...] = (cs * LOG2E).T[0:fh_ref.shape[0], :]


def kv_prep(kv, fl, knw, fb_row, tm=512):
    l = kv.shape[0]
    tm = min(tm, l)
    kvd = FOX_KV_HEADS * FOX_HEAD_DIM
    nh = FOX_KV_HEADS * FOX_GROUP
    row = lambda i: (0, 0)
    return pl.pallas_call(
        _kv_prep_kernel,
        grid=(l // tm,),
        in_specs=[pl.BlockSpec((tm, 2 * kvd), lambda i: (i, 0)),
                  pl.BlockSpec((tm, LANES), lambda i: (i, 0)),
                  pl.BlockSpec((1, FOX_HEAD_DIM), row), pl.BlockSpec((1, LANES), row)],
        out_specs=[pl.BlockSpec((kvd, tm), lambda i: (0, i)),
                   pl.BlockSpec((tm, kvd), lambda i: (i, 0)),
                   pl.BlockSpec((nh, tm), lambda i: (0, i))],
        out_shape=[jax.ShapeDtypeStruct((kvd, l), BF16), jax.ShapeDtypeStruct((l, kvd), BF16),
                   jax.ShapeDtypeStruct((nh, l), F32)],
        scratch_shapes=[pltpu.VMEM((1, LANES), F32)],
        compiler_params=_cparams(("arbitrary",)),
        name="kv_prep",
    )(kv, fl, knw, fb_row)


def _fox_kernel(q_ref, gate_ref, kt_ref, v_ref, fh_ref, qnw_ref, c0_ref, c1_ref, o_ref, co0_ref,
                co1_ref, qn_ref, m_ref, l_ref, acc_ref, *, tk):
    _cast_passengers((c0_ref, c1_ref), (co0_ref, co1_ref))
    i = pl.program_id(1)
    tq = q_ref.shape[0]
    hd = FOX_HEAD_DIM
    scale = hd ** -0.5 * LOG2E
    for g in range(FOX_GROUP):
        qh = q_ref[:, g * hd:(g + 1) * hd]
        ms = jnp.mean(qh * qh, axis=-1, keepdims=True)
        qn_ref[g] = (qh * (lax.rsqrt(ms + NORM_EPS) * scale) * qnw_ref[...]).astype(BF16)
    m_ref[...] = jnp.full(m_ref.shape, -jnp.inf, F32)
    l_ref[...] = jnp.zeros_like(l_ref)
    acc_ref[...] = jnp.zeros_like(acc_ref)
    tkh = tk // 2
    n_full = (i * tq + 1) // tk
    diag0 = n_full * tk
    extra = (i * tq + 1 - diag0) // tkh
    diag_start = diag0 + extra * tkh

    def tile(start, width, mask_offset=None):
        start = pl.multiple_of(start, width)
        kj = kt_ref[:, pl.ds(start, width)]
        vj = v_ref[pl.ds(start, width), :]
        heads = range(FOX_GROUP)
        ss = [jnp.dot(qn_ref[g], kj, preferred_element_type=F32)
              - fh_ref[g:g + 1, pl.ds(start, width)] for g in heads]
        if mask_offset is not None:
            causal = (lax.broadcasted_iota(jnp.int32, (tq, width), 1)
                      - lax.broadcasted_iota(jnp.int32, (tq, width), 0)) <= mask_offset
            ss = [jnp.where(causal, s, -jnp.inf) for s in ss]
        sc = [[s[:, c * LANES:(c + 1) * LANES] for c in range(width // LANES)] for s in ss]
        m_olds = [m_ref[g] for g in heads]
        m_news = [jnp.maximum(m_olds[g], jnp.max(functools.reduce(jnp.maximum, sc[g]),
                                                 axis=-1, keepdims=True)) for g in heads]
        alphas = [jnp.exp2(m_olds[g] - m_news[g]) for g in heads]
        pc = [[jnp.exp2(s - m_news[g]) for s in sc[g]] for g in heads]
        for g in heads:
            row_sum = jnp.sum(functools.reduce(jnp.add, pc[g]), axis=-1, keepdims=True)
            l_ref[g] = alphas[g] * l_ref[g] + row_sum
            m_ref[g] = m_news[g]
        pvs = [jnp.dot(jnp.concatenate(pc[g], axis=1).astype(BF16), vj,
                       preferred_element_type=F32) for g in heads]
        for g in heads:
            for c in range(hd // LANES):
                cols = slice(c * LANES, (c + 1) * LANES)
                acc_ref[g, :, cols] = alphas[g] * acc_ref[g, :, cols] + pvs[g][:, cols]

    def body(j, carry):
        tile(j * tk, tk)
        return carry

    lax.fori_loop(0, n_full, body, 0)

    @pl.when(extra > 0)
    def _():
        tile(diag0, tkh)

    tile(diag_start, tkh, mask_offset=i * tq - diag_start)
    for g in range(FOX_GROUP):
        half_inv_l = 0.5 / l_ref[g]
        for c in range(hd // LANES):
            cols = slice(g * hd + c * LANES, g * hd + (c + 1) * LANES)
            o_ref[:, cols] = (acc_ref[g, :, c * LANES:(c + 1) * LANES] * half_inv_l
                              * (1.0 + jnp.tanh(0.5 * gate_ref[:, cols]))).astype(o_ref.dtype)


def fox_attention(qg, kt, v, fh, qnw, cast_srcs, cast_layer, tq=256, tk=1024):
    l = qg.shape[0]
    hd = FOX_HEAD_DIM
    gw = FOX_GROUP * hd
    tk = min(tk, l)
    tq = min(tq, tk // 2)
    assert tk % (2 * tq) == 0 and l % tk == 0
    nq = l // tq
    c_in, c_out, c_shape = _cast_specs(cast_srcs, cast_layer, FOX_KV_HEADS * nq,
                                       lambda h, i: h * nq + i)
    return pl.pallas_call(
        functools.partial(_fox_kernel, tk=tk),
        grid=(FOX_KV_HEADS, l // tq),
        in_specs=[
            pl.BlockSpec((tq, gw), lambda h, i: (i, h)),
            pl.BlockSpec((tq, gw), lambda h, i: (i, FOX_KV_HEADS + h)),
            pl.BlockSpec((hd, l), lambda h, i: (h, 0), pipeline_mode=pl.Buffered(1)),
            pl.BlockSpec((l, hd), lambda h, i: (0, h), pipeline_mode=pl.Buffered(1)),
            pl.BlockSpec((FOX_GROUP, l), lambda h, i: (h, 0)),
            pl.BlockSpec((1, hd), lambda h, i: (0, 0)),
        ] + c_in,
        out_specs=[pl.BlockSpec((tq, gw), lambda h, i: (i, h))] + c_out,
        out_shape=[jax.ShapeDtypeStruct((l, FOX_KV_HEADS * gw), BF16)] + c_shape,
        scratch_shapes=[pltpu.VMEM((FOX_GROUP, tq, hd), BF16),
                        pltpu.VMEM((FOX_GROUP, tq, LANES), F32),
                        pltpu.VMEM((FOX_GROUP, tq, LANES), F32),
                        pltpu.VMEM((FOX_GROUP, tq, hd), F32)],
        compiler_params=_cparams(("arbitrary", "arbitrary")),
        name="fox_attention",
    )(qg, qg, kt, v, fh, qnw, *cast_srcs)


def kernel(x, c, ada_w, ada_b, norm_mix, norm_ffn, ffn_w_in, ffn_w_out, gdn_w_in, gdn_conv,
           gdn_a_log, gdn_dt_bias, gdn_norm, gdn_w_out, kv_ada_w, kv_ada_b, kv_norm, kv_w, k_norm,
           forget_b, fox_w_in, q_norm, fox_w_out, out_ada_w, out_ada_b, out_norm):
    bsz, l, d = x.shape
    assert bsz == 1 and ada_w.shape[0] == 2 and gdn_w_in.shape[0] == 1 and fox_w_in.shape[0] == 1
    xs = x.reshape(l, d)
    c_col = c.reshape(d, 1)
    row = lambda t: t.reshape(1, -1)

    split = lambda m, n: [m[:, i * d:(i + 1) * d] for i in range(n)]
    layer_mods = adaln(c_col, ada_w, ada_b[:, None, :])
    mods = lambda w3, b3, layer, n: split(
        layer_mods[layer] if w3 is ada_w else adaln(c_col, w3, b3)[layer], n)
    ada_b3 = None
    conv_dim = gdn_conv.shape[2]
    gdn_main = conv_dim + GDN_V_HEADS * GDN_HEAD_DIM
    kvd = FOX_KV_HEADS * FOX_HEAD_DIM

    sh_m, sc_m, g_m, sh_f, sc_f, g_f = mods(ada_w, ada_b3, 0, 6)
    ffn_srcs = (ffn_w_in.reshape(-1, ffn_w_in.shape[2]), ffn_w_out.reshape(-1, d))
    proj, small, gdn_w_out_b = nm_proj_t(
        xs, row(norm_mix[0]), sh_m, sc_m, jnp.swapaxes(gdn_w_in, 1, 2), 0, gdn_main,
        gdn_w_in.shape[2] - gdn_main, cast_src=gdn_w_out.reshape(-1, d))
    pad32 = lambda t: jnp.pad(t, (GDN_V_HEADS, LANES - 2 * GDN_V_HEADS)).reshape(1, LANES)
    gt, gh = gdn_gates(small, pad32(gdn_a_log[0]), pad32(gdn_dt_bias[0]))
    o, w_in_b, w_out_b = gdn_core(proj, gdn_conv[0], gt, gh, row(gdn_norm[0]), ffn_srcs, 0)
    xs = proj_res(o, gdn_w_out_b[None], 0, xs, g_m)
    xs, fox_w_in_b = ffn(xs, row(norm_ffn[0]), sh_f, sc_f, g_f, w_in_b, w_out_b,
                         cast_src=fox_w_in.reshape(d, -1))

    sh_k, sc_k = mods(kv_ada_w[None], kv_ada_b[None, None, :], 0, 2)
    kvp, fl = nm_proj_t(xs, row(kv_norm), sh_k, sc_k, kv_w.T[None], 0, 2 * kvd,
                        kv_w.shape[1] - 2 * kvd)
    fb_row = jnp.pad(forget_b, (0, LANES - forget_b.shape[0])).reshape(1, LANES)
    k_sh, v_sh, fh = kv_prep(kvp, fl, row(k_norm), fb_row)

    sh_m, sc_m, g_m, sh_f, sc_f, g_f = mods(ada_w, ada_b3, 1, 6)
    qg, fox_w_out_b = nm_proj(xs, row(norm_mix[1]), sh_m, sc_m, fox_w_in_b[None], 0,
                              fox_w_out.reshape(-1, d))
    a, w_in_b, w_out_b = fox_attention(qg, k_sh, v_sh, fh, row(q_norm[0]), ffn_srcs, 1)
    xs = proj_res(a, fox_w_out_b[None], 0, xs, g_m)
    sh_o, sc_o = mods(out_ada_w[None], out_ada_b[None, None, :], 0, 2)
    xs = ffn(xs, row(norm_ffn[1]), sh_f, sc_f, g_f, w_in_b, w_out_b,
             final_mod=(row(out_norm), sh_o, sc_o))
    return xs.reshape(bsz, l, d)
```
